```python
import math
import jax, jax.numpy as jnp
from jax import lax
import numpy as np

D_MODEL = 1024
BATCH = 32
SEQ = 256
DEPTH = 1
DEC_BATCH = 4
DEC_SEQ = 4096
PAST_LEN = 256

GRID_W = 64
N_HEADS = 8
HEAD_DIM = 64
ATTN_W = N_HEADS * 2 * HEAD_DIM
CONV_W = 512
CONV_K = 31
N_EXPERTS = 256
TOP_K = 8
N_GROUPS = 8
TOPK_GROUPS = 4
EXPERT_HIDDEN = 256
SHARED_HIDDEN = 256
ROUTE_SCALE = 2.5
ROPE_BASE = 10000.0
Q_BLOCK = 128
MOE_BLOCK = 128
EPS = 1e-6
N_MOD = 6
IN_COLS = 3 * ATTN_W + 2 * CONV_W + 2 * D_MODEL

kernel_name = "hybrid_diffattn_conformer_moe_dit_step"


def rmsnorm(x, g):
    x32 = x.astype(jnp.float32)
    y = x32 * lax.rsqrt(jnp.mean(x32 * x32, axis=-1, keepdims=True) + EPS)
    return (y * g.astype(jnp.float32)).astype(x.dtype)


def layernorm(x, g, b):
    x32 = x.astype(jnp.float32)
    mu = jnp.mean(x32, axis=-1, keepdims=True)
    xc = x32 - mu
    y = xc * lax.rsqrt(jnp.mean(xc * xc, axis=-1, keepdims=True) + EPS)
    return (y * g.astype(jnp.float32) + b.astype(jnp.float32)).astype(x.dtype)


def modulate(h, shift, scale):
    return h * (1 + scale) + shift


def axial_rope(x, n_tokens):
    t = jnp.arange(n_tokens, dtype=jnp.int32)
    pos = jnp.stack([t // GRID_W, t % GRID_W], axis=-1).astype(jnp.float32)
    half = HEAD_DIM // 2
    inv = ROPE_BASE ** (-jnp.arange(0, half, 2, dtype=jnp.float32) / half)
    ang = pos[:, :, None] * inv
    cos = jnp.cos(ang)[:, None, None]
    sin = jnp.sin(ang)[:, None, None]
    xr = x.astype(jnp.float32).reshape(x.shape[:-1] + (2, 2, half // 2))
    x1, x2 = xr[..., 0, :], xr[..., 1, :]
    out = jnp.stack([x1 * cos - x2 * sin, x2 * cos + x1 * sin], axis=-2)
    return out.reshape(x.shape).astype(x.dtype)


def split_proj(h, w_in):
    B, N = h.shape[:2]
    p = h @ w_in
    q, k, v, u, gates = jnp.split(p, [ATTN_W, 2 * ATTN_W, 3 * ATTN_W, 3 * ATTN_W + 2 * CONV_W], axis=-1)
    q = q.reshape(B, N, N_HEADS, 2, HEAD_DIM)
    k = k.reshape(B, N, N_HEADS, 2, HEAD_DIM)
    v = v.reshape(B, N, N_HEADS, 2 * HEAD_DIM)
    return q, k, v, u, gates


def diff_attention(q, k, v, lam):
    B, Nq = q.shape[:2]
    nb = Nq // Q_BLOCK
    qb = jnp.moveaxis(q.reshape((B, nb, Q_BLOCK) + q.shape[2:]), 1, 0)
    scale = HEAD_DIM ** -0.5

    def block(qblk):
        s = jnp.einsum('bqhcd,bkhcd->bchqk', qblk, k, preferred_element_type=jnp.float32) * scale
        p = jax.nn.softmax(s, axis=-1)
        w = p[:, 0] - lam * p[:, 1]
        return jnp.einsum('bhqk,bkhe->bqhe', w.astype(v.dtype), v)

    o = lax.map(block, qb)
    return jnp.moveaxis(o, 0, 1).reshape(B, Nq, N_HEADS, 2 * HEAD_DIM)


def conv_module(u, conv_w, conv_b, ln_g, ln_b, w_conv_out):
    a, b = jnp.split(u, 2, axis=-1)
    z = a * jax.nn.sigmoid(b)
    z = lax.conv_general_dilated(z, conv_w[:, None, :].astype(z.dtype), window_strides=(1,),
                                 padding=[(CONV_K // 2, CONV_K // 2)],
                                 dimension_numbers=('NWC', 'WIO', 'NWC'),
                                 feature_group_count=CONV_W) + conv_b
    z = jax.nn.silu(layernorm(z, ln_g, ln_b))
    return z @ w_conv_out


def merge_branches(o, conv_out, gates, subln_g, lam_init, w_attn_proj, w_out):
    B, N = o.shape[:2]
    o = rmsnorm(o, subln_g) * (1.0 - lam_init)
    a_br = o.reshape(B, N, ATTN_W) @ w_attn_proj
    g_a, g_c = jnp.split(jax.nn.sigmoid(gates), 2, axis=-1)
    return (g_a * a_br + g_c * conv_out) @ w_out


def moe_ffn(h, w_router, router_bias, w1, w3, w2, ws1, ws3, ws2):
    shp = h.shape
    x = h.reshape(-1, D_MODEL)
    T = x.shape[0]
    scores = jax.nn.sigmoid(jnp.matmul(x, w_router, preferred_element_type=jnp.float32))
    biased = scores + router_bias.astype(jnp.float32)
    grp = biased.reshape(T, N_GROUPS, N_EXPERTS // N_GROUPS)
    grp_score = jnp.sum(lax.top_k(grp, 2)[0], axis=-1)
    _, gidx = lax.top_k(grp_score, TOPK_GROUPS)
    gmask = jnp.sum(jax.nn.one_hot(gidx, N_GROUPS, dtype=jnp.float32), axis=1) > 0
    masked = jnp.where(jnp.repeat(gmask, N_EXPERTS // N_GROUPS, axis=1), biased, -jnp.inf)
    _, idx = lax.top_k(masked, TOP_K)
    g = jnp.take_along_axis(scores, idx, axis=1)
    g = g / jnp.sum(g, axis=-1, keepdims=True) * ROUTE_SCALE
    A = T * TOP_K
    flat_e = idx.reshape(-1).astype(jnp.int32)
    sorted_e, order = lax.sort((flat_e, jnp.arange(A, dtype=jnp.int32)), num_keys=1, is_stable=True)
    counts = jnp.bincount(flat_e, length=N_EXPERTS)
    padded = (counts + MOE_BLOCK - 1) // MOE_BLOCK * MOE_BLOCK
    pad_end = jnp.cumsum(padded)
    pad_start = pad_end - padded
    start = jnp.cumsum(counts) - counts
    dest = pad_start[sorted_e] + jnp.arange(A, dtype=jnp.int32) - start[sorted_e]
    n_blocks = -(-A // MOE_BLOCK) + N_EXPERTS
    P = n_blocks * MOE_BLOCK
    row_tok = jnp.full((P,), T, jnp.int32).at[dest].set(order // TOP_K)
    row_gate = jnp.zeros((P,), x.dtype).at[dest].set(g.reshape(-1)[order].astype(x.dtype))
    block_e = jnp.minimum(jnp.searchsorted(pad_end // MOE_BLOCK, jnp.arange(n_blocks), side='right'),
                          N_EXPERTS - 1).astype(jnp.int32)
    x_pad = jnp.concatenate([x, jnp.zeros((1, D_MODEL), x.dtype)], axis=0)

    def expert_block(args):
        tok, gate, e = args
        xb = x_pad[tok]
        hid = jax.nn.silu(xb @ w1[e]) * (xb @ w3[e])
        return (hid @ w2[e]) * gate[:, None]

    y = lax.map(expert_block, (row_tok.reshape(n_blocks, MOE_BLOCK),
                               row_gate.reshape(n_blocks, MOE_BLOCK), block_e))
    routed = jnp.zeros((T + 1, D_MODEL), x.dtype).at[row_tok].add(y.reshape(P, D_MODEL))[:T]
    shared = (jax.nn.silu(x @ ws1) * (x @ ws3)) @ ws2
    return (routed + shared).reshape(shp)


def setup_inputs(seed: int = 0) -> dict:
    key = jax.random.key(seed)
    ks = jax.random.split(key, 32)
    f32 = jnp.float32
    nrm = lambda k, shape, s: jax.random.normal(k, shape, f32) * s
    E, H = N_EXPERTS, EXPERT_HIDDEN
    return {
        "x_prompt": nrm(ks[0], (BATCH, SEQ, D_MODEL), 1.0),
        "x_sample": nrm(ks[1], (DEC_BATCH, DEC_SEQ, D_MODEL), 1.0),
        "cache_k": nrm(ks[2], (DEC_BATCH, DEPTH, PAST_LEN, N_HEADS, 2, HEAD_DIM), 1.0),
        "cache_v": nrm(ks[3], (DEC_BATCH, DEPTH, PAST_LEN, N_HEADS, 2 * HEAD_DIM), 1.0),
        "c": nrm(ks[4], (DEC_BATCH, D_MODEL), 1.0),
        "c_ctx": nrm(ks[5], (D_MODEL,), 1.0),
        "w_mod": nrm(ks[6], (DEPTH, D_MODEL, N_MOD * D_MODEL), 0.5 * D_MODEL ** -0.5),
        "b_mod": nrm(ks[7], (DEPTH, N_MOD * D_MODEL), 0.01),
        "norm1_g": 1.0 + nrm(ks[8], (DEPTH, D_MODEL), 0.01),
        "w_in": nrm(ks[9], (DEPTH, D_MODEL, IN_COLS), D_MODEL ** -0.5),
        "lambda_qk": nrm(ks[10], (DEPTH, 4, HEAD_DIM), 0.1),
        "subln_g": 1.0 + nrm(ks[11], (DEPTH, 2 * HEAD_DIM), 0.01),
        "w_attn_proj": nrm(ks[12], (DEPTH, ATTN_W, D_MODEL), ATTN_W ** -0.5),
        "conv_w": nrm(ks[13], (DEPTH, CONV_K, CONV_W), CONV_K ** -0.5),
        "conv_b": nrm(ks[14], (DEPTH, CONV_W), 0.01),
        "conv_ln_g": 1.0 + nrm(ks[15], (DEPTH, CONV_W), 0.01),
        "conv_ln_b": nrm(ks[16], (DEPTH, CONV_W), 0.01),
        "w_conv_out": nrm(ks[17], (DEPTH, CONV_W, D_MODEL), CONV_W ** -0.5),
        "w_out": nrm(ks[18], (DEPTH, D_MODEL, D_MODEL), D_MODEL ** -0.5),
        "norm2_g": 1.0 + nrm(ks[19], (DEPTH, D_MODEL), 0.01),
        "w_router": nrm(ks[20], (DEPTH, D_MODEL, E), D_MODEL ** -0.5),
        "router_bias": nrm(ks[21], (DEPTH, E), 0.01),
        "w1": nrm(ks[22], (DEPTH, E, D_MODEL, H), D_MODEL ** -0.5),
        "w3": nrm(ks[23], (DEPTH, E, D_MODEL, H), D_MODEL ** -0.5),
        "w2": nrm(ks[24], (DEPTH, E, H, D_MODEL), H ** -0.5),
        "ws1": nrm(ks[25], (DEPTH, D_MODEL, SHARED_HIDDEN), D_MODEL ** -0.5),
        "ws3": nrm(ks[26], (DEPTH, D_MODEL, SHARED_HIDDEN), D_MODEL ** -0.5),
        "ws2": nrm(ks[27], (DEPTH, SHARED_HIDDEN, D_MODEL), SHARED_HIDDEN ** -0.5),
        "normf_g": 1.0 + nrm(ks[28], (D_MODEL,), 0.01),
    }


def reference(x_prompt, x_sample, cache_k, cache_v, c, c_ctx, w_mod, b_mod, norm1_g, w_in,
              lambda_qk, subln_g, w_attn_proj, conv_w, conv_b, conv_ln_g, conv_ln_b, w_conv_out,
              w_out, norm2_g, w_router, router_bias, w1, w3, w2, ws1, ws3, ws2, normf_g):
    xp, xs = x_prompt, x_sample
    n_lat = xs.shape[1]
    new_k, new_v = [], []
    for l in range(DEPTH):
        lam_init = 0.8 - 0.6 * math.exp(-0.3 * l)
        lq = lambda_qk[l].astype(jnp.float32)
        lam = jnp.exp(jnp.sum(lq[0] * lq[1])) - jnp.exp(jnp.sum(lq[2] * lq[3])) + lam_init
        sh1p, sc1p, g1p, sh2p, sc2p, g2p = jnp.split(jax.nn.silu(c_ctx) @ w_mod[l] + b_mod[l], N_MOD, axis=-1)
        mod_s = (jax.nn.silu(c) @ w_mod[l] + b_mod[l])[:, None, :]
        sh1s, sc1s, g1s, sh2s, sc2s, g2s = jnp.split(mod_s, N_MOD, axis=-1)
        conv_args = (conv_w[l], conv_b[l], conv_ln_g[l], conv_ln_b[l], w_conv_out[l])
        h = modulate(rmsnorm(xp, norm1_g[l]), sh1p, sc1p)
        q, k, v, u, gates = split_proj(h, w_in[l])
        new_k.append(k)
        new_v.append(v)
        o = diff_attention(q, k, v, lam)
        xp = xp + g1p * merge_branches(o, conv_module(u, *conv_args), gates, subln_g[l], lam_init,
                                       w_attn_proj[l], w_out[l])
        h = modulate(rmsnorm(xp, norm2_g[l]), sh2p, sc2p)
        xp = xp + g2p * moe_ffn(h, w_router[l], router_bias[l], w1[l], w3[l], w2[l], ws1[l], ws3[l], ws2[l])
        h = modulate(rmsnorm(xs, norm1_g[l]), sh1s, sc1s)
        q, k, v, u, gates = split_proj(h, w_in[l])
        q = axial_rope(q, n_lat)
        k = axial_rope(k, n_lat)
        k_all = jnp.concatenate([k, cache_k[:, l].astype(k.dtype)], axis=1)
        v_all = jnp.concatenate([v, cache_v[:, l].astype(v.dtype)], axis=1)
        o = diff_attention(q, k_all, v_all, lam)
        xs = xs + g1s * merge_branches(o, conv_module(u, *conv_args), gates, subln_g[l], lam_init,
                                       w_attn_proj[l], w_out[l])
        h = modulate(rmsnorm(xs, norm2_g[l]), sh2s, sc2s)
        xs = xs + g2s * moe_ffn(h, w_router[l], router_bias[l], w1[l], w3[l], w2[l], ws1[l], ws3[l], ws2[l])
    y_prompt = rmsnorm(xp, normf_g)
    y_sample = rmsnorm(xs, normf_g)
    state_k = jnp.stack(new_k, axis=1)
    state_v = jnp.stack(new_v, axis=1)
    return (y_prompt, y_sample, state_k, state_v)
```

```python
import functools
import math

import jax
import jax.numpy as jnp
from jax import lax
from jax.experimental import pallas as pl
from jax.experimental.pallas import tpu as pltpu

D_MODEL = 1024
GRID_W = 64
N_HEADS = 8
HEAD_DIM = 64
ATTN_W = N_HEADS * 2 * HEAD_DIM
CONV_W = 512
CONV_K = 31
N_EXPERTS = 256
TOP_K = 8
N_GROUPS = 8
TOPK_GROUPS = 4
GROUP_SIZE = N_EXPERTS // N_GROUPS
EXPERT_HIDDEN = 256
ROUTE_SCALE = 2.5
ROPE_BASE = 10000.0
EPS = 1e-6
N_MOD = 6
IN_COLS = 3 * ATTN_W + 2 * CONV_W + 2 * D_MODEL
LAM_INIT = 0.8 - 0.6 * math.exp(-0.3 * 0)

LANES = 128
SUBLANES = 8
VMEM_LIMIT = 56 * 1024 * 1024
HALO = 16
ROW_TILE = 256
MOE_BLOCK = 256
MOD_ROWS = 8
SLAB = D_MODEL // LANES

BF16 = jnp.bfloat16
F32 = jnp.float32


def _cparams(sem):
    return pltpu.CompilerParams(dimension_semantics=sem, vmem_limit_bytes=VMEM_LIMIT)


def _silu(x):
    return x * jax.nn.sigmoid(x)


def _dot(a, b):
    return jnp.dot(a, b, preferred_element_type=F32)


def _dot_nt(a, b):
    return lax.dot_general(a, b, (((1,), (1,)), ((), ())), preferred_element_type=F32)


def _rms(x, g):
    return x * lax.rsqrt(jnp.mean(x * x, axis=-1, keepdims=True) + EPS) * g


def _mod_kernel(c_ref, w_ref, b_ref, o_ref):
    a = _silu(c_ref[...]).astype(BF16)
    o_ref[...] = _dot(a, w_ref[...].astype(BF16)) + b_ref[...]


def _mod_table(cc, w_mod, b_mod):
    n = w_mod.shape[1]
    tn = n // 4
    return pl.pallas_call(
        _mod_kernel,
        grid=(n // tn,),
        in_specs=[pl.BlockSpec((MOD_ROWS, D_MODEL), lambda j: (0, 0)),
                  pl.BlockSpec((D_MODEL, tn), lambda j: (0, j)),
                  pl.BlockSpec((1, tn), lambda j: (0, j))],
        out_specs=pl.BlockSpec((MOD_ROWS, tn), lambda j: (0, j)),
        out_shape=jax.ShapeDtypeStruct((MOD_ROWS, n), F32),
        compiler_params=_cparams(("arbitrary",)),
        name="mod_table",
    )(cc, w_mod, b_mod.reshape(1, n))


def _rope(x, cos, sin, lane_lo):
    swapped = jnp.where(lane_lo, pltpu.roll(x, LANES - 16, axis=1), pltpu.roll(x, 16, axis=1))
    return x * cos + swapped * sin


def _inproj_kernel(latent, x_ref, mod_ref, g_ref, w_ref, *rest):
    if latent:
        cos_ref, sin_ref, q_ref, k_ref, v_ref, z_ref, gt_ref = rest
    else:
        q_ref, k_ref, v_ref, z_ref, gt_ref, ks_ref, vs_ref = rest
    mod = mod_ref[0]
    shift, scale = mod[:, 0:D_MODEL], mod[:, D_MODEL:2 * D_MODEL]
    h = _rms(x_ref[...], g_ref[...]) * (1.0 + scale) + shift
    hb = h.astype(BF16)
    q = _dot(hb, w_ref[:, 0:ATTN_W])
    k = _dot(hb, w_ref[:, ATTN_W:2 * ATTN_W])
    v = _dot(hb, w_ref[:, 2 * ATTN_W:3 * ATTN_W])
    if latent:
        cos, sin = cos_ref[...], sin_ref[...]
        lane_lo = (lax.broadcasted_iota(jnp.int32, cos.shape, 1) % 32) < 16
        for j in range(ATTN_W // LANES):
            sl = slice(j * LANES, (j + 1) * LANES)
            q_ref[:, sl] = (_rope(q[:, sl], cos, sin, lane_lo) * (HEAD_DIM ** -0.5)).astype(BF16)
            k_ref[:, sl] = _rope(k[:, sl], cos, sin, lane_lo).astype(BF16)
    else:
        q_ref[...] = (q * (HEAD_DIM ** -0.5)).astype(BF16)
        k_ref[...] = k.astype(BF16)
        ks_ref[...] = k
        vs_ref[...] = v
    v_ref[...] = v.astype(BF16)
    u = _dot(hb, w_ref[:, 3 * ATTN_W:3 * ATTN_W + 2 * CONV_W])
    z_ref[...] = u[:, :CONV_W] * jax.nn.sigmoid(u[:, CONV_W:])
    gt_ref[...] = jax.nn.sigmoid(_dot(hb, w_ref[:, 3 * ATTN_W + 2 * CONV_W:]))


def _inproj(x, mod3, mod_index, norm_g, w_in_b, rope=None, seq_len=None):
    rows = x.shape[0]
    tm = ROW_TILE
    latent = rope is not None
    row_spec = lambda w: pl.BlockSpec((tm, w), lambda i: (i, 0))
    in_specs = [row_spec(D_MODEL),
                pl.BlockSpec((1, 1, N_MOD * D_MODEL), lambda i: (mod_index(i), 0, 0)),
                pl.BlockSpec((1, D_MODEL), lambda i: (0, 0)),
                pl.BlockSpec((D_MODEL, IN_COLS), lambda i: (0, 0), pipeline_mode=pl.Buffered(1))]
    args = [x, mod3, norm_g, w_in_b]
    out_specs = [row_spec(ATTN_W)] * 3 + [row_spec(CONV_W), row_spec(2 * D_MODEL)]
    out_shape = [jax.ShapeDtypeStruct((rows, ATTN_W), BF16)] * 3 + [
        jax.ShapeDtypeStruct((rows, CONV_W), F32), jax.ShapeDtypeStruct((rows, 2 * D_MODEL), F32)]
    if latent:
        tiles_per_seq = seq_len // tm
        rope_spec = pl.BlockSpec((tm, LANES), lambda i: (i % tiles_per_seq, 0))
        in_specs += [rope_spec, rope_spec]
        args += list(rope)
    else:
        out_specs += [row_spec(ATTN_W)] * 2
        out_shape += [jax.ShapeDtypeStruct((rows, ATTN_W), F32)] * 2
    return pl.pallas_call(
        functools.partial(_inproj_kernel, latent),
        grid=(rows // tm,),
        in_specs=in_specs, out_specs=out_specs, out_shape=out_shape,
        compiler_params=_cparams(("parallel",)),
        name="inproj_latent" if latent else "inproj_ctx",
    )(*args)


def _rope_tables(n_tokens):
    t = jnp.arange(n_tokens, dtype=jnp.int32)
    pos = jnp.stack([t // GRID_W, t % GRID_W], axis=-1).astype(F32)
    half = HEAD_DIM // 2
    inv = ROPE_BASE ** (-jnp.arange(0, half, 2, dtype=F32) / half)
    ang = pos[:, :, None] * inv
    cos, sin = jnp.cos(ang), jnp.sin(ang)
    cos64 = jnp.concatenate([cos[:, 0], cos[:, 0], cos[:, 1], cos[:, 1]], axis=-1)
    sin64 = jnp.concatenate([-sin[:, 0], sin[:, 0], -sin[:, 1], sin[:, 1]], axis=-1)
    return jnp.tile(cos64, (1, LANES // HEAD_DIM)), jnp.tile(sin64, (1, LANES // HEAD_DIM))


def _attn_kernel(has_cache, lq_ref, sg_ref, q_ref, k_ref, v_ref, *rest):
    if has_cache:
        ck_ref, cv_ref, o_ref = rest
    else:
        (o_ref,) = rest
    lq = lq_ref[...]
    lam = (jnp.exp(jnp.sum(lq[0:1] * lq[1:2], axis=-1, keepdims=True))
           - jnp.exp(jnp.sum(lq[2:3] * lq[3:4], axis=-1, keepdims=True)) + LAM_INIT)
    q = q_ref[...]
    first = lax.broadcasted_iota(jnp.int32, q.shape, 1) < HEAD_DIM
    zero = jnp.zeros_like(q)
    qs = (jnp.where(first, q, zero), jnp.where(first, zero, q))
    keys = [k_ref[...]]
    vals = [v_ref[...]]
    if has_cache:
        keys.append(ck_ref[...].astype(BF16))
        vals.append(cv_ref[...].astype(BF16))
    probs = []
    for qm in qs:
        s = [_dot_nt(qm, kk) for kk in keys]
        m = functools.reduce(jnp.maximum, [jnp.max(t, axis=-1, keepdims=True) for t in s])
        e = [jnp.exp(t - m) for t in s]
        l = functools.reduce(jnp.add, [jnp.sum(t, axis=-1, keepdims=True) for t in e])
        probs.append((e, 1.0 / l))
    (e1, r1), (e2, r2) = probs
    r2 = r2 * lam
    o = None
    for a, b, vv in zip(e1, e2, vals):
        w = (a * r1 - b * r2).astype(BF16)
        part = _dot(w, vv)
        o = part if o is None else o + part
    o_ref[...] = (_rms(o, sg_ref[...]) * (1.0 - LAM_INIT)).astype(BF16)


def _attention(q, k, v, lambda_qk, subln_g, batch, seq_len, tq, cache=None):
    q3, k3, v3 = (t.reshape(batch, seq_len, ATTN_W) for t in (q, k, v))
    head_q = pl.BlockSpec((None, tq, LANES), lambda b, h, i: (b, i, h))
    head_kv = lambda n: pl.BlockSpec((None, n, LANES), lambda b, h, i: (b, 0, h))
    in_specs = [pl.BlockSpec((4, HEAD_DIM), lambda b, h, i: (0, 0)),
                pl.BlockSpec((1, 2 * HEAD_DIM), lambda b, h, i: (0, 0)),
                head_q, head_kv(seq_len), head_kv(seq_len)]
    args = [lambda_qk, subln_g, q3, k3, v3]
    if cache is not None:
        past = cache[0].shape[1]
        in_specs += [head_kv(past), head_kv(past)]
        args += list(cache)
    out = pl.pallas_call(
        functools.partial(_attn_kernel, cache is not None),
        grid=(batch, N_HEADS, seq_len // tq),
        in_specs=in_specs, out_specs=head_q,
        out_shape=jax.ShapeDtypeStruct((batch, seq_len, ATTN_W), BF16),
        compiler_params=_cparams(("parallel", "parallel", "arbitrary")),
        name="attn_latent" if cache is not None else "attn_ctx",
    )(*args)
    return out.reshape(batch * seq_len, ATTN_W)


def _merge_kernel(tiles_per_seq, x_ref, on_ref, z_ref, zp_ref, zn_ref, gt_ref, mod_ref, cw_ref, cb_ref,
                  lg_ref, lb_ref, wco_ref, wap_ref, wout_ref, n2_ref, x1_ref, h2_ref, zext_ref):
    i = pl.program_id(0)
    tm = x_ref.shape[0]
    t = i % tiles_per_seq
    zext_ref[HALO:HALO + tm, :] = z_ref[...]
    zext_ref[0:HALO, :] = jnp.where(t == 0, 0.0, zp_ref[...])
    zext_ref[HALO + tm:, :] = jnp.where(t == tiles_per_seq - 1, 0.0, zn_ref[...])
    rc = 64
    pad = HALO - CONV_K // 2
    col_blocks = []
    for cb in range(CONV_W // LANES):
        cs = slice(cb * LANES, (cb + 1) * LANES)
        row_chunks = []
        for r0 in range(0, tm, rc):
            acc = jnp.zeros((rc, LANES), F32) + cb_ref[:, cs]
            for j in range(CONV_K):
                acc = acc + zext_ref[r0 + j + pad:r0 + j + pad + rc, cs] * cw_ref[j:j + 1, cs]
            row_chunks.append(acc)
        col_blocks.append(jnp.concatenate(row_chunks, axis=0))
    c = jnp.concatenate(col_blocks, axis=1)
    mu = jnp.mean(c, axis=-1, keepdims=True)
    cc = c - mu
    y = cc * lax.rsqrt(jnp.mean(cc * cc, axis=-1, keepdims=True) + EPS) * lg_ref[...] + lb_ref[...]
    conv_out = _dot(_silu(y).astype(BF16), wco_ref[...])
    a_br = _dot(on_ref[...], wap_ref[...])
    gt = gt_ref[...]
    merged = gt[:, :D_MODEL] * a_br + gt[:, D_MODEL:] * conv_out
    mod = mod_ref[0]
    gate1 = mod[:, 2 * D_MODEL:3 * D_MODEL]
    shift2, scale2 = mod[:, 3 * D_MODEL:4 * D_MODEL], mod[:, 4 * D_MODEL:5 * D_MODEL]
    x1 = x_ref[...] + gate1 * _dot(merged.astype(BF16), wout_ref[...])
    x1_ref[...] = x1
    h2_ref[...] = _rms(x1, n2_ref[...]) * (1.0 + scale2) + shift2


def _merge(x, o_n, z, gt, mod3, mod_index, seq_len, conv_w, conv_b, ln_g, ln_b, wco_b, wap_b, wout_b, norm2_g):
    rows = x.shape[0]
    tm = ROW_TILE
    tiles_per_seq = seq_len // tm
    hb = tm // HALO
    n_halo_blocks = rows // HALO
    row_spec = lambda w: pl.BlockSpec((tm, w), lambda i: (i, 0))
    const = lambda shape: pl.BlockSpec(shape, lambda i: (0,) * len(shape))
    in_specs = [row_spec(D_MODEL), row_spec(ATTN_W), row_spec(CONV_W),
                pl.BlockSpec((HALO, CONV_W), lambda i: (jnp.maximum(i * hb - 1, 0), 0)),
                pl.BlockSpec((HALO, CONV_W), lambda i: (jnp.minimum((i + 1) * hb, n_halo_blocks - 1), 0)),
                row_spec(2 * D_MODEL),
                pl.BlockSpec((1, 1, N_MOD * D_MODEL), lambda i: (mod_index(i), 0, 0)),
                const((CONV_K, CONV_W)), const((1, CONV_W)), const((1, CONV_W)), const((1, CONV_W)),
                const((CONV_W, D_MODEL)), const((ATTN_W, D_MODEL)), const((D_MODEL, D_MODEL)),
                const((1, D_MODEL))]
    return pl.pallas_call(
        functools.partial(_merge_kernel, tiles_per_seq),
        grid=(rows // tm,),
        in_specs=in_specs,
        out_specs=[row_spec(D_MODEL), row_spec(D_MODEL)],
        out_shape=[jax.ShapeDtypeStruct((rows, D_MODEL), F32)] * 2,
        scratch_shapes=[pltpu.VMEM((tm + 2 * HALO, CONV_W), F32)],
        compiler_params=_cparams(("parallel",)),
        name="conv_merge",
    )(x, o_n, z, z, z, gt, mod3, conv_w, conv_b, ln_g, ln_b, wco_b, wap_b, wout_b, norm2_g)


def _pack_cols(cols, shape):
    lane = lax.broadcasted_iota(jnp.int32, shape, 1)
    out = jnp.zeros(shape, F32)
    for j, col in enumerate(cols):
        out = jnp.where(lane == j, col, out)
    return out


def _router_kernel(h_ref, wr_ref, rb_ref, idx_ref, gate_ref, pos_ref, cnt_ref, run_ref):
    i = pl.program_id(0)

    @pl.when(i == 0)
    def _():
        run_ref[...] = jnp.zeros_like(run_ref)

    tm = h_ref.shape[0]
    neg = jnp.float32(-jnp.inf)
    scores = jax.nn.sigmoid(_dot(h_ref[...].astype(BF16), wr_ref[...]))
    biased = scores + rb_ref[...]
    lane_i = lax.broadcasted_iota(jnp.int32, scores.shape, 1)
    lane = lane_i.astype(F32)
    far = jnp.float32(2 * N_EXPERTS)

    def first_argmax(v):
        m = jnp.max(v, axis=-1, keepdims=True)
        return m, jnp.min(jnp.where(v == m, lane, far), axis=-1, keepdims=True)

    in_group, gscore = [], []
    for g in range(N_GROUPS):
        inb = (lane_i >= g * GROUP_SIZE) & (lane_i < (g + 1) * GROUP_SIZE)
        v = jnp.where(inb, biased, neg)
        m1, i1 = first_argmax(v)
        m2 = jnp.max(jnp.where(lane == i1, neg, v), axis=-1, keepdims=True)
        in_group.append(inb)
        gscore.append(m1 + m2)
    allowed = jnp.zeros(scores.shape, F32)
    for g in range(N_GROUPS):
        ahead = jnp.zeros((tm, 1), F32)
        for g2 in range(N_GROUPS):
            if g2 < g:
                ahead = ahead + (gscore[g2] >= gscore[g]).astype(F32)
            elif g2 > g:
                ahead = ahead + (gscore[g2] > gscore[g]).astype(F32)
        keep = (ahead < TOPK_GROUPS).astype(F32)
        allowed = jnp.where(in_group[g], keep, allowed)
    masked = jnp.where(allowed > 0.0, biased, neg)
    picked = jnp.zeros(scores.shape, F32)
    idxs, gates = [], []
    for _ in range(TOP_K):
        _, ik = first_argmax(masked)
        hit = lane == ik
        gates.append(jnp.sum(jnp.where(hit, scores, 0.0), axis=-1, keepdims=True))
        masked = jnp.where(hit, neg, masked)
        picked = jnp.where(hit, 1.0, picked)
        idxs.append(ik)
    gsum = functools.reduce(jnp.add, gates)
    gates = [g / gsum * ROUTE_SCALE for g in gates]
    r_i = lax.broadcasted_iota(jnp.int32, (tm, tm), 0)
    c_i = lax.broadcasted_iota(jnp.int32, (tm, tm), 1)
    before = (c_i < r_i).astype(BF16)
    rank = _dot(before, picked.astype(BF16)) + run_ref[...]
    poss = [jnp.sum(jnp.where(lane == ik, rank, 0.0), axis=-1, keepdims=True) for ik in idxs]
    run_ref[...] = run_ref[...] + jnp.sum(picked, axis=0, keepdims=True)
    cnt_ref[...] = run_ref[...]
    shape = idx_ref.shape
    idx_ref[...] = _pack_cols(idxs, shape).astype(jnp.int32)
    gate_ref[...] = _pack_cols(gates, shape)
    pos_ref[...] = _pack_cols(poss, shape).astype(jnp.int32)


def _router(h2, w_router_b, router_bias):
    rows = h2.shape[0]
    tm = ROW_TILE
    row_spec = lambda w: pl.BlockSpec((tm, w), lambda i: (i, 0))
    const = lambda shape: pl.BlockSpec(shape, lambda i: (0,) * len(shape))
    return pl.pallas_call(
        _router_kernel,
        grid=(rows // tm,),
        in_specs=[row_spec(D_MODEL), const((D_MODEL, N_EXPERTS)), const((1, N_EXPERTS))],
        out_specs=[row_spec(LANES), row_spec(LANES), row_spec(LANES), const((1, N_EXPERTS))],
        out_shape=[jax.ShapeDtypeStruct((rows, LANES), jnp.int32), jax.ShapeDtypeStruct((rows, LANES), F32),
                   jax.ShapeDtypeStruct((rows, LANES), jnp.int32), jax.ShapeDtypeStruct((1, N_EXPERTS), F32)],
        scratch_shapes=[pltpu.VMEM((1, N_EXPERTS), F32)],
        compiler_params=_cparams(("arbitrary",)),
        name="router",
    )(h2, w_router_b, router_bias)


def _slab(row):
    return pl.ds(pl.multiple_of(row * SLAB, SLAB), SLAB)


def _from_slabs(ref, n_tokens, lead=()):
    return jnp.concatenate([ref[lead + (pl.ds(s, n_tokens, stride=SLAB), slice(None))] for s in range(SLAB)],
                           axis=1)


def _dispatch_kernel(pe_ref, dest_hbm, h_ref, xs_hbm, dsm, zbuf, sem_i, sem):
    i = pl.program_id(0)
    tm = h_ref.shape[0] // SLAB
    n_slots = tm * TOP_K

    @pl.when(i == 0)
    def _():
        zbuf[...] = jnp.zeros_like(zbuf)
        n_blocks = xs_hbm.shape[0] // (MOE_BLOCK * SLAB)
        n_used = pe_ref[N_EXPERTS - 1] // MOE_BLOCK

        def clear_block(b):
            dst = xs_hbm.at[pl.ds(pl.multiple_of(b * (MOE_BLOCK * SLAB), MOE_BLOCK * SLAB), MOE_BLOCK * SLAB)]
            pltpu.make_async_copy(zbuf, dst, sem).start()

        def clear_last(e, c):
            clear_block(jnp.maximum(pe_ref[e] // MOE_BLOCK - 1, 0))
            return c

        def clear_tail(b, c):
            clear_block(b)
            return c

        def clear_wait(e, c):
            pltpu.make_async_copy(zbuf, xs_hbm.at[pl.ds(0, MOE_BLOCK * SLAB)], sem).wait()
            return c

        lax.fori_loop(0, N_EXPERTS, clear_last, 0)
        lax.fori_loop(n_used, n_blocks, clear_tail, 0)
        lax.fori_loop(0, N_EXPERTS + n_blocks - n_used, clear_wait, 0)

    cp = pltpu.make_async_copy(dest_hbm.at[pl.ds(i * n_slots, n_slots)], dsm, sem_i)
    cp.start()
    cp.wait()

    def issue(r, c):
        for k in range(TOP_K):
            d = dsm[r * TOP_K + k]
            pltpu.make_async_copy(h_ref.at[_slab(r)], xs_hbm.at[_slab(d)], sem).start()
        return c

    lax.fori_loop(0, tm, issue, 0)
    for _ in range(TOP_K):
        pltpu.make_async_copy(h_ref, xs_hbm.at[pl.ds(0, tm * SLAB)], sem).wait()


def _dispatch(pad_end, dest_flat, h2_slab, n_rows_sorted):
    rows = h2_slab.shape[0] // SLAB
    tm = ROW_TILE
    return pl.pallas_call(
        _dispatch_kernel,
        grid_spec=pltpu.PrefetchScalarGridSpec(
            num_scalar_prefetch=1,
            grid=(rows // tm,),
            in_specs=[pl.BlockSpec(memory_space=pl.ANY),
                      pl.BlockSpec((tm * SLAB, LANES), lambda i, pe: (i, 0))],
            out_specs=pl.BlockSpec(memory_space=pl.ANY),
            scratch_shapes=[pltpu.SMEM((tm * TOP_K,), jnp.int32),
                            pltpu.VMEM((MOE_BLOCK * SLAB, LANES), F32),
                            pltpu.SemaphoreType.DMA, pltpu.SemaphoreType.DMA]),
        out_shape=jax.ShapeDtypeStruct((n_rows_sorted * SLAB, LANES), F32),
        compiler_params=_cparams(("arbitrary",)),
        name="moe_dispatch",
    )(pad_end, dest_flat, h2_slab)


def _expert_kernel(be_ref, nu_ref, x_ref, w1_ref, w3_ref, w2_ref, y_ref, w1b, w3b, w2b):
    b = pl.program_id(0)

    @pl.when(b < nu_ref[0])
    def _():
        changed = (b == 0) | (be_ref[b] != be_ref[jnp.maximum(b - 1, 0)])

        @pl.when(changed)
        def _():
            w1b[...] = w1_ref[...].astype(BF16)
            w3b[...] = w3_ref[...].astype(BF16)
            w2b[...] = w2_ref[...].astype(BF16)

        xb = _from_slabs(x_ref, MOE_BLOCK).astype(BF16)
        hid = (_silu(_dot(xb, w1b[...])) * _dot(xb, w3b[...])).astype(BF16)
        y = _dot(hid, w2b[...])
        for s in range(SLAB):
            y_ref[pl.ds(s, MOE_BLOCK, stride=SLAB), :] = y[:, s * LANES:(s + 1) * LANES]


def _experts(block_e, n_used, xs, w1, w3, w2):
    n_blocks = xs.shape[0] // (MOE_BLOCK * SLAB)
    row_map = lambda b, be, nu: (jnp.minimum(b, nu[0] - 1), 0)
    return pl.pallas_call(
        _expert_kernel,
        grid_spec=pltpu.PrefetchScalarGridSpec(
            num_scalar_prefetch=2,
            grid=(n_blocks,),
            in_specs=[pl.BlockSpec((MOE_BLOCK * SLAB, LANES), row_map),
                      pl.BlockSpec((None, D_MODEL, EXPERT_HIDDEN), lambda b, be, nu: (be[b], 0, 0)),
                      pl.BlockSpec((None, D_MODEL, EXPERT_HIDDEN), lambda b, be, nu: (be[b], 0, 0)),
                      pl.BlockSpec((None, EXPERT_HIDDEN, D_MODEL), lambda b, be, nu: (be[b], 0, 0))],
            out_specs=pl.BlockSpec((MOE_BLOCK * SLAB, LANES), row_map),
            scratch_shapes=[pltpu.VMEM((D_MODEL, EXPERT_HIDDEN), BF16),
                            pltpu.VMEM((D_MODEL, EXPERT_HIDDEN), BF16),
                            pltpu.VMEM((EXPERT_HIDDEN, D_MODEL), BF16)]),
        out_shape=jax.ShapeDtypeStruct(xs.shape, F32),
        input_output_aliases={2: 0},
        compiler_params=_cparams(("arbitrary",)),
        name="moe_experts",
    )(block_e, n_used, xs, w1, w3, w2)


def _combine_kernel(dest_hbm, ys_hbm, x1_ref, h_ref, gate_ref, mod_ref, ws1_ref, ws3_ref, ws2_ref, nf_ref,
                    out_ref, dsm, ybuf, sem_i, sem):
    i = pl.program_id(0)
    tm = x1_ref.shape[0]
    n_slots = tm * TOP_K
    cp = pltpu.make_async_copy(dest_hbm.at[pl.ds(i * n_slots, n_slots)], dsm, sem_i)
    cp.start()
    cp.wait()

    def issue(r, c):
        for k in range(TOP_K):
            d = dsm[r * TOP_K + k]
            pltpu.make_async_copy(ys_hbm.at[_slab(d)], ybuf.at[k, _slab(r)], sem).start()
        return c

    lax.fori_loop(0, tm, issue, 0)
    hb = h_ref[...].astype(BF16)
    shared = _dot((_silu(_dot(hb, ws1_ref[...])) * _dot(hb, ws3_ref[...])).astype(BF16), ws2_ref[...])
    for k in range(TOP_K):
        pltpu.make_async_copy(ys_hbm.at[pl.ds(0, tm * SLAB)], ybuf.at[k], sem).wait()
    g = gate_ref[...]
    routed = _from_slabs(ybuf, tm, (0,)) * g[:, 0:1]
    for k in range(1, TOP_K):
        routed = routed + _from_slabs(ybuf, tm, (k,)) * g[:, k:k + 1]
    gate2 = mod_ref[0][:, 5 * D_MODEL:6 * D_MODEL]
    x2 = x1_ref[...] + gate2 * (routed + shared)
    out_ref[...] = _rms(x2, nf_ref[...])


def _combine(dest_flat, ys, x1, h2, gates, mod3, mod_index, ws1_b, ws3_b, ws2_b, normf_g):
    rows = x1.shape[0]
    tm = ROW_TILE
    row_spec = lambda w: pl.BlockSpec((tm, w), lambda i: (i, 0))
    const = lambda shape: pl.BlockSpec(shape, lambda i: (0,) * len(shape))
    return pl.pallas_call(
        _combine_kernel,
        grid=(rows // tm,),
        in_specs=[pl.BlockSpec(memory_space=pl.ANY), pl.BlockSpec(memory_space=pl.ANY),
                  row_spec(D_MODEL), row_spec(D_MODEL), row_spec(LANES),
                  pl.BlockSpec((1, 1, N_MOD * D_MODEL), lambda i: (mod_index(i), 0, 0)),
                  const((D_MODEL, EXPERT_HIDDEN)), const((D_MODEL, EXPERT_HIDDEN)),
                  const((EXPERT_HIDDEN, D_MODEL)), const((1, D_MODEL))],
        out_specs=row_spec(D_MODEL),
        out_shape=jax.ShapeDtypeStruct((rows, D_MODEL), F32),
        scratch_shapes=[pltpu.SMEM((tm * TOP_K,), jnp.int32),
                        pltpu.VMEM((TOP_K, tm * SLAB, LANES), F32),
                        pltpu.SemaphoreType.DMA, pltpu.SemaphoreType.DMA],
        compiler_params=_cparams(("arbitrary",)),
        name="moe_combine",
    )(dest_flat, ys, x1, h2, gates, mod3, ws1_b, ws3_b, ws2_b, normf_g)


def kernel(x_prompt, x_sample, cache_k, cache_v, c, c_ctx, w_mod, b_mod, norm1_g, w_in, lambda_qk, subln_g,
           w_attn_proj, conv_w, conv_b, conv_ln_g, conv_ln_b, w_conv_out, w_out, norm2_g, w_router,
           router_bias, w1, w3, w2, ws1, ws3, ws2, normf_g):
    batch, seq, _ = x_prompt.shape
    dec_batch, dec_seq, _ = x_sample.shape
    past = cache_k.shape[2]
    l = 0
    rows_p, rows_s = batch * seq, dec_batch * dec_seq
    xp = x_prompt.reshape(rows_p, D_MODEL)
    xs = x_sample.reshape(rows_s, D_MODEL)

    cc = jnp.zeros((MOD_ROWS, D_MODEL), F32).at[0].set(c_ctx).at[1:1 + dec_batch].set(c)
    mod3 = _mod_table(cc, w_mod[l], b_mod[l]).reshape(MOD_ROWS, 1, N_MOD * D_MODEL)
    tiles_per_dec = dec_seq // ROW_TILE
    mod_ctx = lambda i: 0
    mod_lat = lambda i: 1 + i // tiles_per_dec

    w_in_b = w_in[l].astype(BF16)
    wap_b, wco_b, wout_b = w_attn_proj[l].astype(BF16), w_conv_out[l].astype(BF16), w_out[l].astype(BF16)
    row = lambda a: a.reshape(1, -1)
    merge_w = (conv_w[l], row(conv_b[l]), row(conv_ln_g[l]), row(conv_ln_b[l]), wco_b, wap_b, wout_b,
               row(norm2_g[l]))

    q, k, v, z, gt, state_k, state_v = _inproj(xp, mod3, mod_ctx, row(norm1_g[l]), w_in_b)
    o_n = _attention(q, k, v, lambda_qk[l], row(subln_g[l]), batch, seq, seq)
    x1p, h2p = _merge(xp, o_n, z, gt, mod3, mod_ctx, seq, *merge_w)

    q, k, v, z, gt = _inproj(xs, mod3, mod_lat, row(norm1_g[l]), w_in_b, rope=_rope_tables(dec_seq),
                             seq_len=dec_seq)
    cache = (cache_k[:, l].reshape(dec_batch, past, ATTN_W), cache_v[:, l].reshape(dec_batch, past, ATTN_W))
    o_n = _attention(q, k, v, lambda_qk[l], row(subln_g[l]), dec_batch, dec_seq, ROW_TILE, cache=cache)
    x1s, h2s = _merge(xs, o_n, z, gt, mod3, mod_lat, dec_seq, *merge_w)

    x1 = jnp.concatenate([x1p, x1s], axis=0)
    h2 = jnp.concatenate([h2p, h2s], axis=0)
    rows = rows_p + rows_s
    idx, gates, pos, counts = _router(h2, w_router[l].astype(BF16), row(router_bias[l]))
    counts = counts[0].astype(jnp.int32)
    padded = (counts + MOE_BLOCK - 1) // MOE_BLOCK * MOE_BLOCK
    pad_end = jnp.cumsum(padded).astype(jnp.int32)
    pad_start = pad_end - padded
    dest = (pad_start[idx[:, :TOP_K]] + pos[:, :TOP_K]).reshape(-1)
    n_blocks = rows * TOP_K // MOE_BLOCK + N_EXPERTS
    block_e = jnp.minimum(jnp.searchsorted(pad_end // MOE_BLOCK, jnp.arange(n_blocks, dtype=jnp.int32),
                                           side='right'), N_EXPERTS - 1).astype(jnp.int32)
    n_used = (pad_end[-1:] // MOE_BLOCK).astype(jnp.int32)
    x_sorted = _dispatch(pad_end, dest, h2.reshape(rows * SLAB, LANES), n_blocks * MOE_BLOCK)
    y_sorted = _experts(block_e, n_used, x_sorted, w1[l], w3[l], w2[l])
    tiles_ctx = rows_p // ROW_TILE
    mod_all = lambda i: jnp.where(i < tiles_ctx, 0, 1 + (i - tiles_ctx) // tiles_per_dec)
    y = _combine(dest, y_sorted, x1, h2, gates, mod3, mod_all, ws1[l].astype(BF16), ws3[l].astype(BF16),
                 ws2[l].astype(BF16), row(normf_g))
    y_prompt = y[:rows_p].reshape(batch, seq, D_MODEL)
    y_sample = y[rows_p:].reshape(dec_batch, dec_seq, D_MODEL)
    return (y_prompt, y_sample,
            state_k.reshape(batch, 1, seq, N_HEADS, 2, HEAD_DIM),
            state_v.reshape(batch, 1, seq, N_HEADS, 2 * HEAD_DIM))
```

```python
import functools
import math
from typing import NamedTuple

import jax
import jax.numpy as jnp
from jax import lax
from jax.experimental import pallas as pl
from jax.experimental.pallas import tpu as pltpu

D_MODEL = 1024
GRID_W = 64
N_HEADS = 8
HEAD_DIM = 64
ATTN_W = N_HEADS * 2 * HEAD_DIM
CONV_W = 512
CONV_K = 31
N_EXPERTS = 256
TOP_K = 8
N_GROUPS = 8
TOPK_GROUPS = 4
GROUP_SIZE = N_EXPERTS // N_GROUPS
EXPERT_HIDDEN = 256
ROUTE_SCALE = 2.5
ROPE_BASE = 10000.0
EPS = 1e-6
N_MOD = 6
IN_COLS = 3 * ATTN_W + 2 * CONV_W + 2 * D_MODEL
LAM_INIT = 0.8 - 0.6 * math.exp(-0.3 * 0)

LANES = 128
SUBLANES = 8
VMEM_LIMIT = 56 * 1024 * 1024
HALO = 16
CONV_ROWS = 64
ROW_TILE = 256
MOE_BLOCK = 256
MOD_ROWS = 8
MOD_COL_TILES = 4
SLAB = D_MODEL // LANES
assert SLAB == SUBLANES

BF16 = jnp.bfloat16
F32 = jnp.float32


class Rows(NamedTuple):
    tiles_ctx: int
    tiles_lat: int
    tiles_per_ctx: int
    tiles_per_lat: int

    @property
    def tiles(self):
        return self.tiles_ctx + self.tiles_lat

    def ctx_tile(self, i):
        return jnp.minimum(i, self.tiles_ctx - 1)

    def lat_tile(self, i):
        return jnp.maximum(i - self.tiles_ctx, 0)

    def mod_row(self, i):
        return jnp.where(i < self.tiles_ctx, 0, 1 + self.lat_tile(i) // self.tiles_per_lat)

    def seq_tile(self, i):
        is_ctx = i < self.tiles_ctx
        per = jnp.where(is_ctx, self.tiles_per_ctx, self.tiles_per_lat)
        return jnp.where(is_ctx, i % self.tiles_per_ctx, self.lat_tile(i) % self.tiles_per_lat), per


def _cparams(sem):
    return pltpu.CompilerParams(dimension_semantics=sem, vmem_limit_bytes=VMEM_LIMIT)


def _row_spec(width, tile=lambda i: i):
    return pl.BlockSpec((ROW_TILE, width), lambda i: (tile(i), 0))


def _const_spec(shape):
    return pl.BlockSpec(shape, lambda i: (0,) * len(shape))


def _mod_spec(rows):
    return pl.BlockSpec((1, 1, N_MOD * D_MODEL), lambda i: (rows.mod_row(i), 0, 0))


def _silu(x):
    return x * jax.nn.sigmoid(x)


def _dot(a, b):
    return jnp.dot(a, b, preferred_element_type=F32)


def _dot_nt(a, b):
    return lax.dot_general(a, b, (((1,), (1,)), ((), ())), preferred_element_type=F32)


def _rms(x, g):
    return x * lax.rsqrt(jnp.mean(x * x, axis=-1, keepdims=True) + EPS) * g


def _slab(row):
    return pl.ds(pl.multiple_of(row * SLAB, SLAB), SLAB)


def _from_slabs(ref, n_tokens, lead=()):
    return jnp.concatenate([ref[lead + (pl.ds(s, n_tokens, stride=SLAB), slice(None))] for s in range(SLAB)],
                           axis=1)


def _to_slabs(ref, x, n_tokens):
    for s in range(SLAB):
        ref[pl.ds(s, n_tokens, stride=SLAB), :] = x[:, s * LANES:(s + 1) * LANES]


def _mod_kernel(c_ref, w_ref, b_ref, o_ref):
    a = _silu(c_ref[...]).astype(BF16)
    o_ref[...] = _dot(a, w_ref[...].astype(BF16)) + b_ref[...]


def _mod_table(cc, w_mod, b_mod):
    n = w_mod.shape[1]
    tn = n // MOD_COL_TILES
    return pl.pallas_call(
        _mod_kernel,
        grid=(MOD_COL_TILES,),
        in_specs=[pl.BlockSpec((MOD_ROWS, D_MODEL), lambda j: (0, 0)),
                  pl.BlockSpec((D_MODEL, tn), lambda j: (0, j)),
                  pl.BlockSpec((1, tn), lambda j: (0, j))],
        out_specs=pl.BlockSpec((MOD_ROWS, tn), lambda j: (0, j)),
        out_shape=jax.ShapeDtypeStruct((MOD_ROWS, n), F32),
        compiler_params=_cparams(("arbitrary",)),
        name="mod_table",
    )(cc, w_mod, b_mod.reshape(1, n))


def _rope(x, cos, sin, lane_lo):
    swapped = jnp.where(lane_lo, pltpu.roll(x, LANES - 16, axis=1), pltpu.roll(x, 16, axis=1))
    return x * cos + swapped * sin


def _inproj_kernel(rows, xp_ref, xs_ref, mod_ref, g_ref, w_ref, cos_ref, sin_ref,
                   q_ref, k_ref, v_ref, z_ref, gt_ref, ks_ref, vs_ref):
    is_ctx = pl.program_id(0) < rows.tiles_ctx
    mod = mod_ref[0]
    shift, scale = mod[:, 0:D_MODEL], mod[:, D_MODEL:2 * D_MODEL]
    x = jnp.where(is_ctx, xp_ref[...], xs_ref[...])
    h = _rms(x, g_ref[...]) * (1.0 + scale) + shift
    hb = h.astype(BF16)
    q = _dot(hb, w_ref[:, 0:ATTN_W]) * (HEAD_DIM ** -0.5)
    k = _dot(hb, w_ref[:, ATTN_W:2 * ATTN_W])
    v = _dot(hb, w_ref[:, 2 * ATTN_W:3 * ATTN_W])

    @pl.when(is_ctx)
    def _():
        q_ref[...] = q.astype(BF16)
        k_ref[...] = k.astype(BF16)
        ks_ref[...] = k
        vs_ref[...] = v

    @pl.when(jnp.logical_not(is_ctx))
    def _():
        cos, sin = cos_ref[...], sin_ref[...]
        lane_lo = (lax.broadcasted_iota(jnp.int32, cos.shape, 1) % 32) < 16
        for j in range(ATTN_W // LANES):
            sl = slice(j * LANES, (j + 1) * LANES)
            q_ref[:, sl] = _rope(q[:, sl], cos, sin, lane_lo).astype(BF16)
            k_ref[:, sl] = _rope(k[:, sl], cos, sin, lane_lo).astype(BF16)

    v_ref[...] = v.astype(BF16)
    u = _dot(hb, w_ref[:, 3 * ATTN_W:3 * ATTN_W + 2 * CONV_W])
    z_ref[...] = u[:, :CONV_W] * jax.nn.sigmoid(u[:, CONV_W:])
    gt_ref[...] = jax.nn.sigmoid(_dot(hb, w_ref[:, 3 * ATTN_W + 2 * CONV_W:]))


def _inproj(rows, xp, xs, mod3, norm_g, w_in_b, rope):
    n_rows = rows.tiles * ROW_TILE
    rows_p = rows.tiles_ctx * ROW_TILE
    rope_spec = pl.BlockSpec((ROW_TILE, LANES), lambda i: (rows.lat_tile(i) % rows.tiles_per_lat, 0))
    shapes = lambda w, dt, r=n_rows: jax.ShapeDtypeStruct((r, w), dt)
    return pl.pallas_call(
        functools.partial(_inproj_kernel, rows),
        grid=(rows.tiles,),
        in_specs=[_row_spec(D_MODEL, rows.ctx_tile), _row_spec(D_MODEL, rows.lat_tile), _mod_spec(rows),
                  _const_spec((1, D_MODEL)),
                  pl.BlockSpec((D_MODEL, IN_COLS), lambda i: (0, 0), pipeline_mode=pl.Buffered(1)),
                  rope_spec, rope_spec],
        out_specs=[_row_spec(ATTN_W)] * 3 + [_row_spec(CONV_W), _row_spec(2 * D_MODEL)]
                  + [_row_spec(ATTN_W, rows.ctx_tile)] * 2,
        out_shape=[shapes(ATTN_W, BF16)] * 3 + [shapes(CONV_W, F32), shapes(2 * D_MODEL, F32)]
                  + [shapes(ATTN_W, F32, rows_p)] * 2,
        compiler_params=_cparams(("arbitrary",)),
        name="inproj",
    )(xp, xs, mod3, norm_g, w_in_b, *rope)


def _rope_tables(n_tokens):
    t = jnp.arange(n_tokens, dtype=jnp.int32)
    pos = jnp.stack([t // GRID_W, t % GRID_W], axis=-1).astype(F32)
    half = HEAD_DIM // 2
    inv = ROPE_BASE ** (-jnp.arange(0, half, 2, dtype=F32) / half)
    ang = pos[:, :, None] * inv
    cos, sin = jnp.cos(ang), jnp.sin(ang)
    cos64 = jnp.concatenate([cos[:, 0], cos[:, 0], cos[:, 1], cos[:, 1]], axis=-1)
    sin64 = jnp.concatenate([-sin[:, 0], sin[:, 0], -sin[:, 1], sin[:, 1]], axis=-1)
    return jnp.tile(cos64, (1, LANES // HEAD_DIM)), jnp.tile(sin64, (1, LANES // HEAD_DIM))


def _attn_kernel(has_cache, lq_ref, sg_ref, q_ref, k_ref, v_ref, *rest):
    if has_cache:
        ck_ref, cv_ref, o_ref = rest
    else:
        (o_ref,) = rest
    lq = lq_ref[...]
    lam = (jnp.exp(jnp.sum(lq[0:1] * lq[1:2], axis=-1, keepdims=True))
           - jnp.exp(jnp.sum(lq[2:3] * lq[3:4], axis=-1, keepdims=True)) + LAM_INIT)
    q = q_ref[...]
    first = lax.broadcasted_iota(jnp.int32, q.shape, 1) < HEAD_DIM
    zero = jnp.zeros_like(q)
    qs = (jnp.where(first, q, zero), jnp.where(first, zero, q))
    keys = [k_ref[...]]
    vals = [v_ref[...]]
    if has_cache:
        keys.append(ck_ref[...].astype(BF16))
        vals.append(cv_ref[...].astype(BF16))
    probs = []
    for qm in qs:
        s = [_dot_nt(qm, kk) for kk in keys]
        m = functools.reduce(jnp.maximum, [jnp.max(t, axis=-1, keepdims=True) for t in s])
        e = [jnp.exp(t - m) for t in s]
        l = functools.reduce(jnp.add, [jnp.sum(t, axis=-1, keepdims=True) for t in e])
        probs.append((e, 1.0 / l))
    (e1, r1), (e2, r2) = probs
    r2 = r2 * lam
    o = None
    for a, b, vv in zip(e1, e2, vals):
        w = (a * r1 - b * r2).astype(BF16)
        part = _dot(w, vv)
        o = part if o is None else o + part
    o_ref[...] = (_rms(o, sg_ref[...]) * (1.0 - LAM_INIT)).astype(BF16)


def _attention(q, k, v, lambda_qk, subln_g, row0, batch, seq_len, cache=None):
    tq = ROW_TILE
    q_tiles = seq_len // tq
    assert row0 % seq_len == 0
    seq0 = row0 // seq_len
    head_q = pl.BlockSpec((tq, LANES), lambda b, h, i: (seq0 * q_tiles + b * q_tiles + i, h))
    head_kv = pl.BlockSpec((seq_len, LANES), lambda b, h, i: (seq0 + b, h))
    in_specs = [pl.BlockSpec((4, HEAD_DIM), lambda b, h, i: (0, 0)),
                pl.BlockSpec((1, 2 * HEAD_DIM), lambda b, h, i: (0, 0)),
                head_q, head_kv, head_kv]
    args = [lambda_qk, subln_g, q, k, v]
    if cache is not None:
        past = cache[0].shape[1]
        head_cache = pl.BlockSpec((None, past, LANES), lambda b, h, i: (b, 0, h))
        in_specs += [head_cache, head_cache]
        args += list(cache)
    return pl.pallas_call(
        functools.partial(_attn_kernel, cache is not None),
        grid=(batch, N_HEADS, q_tiles),
        in_specs=in_specs,
        out_specs=pl.BlockSpec((tq, LANES), lambda b, h, i: (b * q_tiles + i, h)),
        out_shape=jax.ShapeDtypeStruct((batch * seq_len, ATTN_W), BF16),
        compiler_params=_cparams(("parallel", "parallel", "arbitrary")),
        name="attn_latent" if cache is not None else "attn_ctx",
    )(*args)


def _merge_kernel(rows, xp_ref, xs_ref, onp_ref, ons_ref, z_ref, zp_ref, zn_ref, gt_ref, mod_ref, cw_ref, cb_ref,
                  lg_ref, lb_ref, wco_ref, wap_ref, wout_ref, n2_ref, x1_ref, h2_ref, zext_ref):
    i = pl.program_id(0)
    is_ctx = i < rows.tiles_ctx
    tm = ROW_TILE
    t, per = rows.seq_tile(i)
    zext_ref[HALO:HALO + tm, :] = z_ref[...]
    zext_ref[0:HALO, :] = jnp.where(t == 0, 0.0, zp_ref[...])
    zext_ref[HALO + tm:, :] = jnp.where(t == per - 1, 0.0, zn_ref[...])
    pad = HALO - CONV_K // 2
    col_blocks = []
    for cb in range(CONV_W // LANES):
        cs = slice(cb * LANES, (cb + 1) * LANES)
        row_chunks = []
        for r0 in range(0, tm, CONV_ROWS):
            acc = jnp.zeros((CONV_ROWS, LANES), F32) + cb_ref[:, cs]
            for j in range(CONV_K):
                acc = acc + zext_ref[r0 + j + pad:r0 + j + pad + CONV_ROWS, cs] * cw_ref[j:j + 1, cs]
            row_chunks.append(acc)
        col_blocks.append(jnp.concatenate(row_chunks, axis=0))
    c = jnp.concatenate(col_blocks, axis=1)
    mu = jnp.mean(c, axis=-1, keepdims=True)
    cc = c - mu
    y = cc * lax.rsqrt(jnp.mean(cc * cc, axis=-1, keepdims=True) + EPS) * lg_ref[...] + lb_ref[...]
    conv_out = _dot(_silu(y).astype(BF16), wco_ref[...])
    o_n = jnp.where(is_ctx, onp_ref[...], ons_ref[...])
    a_br = _dot(o_n, wap_ref[...])
    gt = gt_ref[...]
    merged = gt[:, :D_MODEL] * a_br + gt[:, D_MODEL:] * conv_out
    mod = mod_ref[0]
    gate1 = mod[:, 2 * D_MODEL:3 * D_MODEL]
    shift2, scale2 = mod[:, 3 * D_MODEL:4 * D_MODEL], mod[:, 4 * D_MODEL:5 * D_MODEL]
    x = jnp.where(is_ctx, xp_ref[...], xs_ref[...])
    x1 = x + gate1 * _dot(merged.astype(BF16), wout_ref[...])
    x1_ref[...] = x1
    _to_slabs(h2_ref, _rms(x1, n2_ref[...]) * (1.0 + scale2) + shift2, tm)


def _merge(rows, xp, xs, on_p, on_s, z, gt, mod3, conv_w, conv_b, ln_g, ln_b, wco_b, wap_b, wout_b, norm2_g):
    n_rows = rows.tiles * ROW_TILE
    hb = ROW_TILE // HALO
    n_halo_blocks = n_rows // HALO
    in_specs = [_row_spec(D_MODEL, rows.ctx_tile), _row_spec(D_MODEL, rows.lat_tile),
                _row_spec(ATTN_W, rows.ctx_tile), _row_spec(ATTN_W, rows.lat_tile),
                _row_spec(CONV_W),
                pl.BlockSpec((HALO, CONV_W), lambda i: (jnp.maximum(i * hb - 1, 0), 0)),
                pl.BlockSpec((HALO, CONV_W), lambda i: (jnp.minimum((i + 1) * hb, n_halo_blocks - 1), 0)),
                _row_spec(2 * D_MODEL), _mod_spec(rows),
                _const_spec((CONV_K, CONV_W)), _const_spec((1, CONV_W)), _const_spec((1, CONV_W)),
                _const_spec((1, CONV_W)), _const_spec((CONV_W, D_MODEL)), _const_spec((ATTN_W, D_MODEL)),
                _const_spec((D_MODEL, D_MODEL)), _const_spec((1, D_MODEL))]
    return pl.pallas_call(
        functools.partial(_merge_kernel, rows),
        grid=(rows.tiles,),
        in_specs=in_specs,
        out_specs=[_row_spec(D_MODEL), pl.BlockSpec((ROW_TILE * SLAB, LANES), lambda i: (i, 0))],
        out_shape=[jax.ShapeDtypeStruct((n_rows, D_MODEL), F32),
                   jax.ShapeDtypeStruct((n_rows * SLAB, LANES), F32)],
        scratch_shapes=[pltpu.VMEM((ROW_TILE + 2 * HALO, CONV_W), F32)],
        compiler_params=_cparams(("arbitrary",)),
        name="conv_merge",
    )(xp, xs, on_p, on_s, z, z, z, gt, mod3, conv_w, conv_b, ln_g, ln_b, wco_b, wap_b, wout_b, norm2_g)


def _pack_cols(cols, shape):
    lane = lax.broadcasted_iota(jnp.int32, shape, 1)
    out = jnp.zeros(shape, F32)
    for j, col in enumerate(cols):
        out = jnp.where(lane == j, col, out)
    return out


def _router_kernel(h_ref, wr_ref, rb_ref, idx_ref, gate_ref, pos_ref, cnt_ref, run_ref):
    i = pl.program_id(0)

    @pl.when(i == 0)
    def _():
        run_ref[...] = jnp.zeros_like(run_ref)

    tm = ROW_TILE
    neg = jnp.float32(-jnp.inf)
    scores = jax.nn.sigmoid(_dot(_from_slabs(h_ref, tm).astype(BF16), wr_ref[...]))
    biased = scores + rb_ref[...]
    lane_i = lax.broadcasted_iota(jnp.int32, scores.shape, 1)
    lane = lane_i.astype(F32)
    far = jnp.float32(2 * N_EXPERTS)

    def first_argmax(v):
        m = jnp.max(v, axis=-1, keepdims=True)
        return m, jnp.min(jnp.where(v == m, lane, far), axis=-1, keepdims=True)

    in_group, gscore = [], []
    for g in range(N_GROUPS):
        inb = (lane_i >= g * GROUP_SIZE) & (lane_i < (g + 1) * GROUP_SIZE)
        v = jnp.where(inb, biased, neg)
        m1, i1 = first_argmax(v)
        m2 = jnp.max(jnp.where(lane == i1, neg, v), axis=-1, keepdims=True)
        in_group.append(inb)
        gscore.append(m1 + m2)
    allowed = jnp.zeros(scores.shape, F32)
    for g in range(N_GROUPS):
        ahead = jnp.zeros((tm, 1), F32)
        for g2 in range(N_GROUPS):
            if g2 < g:
                ahead = ahead + (gscore[g2] >= gscore[g]).astype(F32)
            elif g2 > g:
                ahead = ahead + (gscore[g2] > gscore[g]).astype(F32)
        keep = (ahead < TOPK_GROUPS).astype(F32)
        allowed = jnp.where(in_group[g], keep, allowed)
    masked = jnp.where(allowed > 0.0, biased, neg)
    picked = jnp.zeros(scores.shape, F32)
    idxs, gates = [], []
    for _ in range(TOP_K):
        _, ik = first_argmax(masked)
        hit = lane == ik
        gates.append(jnp.sum(jnp.where(hit, scores, 0.0), axis=-1, keepdims=True))
        masked = jnp.where(hit, neg, masked)
        picked = jnp.where(hit, 1.0, picked)
        idxs.append(ik)
    gsum = functools.reduce(jnp.add, gates)
    gates = [g / gsum * ROUTE_SCALE for g in gates]
    r_i = lax.broadcasted_iota(jnp.int32, (tm, tm), 0)
    c_i = lax.broadcasted_iota(jnp.int32, (tm, tm), 1)
    before = (c_i < r_i).astype(BF16)
    rank = _dot(before, picked.astype(BF16)) + run_ref[...]
    poss = [jnp.sum(jnp.where(lane == ik, rank, 0.0), axis=-1, keepdims=True) for ik in idxs]
    run_ref[...] = run_ref[...] + jnp.sum(picked, axis=0, keepdims=True)
    cnt_ref[...] = run_ref[...]
    shape = idx_ref.shape
    idx_ref[...] = _pack_cols(idxs, shape).astype(jnp.int32)
    gate_ref[...] = _pack_cols(gates, shape)
    pos_ref[...] = _pack_cols(poss, shape).astype(jnp.int32)


def _router(rows, h2_slab, w_router_b, router_bias):
    n_rows = rows.tiles * ROW_TILE
    return pl.pallas_call(
        _router_kernel,
        grid=(rows.tiles,),
        in_specs=[pl.BlockSpec((ROW_TILE * SLAB, LANES), lambda i: (i, 0)),
                  _const_spec((D_MODEL, N_EXPERTS)), _const_spec((1, N_EXPERTS))],
        out_specs=[_row_spec(LANES), _row_spec(LANES), _row_spec(LANES), _const_spec((1, N_EXPERTS))],
        out_shape=[jax.ShapeDtypeStruct((n_rows, LANES), jnp.int32), jax.ShapeDtypeStruct((n_rows, LANES), F32),
                   jax.ShapeDtypeStruct((n_rows, LANES), jnp.int32), jax.ShapeDtypeStruct((1, N_EXPERTS), F32)],
        scratch_shapes=[pltpu.VMEM((1, N_EXPERTS), F32)],
        compiler_params=_cparams(("arbitrary",)),
        name="router",
    )(h2_slab, w_router_b, router_bias)


def _dispatch_kernel(pe_ref, dest_hbm, h_ref, xs_hbm, dsm, zbuf, sem_i, sem):
    i = pl.program_id(0)
    tm = ROW_TILE
    n_slots = tm * TOP_K

    @pl.when(i == 0)
    def _():
        zbuf[...] = jnp.zeros_like(zbuf)
        n_blocks = xs_hbm.shape[0] // (MOE_BLOCK * SLAB)
        n_used = pe_ref[N_EXPERTS - 1] // MOE_BLOCK

        def clear_block(b):
            dst = xs_hbm.at[pl.ds(pl.multiple_of(b * (MOE_BLOCK * SLAB), MOE_BLOCK * SLAB), MOE_BLOCK * SLAB)]
            pltpu.make_async_copy(zbuf, dst, sem).start()

        def clear_last(e, c):
            clear_block(jnp.maximum(pe_ref[e] // MOE_BLOCK - 1, 0))
            return c

        def clear_tail(b, c):
            clear_block(b)
            return c

        def clear_wait(e, c):
            pltpu.make_async_copy(zbuf, xs_hbm.at[pl.ds(0, MOE_BLOCK * SLAB)], sem).wait()
            return c

        lax.fori_loop(0, N_EXPERTS, clear_last, 0)
        lax.fori_loop(n_used, n_blocks, clear_tail, 0)
        lax.fori_loop(0, N_EXPERTS + n_blocks - n_used, clear_wait, 0)

    cp = pltpu.make_async_copy(dest_hbm.at[pl.ds(i * n_slots, n_slots)], dsm, sem_i)
    cp.start()
    cp.wait()

    def issue(r, c):
        for k in range(TOP_K):
            d = dsm[r * TOP_K + k]
            pltpu.make_async_copy(h_ref.at[_slab(r)], xs_hbm.at[_slab(d)], sem).start()
        return c

    lax.fori_loop(0, tm, issue, 0)
    for _ in range(TOP_K):
        pltpu.make_async_copy(h_ref, xs_hbm.at[pl.ds(0, tm * SLAB)], sem).wait()


def _dispatch(rows, pad_end, dest_flat, h2_slab, n_rows_sorted):
    return pl.pallas_call(
        _dispatch_kernel,
        grid_spec=pltpu.PrefetchScalarGridSpec(
            num_scalar_prefetch=1,
            grid=(rows.tiles,),
            in_specs=[pl.BlockSpec(memory_space=pl.ANY),
                      pl.BlockSpec((ROW_TILE * SLAB, LANES), lambda i, pe: (i, 0))],
            out_specs=pl.BlockSpec(memory_space=pl.ANY),
            scratch_shapes=[pltpu.SMEM((ROW_TILE * TOP_K,), jnp.int32),
                            pltpu.VMEM((MOE_BLOCK * SLAB, LANES), F32),
                            pltpu.SemaphoreType.DMA, pltpu.SemaphoreType.DMA]),
        out_shape=jax.ShapeDtypeStruct((n_rows_sorted * SLAB, LANES), F32),
        compiler_params=_cparams(("arbitrary",)),
        name="moe_dispatch",
    )(pad_end, dest_flat, h2_slab)


def _expert_kernel(be_ref, nu_ref, x_ref, w1_ref, w3_ref, w2_ref, y_ref, w1b, w3b, w2b):
    b = pl.program_id(0)

    @pl.when(b < nu_ref[0])
    def _():
        changed = (b == 0) | (be_ref[b] != be_ref[jnp.maximum(b - 1, 0)])

        @pl.when(changed)
        def _():
            w1b[...] = w1_ref[...].astype(BF16)
            w3b[...] = w3_ref[...].astype(BF16)
            w2b[...] = w2_ref[...].astype(BF16)

        xb = _from_slabs(x_ref, MOE_BLOCK).astype(BF16)
        hid = (_silu(_dot(xb, w1b[...])) * _dot(xb, w3b[...])).astype(BF16)
        _to_slabs(y_ref, _dot(hid, w2b[...]), MOE_BLOCK)


def _experts(block_e, n_used, xs, w1, w3, w2):
    n_blocks = xs.shape[0] // (MOE_BLOCK * SLAB)
    row_map = lambda b, be, nu: (jnp.minimum(b, nu[0] - 1), 0)
    return pl.pallas_call(
        _expert_kernel,
        grid_spec=pltpu.PrefetchScalarGridSpec(
            num_scalar_prefetch=2,
            grid=(n_blocks,),
            in_specs=[pl.BlockSpec((MOE_BLOCK * SLAB, LANES), row_map),
                      pl.BlockSpec((None, D_MODEL, EXPERT_HIDDEN), lambda b, be, nu: (be[b], 0, 0)),
                      pl.BlockSpec((None, D_MODEL, EXPERT_HIDDEN), lambda b, be, nu: (be[b], 0, 0)),
                      pl.BlockSpec((None, EXPERT_HIDDEN, D_MODEL), lambda b, be, nu: (be[b], 0, 0))],
            out_specs=pl.BlockSpec((MOE_BLOCK * SLAB, LANES), row_map),
            scratch_shapes=[pltpu.VMEM((D_MODEL, EXPERT_HIDDEN), BF16),
                            pltpu.VMEM((D_MODEL, EXPERT_HIDDEN), BF16),
                            pltpu.VMEM((EXPERT_HIDDEN, D_MODEL), BF16)]),
        out_shape=jax.ShapeDtypeStruct(xs.shape, F32),
        input_output_aliases={2: 0},
        compiler_params=_cparams(("arbitrary",)),
        name="moe_experts",
    )(block_e, n_used, xs, w1, w3, w2)


def _combine_kernel(rows, dest_hbm, ys_hbm, x1_ref, h_ref, gate_ref, mod_ref, ws1_ref, ws3_ref, ws2_ref, nf_ref,
                    outp_ref, outs_ref, dsm, ybuf, sem_i, sem):
    i = pl.program_id(0)
    tm = ROW_TILE
    n_slots = tm * TOP_K
    cp = pltpu.make_async_copy(dest_hbm.at[pl.ds(i * n_slots, n_slots)], dsm, sem_i)
    cp.start()
    cp.wait()

    def issue(r, c):
        for k in range(TOP_K):
            d = dsm[r * TOP_K + k]
            pltpu.make_async_copy(ys_hbm.at[_slab(d)], ybuf.at[k, _slab(r)], sem).start()
        return c

    lax.fori_loop(0, tm, issue, 0)
    hb = _from_slabs(h_ref, tm).astype(BF16)
    shared = _dot((_silu(_dot(hb, ws1_ref[...])) * _dot(hb, ws3_ref[...])).astype(BF16), ws2_ref[...])
    for k in range(TOP_K):
        pltpu.make_async_copy(ys_hbm.at[pl.ds(0, tm * SLAB)], ybuf.at[k], sem).wait()
    g = gate_ref[...]
    routed = _from_slabs(ybuf, tm, (0,)) * g[:, 0:1]
    for k in range(1, TOP_K):
        routed = routed + _from_slabs(ybuf, tm, (k,)) * g[:, k:k + 1]
    gate2 = mod_ref[0][:, 5 * D_MODEL:6 * D_MODEL]
    x2 = x1_ref[...] + gate2 * (routed + shared)
    out = _rms(x2, nf_ref[...])

    @pl.when(i < rows.tiles_ctx)
    def _():
        outp_ref[...] = out

    @pl.when(i >= rows.tiles_ctx)
    def _():
        outs_ref[...] = out


def _combine(rows, dest_flat, ys, x1, h2_slab, gates, mod3, ws1_b, ws3_b, ws2_b, normf_g):
    return pl.pallas_call(
        functools.partial(_combine_kernel, rows),
        grid=(rows.tiles,),
        in_specs=[pl.BlockSpec(memory_space=pl.ANY), pl.BlockSpec(memory_space=pl.ANY),
                  _row_spec(D_MODEL), pl.BlockSpec((ROW_TILE * SLAB, LANES), lambda i: (i, 0)),
                  _row_spec(LANES), _mod_spec(rows),
                  _const_spec((D_MODEL, EXPERT_HIDDEN)), _const_spec((D_MODEL, EXPERT_HIDDEN)),
                  _const_spec((EXPERT_HIDDEN, D_MODEL)), _const_spec((1, D_MODEL))],
        out_specs=[_row_spec(D_MODEL, rows.ctx_tile), _row_spec(D_MODEL, rows.lat_tile)],
        out_shape=[jax.ShapeDtypeStruct((rows.tiles_ctx * ROW_TILE, D_MODEL), F32),
                   jax.ShapeDtypeStruct((rows.tiles_lat * ROW_TILE, D_MODEL), F32)],
        scratch_shapes=[pltpu.SMEM((ROW_TILE * TOP_K,), jnp.int32),
                        pltpu.VMEM((TOP_K, ROW_TILE * SLAB, LANES), F32),
                        pltpu.SemaphoreType.DMA, pltpu.SemaphoreType.DMA],
        compiler_params=_cparams(("arbitrary",)),
        name="moe_combine",
    )(dest_flat, ys, x1, h2_slab, gates, mod3, ws1_b, ws3_b, ws2_b, normf_g)


def kernel(x_prompt, x_sample, cache_k, cache_v, c, c_ctx, w_mod, b_mod, norm1_g, w_in, lambda_qk, subln_g,
           w_attn_proj, conv_w, conv_b, conv_ln_g, conv_ln_b, w_conv_out, w_out, norm2_g, w_router,
           router_bias, w1, w3, w2, ws1, ws3, ws2, normf_g):
    batch, seq, _ = x_prompt.shape
    dec_batch, dec_seq, _ = x_sample.shape
    past = cache_k.shape[2]
    l = 0
    rows_p, rows_s = batch * seq, dec_batch * dec_seq
    n_rows = rows_p + rows_s
    rows = Rows(rows_p // ROW_TILE, rows_s // ROW_TILE, seq // ROW_TILE, dec_seq // ROW_TILE)
    xp = x_prompt.reshape(rows_p, D_MODEL)
    xs = x_sample.reshape(rows_s, D_MODEL)
    row = lambda a: a.reshape(1, -1)

    cc = jnp.zeros((MOD_ROWS, D_MODEL), F32).at[0].set(c_ctx).at[1:1 + dec_batch].set(c)
    mod3 = _mod_table(cc, w_mod[l], b_mod[l]).reshape(MOD_ROWS, 1, N_MOD * D_MODEL)

    q, k, v, z, gt, state_k, state_v = _inproj(rows, xp, xs, mod3, row(norm1_g[l]), w_in[l].astype(BF16),
                                               _rope_tables(dec_seq))
    lq, sg = lambda_qk[l], row(subln_g[l])
    on_p = _attention(q, k, v, lq, sg, 0, batch, seq)
    cache = (cache_k[:, l].reshape(dec_batch, past, ATTN_W), cache_v[:, l].reshape(dec_batch, past, ATTN_W))
    on_s = _attention(q, k, v, lq, sg, rows_p, dec_batch, dec_seq, cache=cache)
    x1, h2 = _merge(rows, xp, xs, on_p, on_s, z, gt, mod3, conv_w[l], row(conv_b[l]), row(conv_ln_g[l]),
                    row(conv_ln_b[l]), w_conv_out[l].astype(BF16), w_attn_proj[l].astype(BF16),
                    w_out[l].astype(BF16), row(norm2_g[l]))

    idx, gates, pos, counts = _router(rows, h2, w_router[l].astype(BF16), row(router_bias[l]))
    counts = counts[0].astype(jnp.int32)
    padded = (counts + MOE_BLOCK - 1) // MOE_BLOCK * MOE_BLOCK
    pad_end = jnp.cumsum(padded).astype(jnp.int32)
    pad_start = pad_end - padded
    experts = jnp.arange(N_EXPERTS, dtype=jnp.int32)
    start_of = jnp.sum(jnp.where(idx[:, :TOP_K, None] == experts, pad_start, 0), axis=-1)
    dest = (start_of + pos[:, :TOP_K]).reshape(-1)
    n_blocks = n_rows * TOP_K // MOE_BLOCK + N_EXPERTS
    blocks = jnp.arange(n_blocks, dtype=jnp.int32)
    block_e = jnp.minimum(jnp.sum((pad_end // MOE_BLOCK)[None, :] <= blocks[:, None], axis=-1),
                          N_EXPERTS - 1).astype(jnp.int32)
    n_used = (pad_end[-1:] // MOE_BLOCK).astype(jnp.int32)
    x_sorted = _dispatch(rows, pad_end, dest, h2, n_blocks * MOE_BLOCK)
    y_sorted = _experts(block_e, n_used, x_sorted, w1[l], w3[l], w2[l])
    y_p, y_s = _combine(rows, dest, y_sorted, x1, h2, gates, mod3, ws1[l].astype(BF16), ws3[l].astype(BF16),
                        ws2[l].astype(BF16), row(normf_g))
    return (y_p.reshape(batch, seq, D_MODEL), y_s.reshape(dec_batch, dec_seq, D_MODEL),
            state_k.reshape(batch, 1, seq, N_HEADS, 2, HEAD_DIM),
            state_v.reshape(batch, 1, seq, N_HEADS, 2 * HEAD_DIM))
```

```python
import functools
import math
from typing import NamedTuple

import jax
import jax.numpy as jnp
from jax import lax
from jax.experimental import pallas as pl
from jax.experimental.pallas import tpu as pltpu

D_MODEL = 1024
GRID_W = 64
N_HEADS = 8
HEAD_DIM = 64
ATTN_W = N_HEADS * 2 * HEAD_DIM
CONV_W = 512
CONV_K = 31
N_EXPERTS = 256
TOP_K = 8
N_GROUPS = 8
TOPK_GROUPS = 4
GROUP_SIZE = N_EXPERTS // N_GROUPS
EXPERT_HIDDEN = 256
ROUTE_SCALE = 2.5
ROPE_BASE = 10000.0
EPS = 1e-6
N_MOD = 6
IN_COLS = 3 * ATTN_W + 2 * CONV_W + 2 * D_MODEL
LAM_INIT = 0.8 - 0.6 * math.exp(-0.3 * 0)
LOG2E = math.log2(math.e)

LANES = 128
SUBLANES = 8
VMEM_LIMIT = 56 * 1024 * 1024
HALO = 16
CONV_ROWS = 64
ROW_TILE = 256
CTX_HEADS_PER_STEP = 8
LAT_HEADS_PER_STEP = 2
ATTN_KEY_CHUNK = 512
MOE_BLOCK = 256
MOD_ROWS = 8
MOD_COL_TILES = 4
SLAB = D_MODEL // LANES
assert SLAB == SUBLANES

BF16 = jnp.bfloat16
F32 = jnp.float32


class Rows(NamedTuple):
    tiles_ctx: int
    tiles_lat: int
    tiles_per_ctx: int
    tiles_per_lat: int

    @property
    def tiles(self):
        return self.tiles_ctx + self.tiles_lat

    def ctx_tile(self, i):
        return jnp.minimum(i, self.tiles_ctx - 1)

    def lat_tile(self, i):
        return jnp.maximum(i - self.tiles_ctx, 0)

    def mod_row(self, i):
        return jnp.where(i < self.tiles_ctx, 0, 1 + self.lat_tile(i) // self.tiles_per_lat)

    def seq_tile(self, i):
        is_ctx = i < self.tiles_ctx
        per = jnp.where(is_ctx, self.tiles_per_ctx, self.tiles_per_lat)
        return jnp.where(is_ctx, i % self.tiles_per_ctx, self.lat_tile(i) % self.tiles_per_lat), per


def _cparams(sem):
    return pltpu.CompilerParams(dimension_semantics=sem, vmem_limit_bytes=VMEM_LIMIT)


def _row_spec(width, tile=lambda i: i):
    return pl.BlockSpec((ROW_TILE, width), lambda i: (tile(i), 0))


def _const_spec(shape):
    return pl.BlockSpec(shape, lambda i: (0,) * len(shape))


def _mod_spec(rows):
    return pl.BlockSpec((1, 1, N_MOD * D_MODEL), lambda i: (rows.mod_row(i), 0, 0))


def _silu(x):
    return x * jax.nn.sigmoid(x)


def _dot(a, b):
    return jnp.dot(a, b, preferred_element_type=F32)


def _dot_nt(a, b):
    return lax.dot_general(a, b, (((1,), (1,)), ((), ())), preferred_element_type=F32)


def _rms(x, g):
    return x * lax.rsqrt(jnp.mean(x * x, axis=-1, keepdims=True) + EPS) * g


def _slab(row):
    return pl.ds(pl.multiple_of(row * SLAB, SLAB), SLAB)


def _from_slabs(ref, n_tokens, lead=()):
    return jnp.concatenate([ref[lead + (pl.ds(s, n_tokens, stride=SLAB), slice(None))] for s in range(SLAB)],
                           axis=1)


def _to_slabs(ref, x, n_tokens):
    for s in range(SLAB):
        ref[pl.ds(s, n_tokens, stride=SLAB), :] = x[:, s * LANES:(s + 1) * LANES]


def _mod_kernel(c_ref, w_ref, b_ref, o_ref):
    a = _silu(c_ref[...]).astype(BF16)
    o_ref[...] = _dot(a, w_ref[...].astype(BF16)) + b_ref[...]


def _mod_table(cc, w_mod, b_mod):
    n = w_mod.shape[1]
    tn = n // MOD_COL_TILES
    return pl.pallas_call(
        _mod_kernel,
        grid=(MOD_COL_TILES,),
        in_specs=[pl.BlockSpec((MOD_ROWS, D_MODEL), lambda j: (0, 0)),
                  pl.BlockSpec((D_MODEL, tn), lambda j: (0, j)),
                  pl.BlockSpec((1, tn), lambda j: (0, j))],
        out_specs=pl.BlockSpec((MOD_ROWS, tn), lambda j: (0, j)),
        out_shape=jax.ShapeDtypeStruct((MOD_ROWS, n), F32),
        compiler_params=_cparams(("arbitrary",)),
        name="mod_table",
    )(cc, w_mod, b_mod.reshape(1, n))


def _rope(x, cos, sin, lane_lo):
    swapped = jnp.where(lane_lo, pltpu.roll(x, LANES - 16, axis=1), pltpu.roll(x, 16, axis=1))
    return x * cos + swapped * sin


def _inproj_kernel(rows, xp_ref, xs_ref, mod_ref, g_ref, w_ref, cos_ref, sin_ref,
                   q_ref, k_ref, v_ref, z_ref, gt_ref, ks_ref, vs_ref):
    is_ctx = pl.program_id(0) < rows.tiles_ctx
    mod = mod_ref[0]
    shift, scale = mod[:, 0:D_MODEL], mod[:, D_MODEL:2 * D_MODEL]
    x = jnp.where(is_ctx, xp_ref[...], xs_ref[...])
    h = _rms(x, g_ref[...]) * (1.0 + scale) + shift
    hb = h.astype(BF16)
    q = _dot(hb, w_ref[:, 0:ATTN_W]) * (HEAD_DIM ** -0.5 * LOG2E)
    k = _dot(hb, w_ref[:, ATTN_W:2 * ATTN_W])
    v = _dot(hb, w_ref[:, 2 * ATTN_W:3 * ATTN_W])

    @pl.when(is_ctx)
    def _():
        q_ref[...] = q.astype(BF16)
        k_ref[...] = k.astype(BF16)
        ks_ref[...] = k
        vs_ref[...] = v

    @pl.when(jnp.logical_not(is_ctx))
    def _():
        cos, sin = cos_ref[...], sin_ref[...]
        lane_lo = (lax.broadcasted_iota(jnp.int32, cos.shape, 1) % 32) < 16
        for j in range(ATTN_W // LANES):
            sl = slice(j * LANES, (j + 1) * LANES)
            q_ref[:, sl] = _rope(q[:, sl], cos, sin, lane_lo).astype(BF16)
            k_ref[:, sl] = _rope(k[:, sl], cos, sin, lane_lo).astype(BF16)

    v_ref[...] = v.astype(BF16)
    u = _dot(hb, w_ref[:, 3 * ATTN_W:3 * ATTN_W + 2 * CONV_W])
    z_ref[...] = u[:, :CONV_W] * jax.nn.sigmoid(u[:, CONV_W:])
    gt_ref[...] = jax.nn.sigmoid(_dot(hb, w_ref[:, 3 * ATTN_W + 2 * CONV_W:]))


def _inproj(rows, xp, xs, mod3, norm_g, w_in_b, rope):
    n_rows = rows.tiles * ROW_TILE
    rows_p = rows.tiles_ctx * ROW_TILE
    rope_spec = pl.BlockSpec((ROW_TILE, LANES), lambda i: (rows.lat_tile(i) % rows.tiles_per_lat, 0))
    shapes = lambda w, dt, r=n_rows: jax.ShapeDtypeStruct((r, w), dt)
    return pl.pallas_call(
        functools.partial(_inproj_kernel, rows),
        grid=(rows.tiles,),
        in_specs=[_row_spec(D_MODEL, rows.ctx_tile), _row_spec(D_MODEL, rows.lat_tile), _mod_spec(rows),
                  _const_spec((1, D_MODEL)),
                  pl.BlockSpec((D_MODEL, IN_COLS), lambda i: (0, 0), pipeline_mode=pl.Buffered(1)),
                  rope_spec, rope_spec],
        out_specs=[_row_spec(ATTN_W)] * 3 + [_row_spec(CONV_W), _row_spec(2 * D_MODEL)]
                  + [_row_spec(ATTN_W, rows.ctx_tile)] * 2,
        out_shape=[shapes(ATTN_W, BF16)] * 3 + [shapes(CONV_W, F32), shapes(2 * D_MODEL, F32)]
                  + [shapes(ATTN_W, F32, rows_p)] * 2,
        compiler_params=_cparams(("arbitrary",)),
        name="inproj",
    )(xp, xs, mod3, norm_g, w_in_b, *rope)


def _rope_tables(n_tokens):
    t = jnp.arange(n_tokens, dtype=jnp.int32)
    pos = jnp.stack([t // GRID_W, t % GRID_W], axis=-1).astype(F32)
    half = HEAD_DIM // 2
    inv = ROPE_BASE ** (-jnp.arange(0, half, 2, dtype=F32) / half)
    ang = pos[:, :, None] * inv
    cos, sin = jnp.cos(ang), jnp.sin(ang)
    cos64 = jnp.concatenate([cos[:, 0], cos[:, 0], cos[:, 1], cos[:, 1]], axis=-1)
    sin64 = jnp.concatenate([-sin[:, 0], sin[:, 0], -sin[:, 1], sin[:, 1]], axis=-1)
    return jnp.tile(cos64, (1, LANES // HEAD_DIM)), jnp.tile(sin64, (1, LANES // HEAD_DIM))


def _lane_groups(x):
    return [x[:, j * LANES:(j + 1) * LANES] for j in range(x.shape[1] // LANES)]


def _attn_kernel(has_cache, heads, lq_ref, sg_ref, q_ref, k_ref, v_ref, *rest):
    if has_cache:
        ck_ref, cv_ref, o_ref, *bufs = rest
    else:
        o_ref, *bufs = rest

    def s_ref(u, mp):
        return bufs[(u % 2) * 2 + mp]

    lq = lq_ref[...]
    lam = (jnp.exp(jnp.sum(lq[0:1] * lq[1:2], axis=-1, keepdims=True))
           - jnp.exp(jnp.sum(lq[2:3] * lq[3:4], axis=-1, keepdims=True)) + LAM_INIT)
    tq, seq = q_ref.shape[0], k_ref.shape[0]
    chunks = [(off, min(ATTN_KEY_CHUNK, seq - off), False) for off in range(0, seq, ATTN_KEY_CHUNK)]
    if has_cache:
        chunks.append((seq, ck_ref.shape[0], True))
    first = lax.broadcasted_iota(jnp.int32, (tq, LANES), 1) < HEAD_DIM
    neg = jnp.full((tq, LANES), -jnp.inf, F32)
    heads_state = [dict() for _ in range(heads)]

    def lanes(u):
        return slice(u * LANES, (u + 1) * LANES)

    def load(main_ref, cache_ref, u, chunk):
        off, size, cached = chunk
        if cached:
            return cache_ref[:, lanes(u)].astype(BF16)
        return main_ref[off:off + size, lanes(u)]

    def scores(u, chunk):
        st = heads_state[u]
        if "q" not in st:
            q = q_ref[:, lanes(u)]
            zero = jnp.zeros_like(q)
            st["q"] = (jnp.where(first, q, zero), jnp.where(first, zero, q))
            st["macc"] = [neg, neg]
        kk = load(k_ref, ck_ref if has_cache else None, u, chunk)
        off, size, _ = chunk
        for mp in range(2):
            s = _dot_nt(st["q"][mp], kk)
            s_ref(u, mp)[:, off:off + size] = s
            st["macc"][mp] = functools.reduce(jnp.maximum, _lane_groups(s), st["macc"][mp])

    def exps(u, chunk):
        st = heads_state[u]
        if "m" not in st:
            st["m"] = [jnp.max(a, axis=-1, keepdims=True) for a in st["macc"]]
            st["lacc"] = [jnp.zeros((tq, LANES), F32)] * 2
        off, size, _ = chunk
        for mp in range(2):
            e = jnp.exp2(s_ref(u, mp)[:, off:off + size] - st["m"][mp])
            s_ref(u, mp)[:, off:off + size] = e
            st["lacc"][mp] = functools.reduce(jnp.add, _lane_groups(e), st["lacc"][mp])

    def values(u, chunk):
        st = heads_state[u]
        if "r" not in st:
            l1, l2 = [jnp.sum(a, axis=-1, keepdims=True) for a in st["lacc"]]
            st["r"] = (1.0 / l1, lam / l2)
            st["o"] = jnp.zeros((tq, LANES), F32)
        off, size, _ = chunk
        w = s_ref(u, 0)[:, off:off + size] * st["r"][0] - s_ref(u, 1)[:, off:off + size] * st["r"][1]
        st["o"] = st["o"] + _dot(w.astype(BF16), load(v_ref, cv_ref if has_cache else None, u, chunk))

    stages = (scores, exps, values)
    for phase in range(heads + len(stages) - 1):
        for chunk in chunks:
            for s in reversed(range(len(stages))):
                u = phase - s
                if 0 <= u < heads:
                    stages[s](u, chunk)
        u = phase - (len(stages) - 1)
        if 0 <= u < heads:
            o_ref[:, lanes(u)] = (_rms(heads_state[u]["o"], sg_ref[...]) * (1.0 - LAM_INIT)).astype(BF16)


def _attention(q, k, v, lambda_qk, subln_g, row0, batch, seq_len, heads, cache=None):
    tq = ROW_TILE
    q_tiles = seq_len // tq
    assert row0 % seq_len == 0
    seq0 = row0 // seq_len
    width = heads * LANES
    head_q = pl.BlockSpec((tq, width), lambda b, h, i: (seq0 * q_tiles + b * q_tiles + i, h))
    head_kv = pl.BlockSpec((seq_len, width), lambda b, h, i: (seq0 + b, h))
    in_specs = [pl.BlockSpec((4, HEAD_DIM), lambda b, h, i: (0, 0)),
                pl.BlockSpec((1, 2 * HEAD_DIM), lambda b, h, i: (0, 0)),
                head_q, head_kv, head_kv]
    args = [lambda_qk, subln_g, q, k, v]
    n_keys = seq_len
    if cache is not None:
        past = cache[0].shape[1]
        n_keys += past
        head_cache = pl.BlockSpec((None, past, width), lambda b, h, i: (b, 0, h))
        in_specs += [head_cache, head_cache]
        args += list(cache)
    return pl.pallas_call(
        functools.partial(_attn_kernel, cache is not None, heads),
        grid=(batch, N_HEADS // heads, q_tiles),
        in_specs=in_specs,
        out_specs=pl.BlockSpec((tq, width), lambda b, h, i: (b * q_tiles + i, h)),
        out_shape=jax.ShapeDtypeStruct((batch * seq_len, ATTN_W), BF16),
        scratch_shapes=[pltpu.VMEM((tq, n_keys), F32)] * 4,
        compiler_params=_cparams(("parallel", "parallel", "arbitrary")),
        name="attn_latent" if cache is not None else "attn_ctx",
    )(*args)


def _merge_kernel(rows, xp_ref, xs_ref, onp_ref, ons_ref, z_ref, zp_ref, zn_ref, gt_ref, mod_ref, cw_ref, cb_ref,
                  lg_ref, lb_ref, wco_ref, wap_ref, wout_ref, n2_ref, x1_ref, h2_ref, zext_ref):
    i = pl.program_id(0)
    is_ctx = i < rows.tiles_ctx
    tm = ROW_TILE
    t, per = rows.seq_tile(i)
    zext_ref[HALO:HALO + tm, :] = z_ref[...]
    zext_ref[0:HALO, :] = jnp.where(t == 0, 0.0, zp_ref[...])
    zext_ref[HALO + tm:, :] = jnp.where(t == per - 1, 0.0, zn_ref[...])
    pad = HALO - CONV_K // 2
    col_blocks = []
    for cb in range(CONV_W // LANES):
        cs = slice(cb * LANES, (cb + 1) * LANES)
        row_chunks = []
        for r0 in range(0, tm, CONV_ROWS):
            acc = jnp.zeros((CONV_ROWS, LANES), F32) + cb_ref[:, cs]
            for j in range(CONV_K):
                acc = acc + zext_ref[r0 + j + pad:r0 + j + pad + CONV_ROWS, cs] * cw_ref[j:j + 1, cs]
            row_chunks.append(acc)
        col_blocks.append(jnp.concatenate(row_chunks, axis=0))
    c = jnp.concatenate(col_blocks, axis=1)
    mu = jnp.mean(c, axis=-1, keepdims=True)
    cc = c - mu
    y = cc * lax.rsqrt(jnp.mean(cc * cc, axis=-1, keepdims=True) + EPS) * lg_ref[...] + lb_ref[...]
    conv_out = _dot(_silu(y).astype(BF16), wco_ref[...])
    o_n = jnp.where(is_ctx, onp_ref[...], ons_ref[...])
    a_br = _dot(o_n, wap_ref[...])
    gt = gt_ref[...]
    merged = gt[:, :D_MODEL] * a_br + gt[:, D_MODEL:] * conv_out
    mod = mod_ref[0]
    gate1 = mod[:, 2 * D_MODEL:3 * D_MODEL]
    shift2, scale2 = mod[:, 3 * D_MODEL:4 * D_MODEL], mod[:, 4 * D_MODEL:5 * D_MODEL]
    x = jnp.where(is_ctx, xp_ref[...], xs_ref[...])
    x1 = x + gate1 * _dot(merged.astype(BF16), wout_ref[...])
    x1_ref[...] = x1
    _to_slabs(h2_ref, _rms(x1, n2_ref[...]) * (1.0 + scale2) + shift2, tm)


def _merge(rows, xp, xs, on_p, on_s, z, gt, mod3, conv_w, conv_b, ln_g, ln_b, wco_b, wap_b, wout_b, norm2_g):
    n_rows = rows.tiles * ROW_TILE
    hb = ROW_TILE // HALO
    n_halo_blocks = n_rows // HALO
    in_specs = [_row_spec(D_MODEL, rows.ctx_tile), _row_spec(D_MODEL, rows.lat_tile),
                _row_spec(ATTN_W, rows.ctx_tile), _row_spec(ATTN_W, rows.lat_tile),
                _row_spec(CONV_W),
                pl.BlockSpec((HALO, CONV_W), lambda i: (jnp.maximum(i * hb - 1, 0), 0)),
                pl.BlockSpec((HALO, CONV_W), lambda i: (jnp.minimum((i + 1) * hb, n_halo_blocks - 1), 0)),
                _row_spec(2 * D_MODEL), _mod_spec(rows),
                _const_spec((CONV_K, CONV_W)), _const_spec((1, CONV_W)), _const_spec((1, CONV_W)),
                _const_spec((1, CONV_W)), _const_spec((CONV_W, D_MODEL)), _const_spec((ATTN_W, D_MODEL)),
                _const_spec((D_MODEL, D_MODEL)), _const_spec((1, D_MODEL))]
    return pl.pallas_call(
        functools.partial(_merge_kernel, rows),
        grid=(rows.tiles,),
        in_specs=in_specs,
        out_specs=[_row_spec(D_MODEL), pl.BlockSpec((ROW_TILE * SLAB, LANES), lambda i: (i, 0))],
        out_shape=[jax.ShapeDtypeStruct((n_rows, D_MODEL), F32),
                   jax.ShapeDtypeStruct((n_rows * SLAB, LANES), F32)],
        scratch_shapes=[pltpu.VMEM((ROW_TILE + 2 * HALO, CONV_W), F32)],
        compiler_params=_cparams(("arbitrary",)),
        name="conv_merge",
    )(xp, xs, on_p, on_s, z, z, z, gt, mod3, conv_w, conv_b, ln_g, ln_b, wco_b, wap_b, wout_b, norm2_g)


def _pack_cols(cols, shape):
    lane = lax.broadcasted_iota(jnp.int32, shape, 1)
    out = jnp.zeros(shape, F32)
    for j, col in enumerate(cols):
        out = jnp.where(lane == j, col, out)
    return out


def _router_kernel(h_ref, wr_ref, rb_ref, idx_ref, gate_ref, pos_ref, cnt_ref, run_ref):
    i = pl.program_id(0)

    @pl.when(i == 0)
    def _():
        run_ref[...] = jnp.zeros_like(run_ref)

    tm = ROW_TILE
    neg = jnp.float32(-jnp.inf)
    scores = jax.nn.sigmoid(_dot(_from_slabs(h_ref, tm).astype(BF16), wr_ref[...]))
    biased = scores + rb_ref[...]
    lane_i = lax.broadcasted_iota(jnp.int32, scores.shape, 1)
    lane = lane_i.astype(F32)
    far = jnp.float32(2 * N_EXPERTS)

    def first_argmax(v):
        m = jnp.max(v, axis=-1, keepdims=True)
        return m, jnp.min(jnp.where(v == m, lane, far), axis=-1, keepdims=True)

    in_group, gscore = [], []
    for g in range(N_GROUPS):
        inb = (lane_i >= g * GROUP_SIZE) & (lane_i < (g + 1) * GROUP_SIZE)
        v = jnp.where(inb, biased, neg)
        m1, i1 = first_argmax(v)
        m2 = jnp.max(jnp.where(lane == i1, neg, v), axis=-1, keepdims=True)
        in_group.append(inb)
        gscore.append(m1 + m2)
    allowed = jnp.zeros(scores.shape, F32)
    for g in range(N_GROUPS):
        ahead = jnp.zeros((tm, 1), F32)
        for g2 in range(N_GROUPS):
            if g2 < g:
                ahead = ahead + (gscore[g2] >= gscore[g]).astype(F32)
            elif g2 > g:
                ahead = ahead + (gscore[g2] > gscore[g]).astype(F32)
        keep = (ahead < TOPK_GROUPS).astype(F32)
        allowed = jnp.where(in_group[g], keep, allowed)
    masked = jnp.where(allowed > 0.0, biased, neg)
    picked = jnp.zeros(scores.shape, F32)
    idxs, gates = [], []
    for _ in range(TOP_K):
        _, ik = first_argmax(masked)
        hit = lane == ik
        gates.append(jnp.sum(jnp.where(hit, scores, 0.0), axis=-1, keepdims=True))
        masked = jnp.where(hit, neg, masked)
        picked = jnp.where(hit, 1.0, picked)
        idxs.append(ik)
    gsum = functools.reduce(jnp.add, gates)
    gates = [g / gsum * ROUTE_SCALE for g in gates]
    r_i = lax.broadcasted_iota(jnp.int32, (tm, tm), 0)
    c_i = lax.broadcasted_iota(jnp.int32, (tm, tm), 1)
    before = (c_i < r_i).astype(BF16)
    rank = _dot(before, picked.astype(BF16)) + run_ref[...]
    poss = [jnp.sum(jnp.where(lane == ik, rank, 0.0), axis=-1, keepdims=True) for ik in idxs]
    run_ref[...] = run_ref[...] + jnp.sum(picked, axis=0, keepdims=True)
    cnt_ref[...] = run_ref[...]
    shape = idx_ref.shape
    idx_ref[...] = _pack_cols(idxs, shape).astype(jnp.int32)
    gate_ref[...] = _pack_cols(gates, shape)
    pos_ref[...] = _pack_cols(poss, shape).astype(jnp.int32)


def _router(rows, h2_slab, w_router_b, router_bias):
    n_rows = rows.tiles * ROW_TILE
    return pl.pallas_call(
        _router_kernel,
        grid=(rows.tiles,),
        in_specs=[pl.BlockSpec((ROW_TILE * SLAB, LANES), lambda i: (i, 0)),
                  _const_spec((D_MODEL, N_EXPERTS)), _const_spec((1, N_EXPERTS))],
        out_specs=[_row_spec(LANES), _row_spec(LANES), _row_spec(LANES), _const_spec((1, N_EXPERTS))],
        out_shape=[jax.ShapeDtypeStruct((n_rows, LANES), jnp.int32), jax.ShapeDtypeStruct((n_rows, LANES), F32),
                   jax.ShapeDtypeStruct((n_rows, LANES), jnp.int32), jax.ShapeDtypeStruct((1, N_EXPERTS), F32)],
        scratch_shapes=[pltpu.VMEM((1, N_EXPERTS), F32)],
        compiler_params=_cparams(("arbitrary",)),
        name="router",
    )(h2_slab, w_router_b, router_bias)


def _load_slots(i, idx_hbm, pos_hbm, ism, psm, sem_i):
    n_slots = ROW_TILE * TOP_K
    copies = [pltpu.make_async_copy(src.at[pl.ds(i * n_slots, n_slots)], dst, sem_i)
              for src, dst in ((idx_hbm, ism), (pos_hbm, psm))]
    for cp in copies:
        cp.start()
    for cp in copies:
        cp.wait()


def _dispatch_kernel(pe_ref, ps_ref, idx_hbm, pos_hbm, h_ref, xs_hbm, ism, psm, zbuf, sem_i, sem):
    i = pl.program_id(0)
    tm = ROW_TILE

    @pl.when(i == 0)
    def _():
        zbuf[...] = jnp.zeros_like(zbuf)
        n_blocks = xs_hbm.shape[0] // (MOE_BLOCK * SLAB)
        n_used = pe_ref[N_EXPERTS - 1] // MOE_BLOCK

        def clear_block(b):
            dst = xs_hbm.at[pl.ds(pl.multiple_of(b * (MOE_BLOCK * SLAB), MOE_BLOCK * SLAB), MOE_BLOCK * SLAB)]
            pltpu.make_async_copy(zbuf, dst, sem).start()

        def clear_last(e, c):
            clear_block(jnp.maximum(pe_ref[e] // MOE_BLOCK - 1, 0))
            return c

        def clear_tail(b, c):
            clear_block(b)
            return c

        def clear_wait(e, c):
            pltpu.make_async_copy(zbuf, xs_hbm.at[pl.ds(0, MOE_BLOCK * SLAB)], sem).wait()
            return c

        lax.fori_loop(0, N_EXPERTS, clear_last, 0)
        lax.fori_loop(n_used, n_blocks, clear_tail, 0)
        lax.fori_loop(0, N_EXPERTS + n_blocks - n_used, clear_wait, 0)

    _load_slots(i, idx_hbm, pos_hbm, ism, psm, sem_i)

    def issue(r, c):
        for k in range(TOP_K):
            j = r * TOP_K + k
            d = ps_ref[ism[j]] + psm[j]
            pltpu.make_async_copy(h_ref.at[_slab(r)], xs_hbm.at[_slab(d)], sem).start()
        return c

    lax.fori_loop(0, tm, issue, 0)
    for _ in range(TOP_K):
        pltpu.make_async_copy(h_ref, xs_hbm.at[pl.ds(0, tm * SLAB)], sem).wait()


def _dispatch(rows, pad_end, pad_start, idx_flat, pos_flat, h2_slab, n_rows_sorted):
    return pl.pallas_call(
        _dispatch_kernel,
        grid_spec=pltpu.PrefetchScalarGridSpec(
            num_scalar_prefetch=2,
            grid=(rows.tiles,),
            in_specs=[pl.BlockSpec(memory_space=pl.ANY), pl.BlockSpec(memory_space=pl.ANY),
                      pl.BlockSpec((ROW_TILE * SLAB, LANES), lambda i, pe, ps: (i, 0))],
            out_specs=pl.BlockSpec(memory_space=pl.ANY),
            scratch_shapes=[pltpu.SMEM((ROW_TILE * TOP_K,), jnp.int32),
                            pltpu.SMEM((ROW_TILE * TOP_K,), jnp.int32),
                            pltpu.VMEM((MOE_BLOCK * SLAB, LANES), F32),
                            pltpu.SemaphoreType.DMA, pltpu.SemaphoreType.DMA]),
        out_shape=jax.ShapeDtypeStruct((n_rows_sorted * SLAB, LANES), F32),
        compiler_params=_cparams(("arbitrary",)),
        name="moe_dispatch",
    )(pad_end, pad_start, idx_flat, pos_flat, h2_slab)


def _expert_kernel(be_ref, nu_ref, x_ref, w1_ref, w3_ref, w2_ref, y_ref, w1b, w3b, w2b):
    b = pl.program_id(0)

    @pl.when(b < nu_ref[0])
    def _():
        changed = (b == 0) | (be_ref[b] != be_ref[jnp.maximum(b - 1, 0)])

        @pl.when(changed)
        def _():
            w1b[...] = w1_ref[...].astype(BF16)
            w3b[...] = w3_ref[...].astype(BF16)
            w2b[...] = w2_ref[...].astype(BF16)

        xb = _from_slabs(x_ref, MOE_BLOCK).astype(BF16)
        hid = (_silu(_dot(xb, w1b[...])) * _dot(xb, w3b[...])).astype(BF16)
        _to_slabs(y_ref, _dot(hid, w2b[...]), MOE_BLOCK)


def _experts(block_e, n_used, xs, w1, w3, w2):
    n_blocks = xs.shape[0] // (MOE_BLOCK * SLAB)
    row_map = lambda b, be, nu: (jnp.minimum(b, nu[0] - 1), 0)
    return pl.pallas_call(
        _expert_kernel,
        grid_spec=pltpu.PrefetchScalarGridSpec(
            num_scalar_prefetch=2,
            grid=(n_blocks,),
            in_specs=[pl.BlockSpec((MOE_BLOCK * SLAB, LANES), row_map),
                      pl.BlockSpec((None, D_MODEL, EXPERT_HIDDEN), lambda b, be, nu: (be[b], 0, 0)),
                      pl.BlockSpec((None, D_MODEL, EXPERT_HIDDEN), lambda b, be, nu: (be[b], 0, 0)),
                      pl.BlockSpec((None, EXPERT_HIDDEN, D_MODEL), lambda b, be, nu: (be[b], 0, 0))],
            out_specs=pl.BlockSpec((MOE_BLOCK * SLAB, LANES), row_map),
            scratch_shapes=[pltpu.VMEM((D_MODEL, EXPERT_HIDDEN), BF16),
                            pltpu.VMEM((D_MODEL, EXPERT_HIDDEN), BF16),
                            pltpu.VMEM((EXPERT_HIDDEN, D_MODEL), BF16)]),
        out_shape=jax.ShapeDtypeStruct(xs.shape, F32),
        input_output_aliases={2: 0},
        compiler_params=_cparams(("arbitrary",)),
        name="moe_experts",
    )(block_e, n_used, xs, w1, w3, w2)


def _combine_kernel(rows, ps_ref, idx_hbm, pos_hbm, ys_hbm, x1_ref, h_ref, gate_ref, mod_ref, ws1_ref, ws3_ref,
                    ws2_ref, nf_ref, outp_ref, outs_ref, ism, psm, ybuf, sem_i, sem):
    i = pl.program_id(0)
    tm = ROW_TILE
    _load_slots(i, idx_hbm, pos_hbm, ism, psm, sem_i)

    def issue(r, c):
        for k in range(TOP_K):
            j = r * TOP_K + k
            d = ps_ref[ism[j]] + psm[j]
            pltpu.make_async_copy(ys_hbm.at[_slab(d)], ybuf.at[k, _slab(r)], sem).start()
        return c

    lax.fori_loop(0, tm, issue, 0)
    hb = _from_slabs(h_ref, tm).astype(BF16)
    shared = _dot((_silu(_dot(hb, ws1_ref[...])) * _dot(hb, ws3_ref[...])).astype(BF16), ws2_ref[...])
    for k in range(TOP_K):
        pltpu.make_async_copy(ys_hbm.at[pl.ds(0, tm * SLAB)], ybuf.at[k], sem).wait()
    g = gate_ref[...]
    routed = _from_slabs(ybuf, tm, (0,)) * g[:, 0:1]
    for k in range(1, TOP_K):
        routed = routed + _from_slabs(ybuf, tm, (k,)) * g[:, k:k + 1]
    gate2 = mod_ref[0][:, 5 * D_MODEL:6 * D_MODEL]
    x2 = x1_ref[...] + gate2 * (routed + shared)
    out = _rms(x2, nf_ref[...])

    @pl.when(i < rows.tiles_ctx)
    def _():
        outp_ref[...] = out

    @pl.when(i >= rows.tiles_ctx)
    def _():
        outs_ref[...] = out


def _combine(rows, pad_start, idx_flat, pos_flat, ys, x1, h2_slab, gates, mod3, ws1_b, ws3_b, ws2_b, normf_g):
    drop = lambda spec: pl.BlockSpec(spec.block_shape, lambda i, ps, f=spec.index_map: f(i))
    any_spec = pl.BlockSpec(memory_space=pl.ANY)
    return pl.pallas_call(
        functools.partial(_combine_kernel, rows),
        grid_spec=pltpu.PrefetchScalarGridSpec(
            num_scalar_prefetch=1,
            grid=(rows.tiles,),
            in_specs=[any_spec, any_spec, any_spec]
                     + [drop(s) for s in (
                         _row_spec(D_MODEL), pl.BlockSpec((ROW_TILE * SLAB, LANES), lambda i: (i, 0)),
                         _row_spec(LANES), _mod_spec(rows),
                         _const_spec((D_MODEL, EXPERT_HIDDEN)), _const_spec((D_MODEL, EXPERT_HIDDEN)),
                         _const_spec((EXPERT_HIDDEN, D_MODEL)), _const_spec((1, D_MODEL)))],
            out_specs=[drop(_row_spec(D_MODEL, rows.ctx_tile)), drop(_row_spec(D_MODEL, rows.lat_tile))],
            scratch_shapes=[pltpu.SMEM((ROW_TILE * TOP_K,), jnp.int32),
                            pltpu.SMEM((ROW_TILE * TOP_K,), jnp.int32),
                            pltpu.VMEM((TOP_K, ROW_TILE * SLAB, LANES), F32),
                            pltpu.SemaphoreType.DMA, pltpu.SemaphoreType.DMA]),
        out_shape=[jax.ShapeDtypeStruct((rows.tiles_ctx * ROW_TILE, D_MODEL), F32),
                   jax.ShapeDtypeStruct((rows.tiles_lat * ROW_TILE, D_MODEL), F32)],
        compiler_params=_cparams(("arbitrary",)),
        name="moe_combine",
    )(pad_start, idx_flat, pos_flat, ys, x1, h2_slab, gates, mod3, ws1_b, ws3_b, ws2_b, normf_g)


def kernel(x_prompt, x_sample, cache_k, cache_v, c, c_ctx, w_mod, b_mod, norm1_g, w_in, lambda_qk, subln_g,
           w_attn_proj, conv_w, conv_b, conv_ln_g, conv_ln_b, w_conv_out, w_out, norm2_g, w_router,
           router_bias, w1, w3, w2, ws1, ws3, ws2, normf_g):
    batch, seq, _ = x_prompt.shape
    dec_batch, dec_seq, _ = x_sample.shape
    past = cache_k.shape[2]
    l = 0
    rows_p, rows_s = batch * seq, dec_batch * dec_seq
    n_rows = rows_p + rows_s
    rows = Rows(rows_p // ROW_TILE, rows_s // ROW_TILE, seq // ROW_TILE, dec_seq // ROW_TILE)
    xp = x_prompt.reshape(rows_p, D_MODEL)
    xs = x_sample.reshape(rows_s, D_MODEL)
    row = lambda a: a.reshape(1, -1)

    cc = jnp.zeros((MOD_ROWS, D_MODEL), F32).at[0].set(c_ctx).at[1:1 + dec_batch].set(c)
    mod3 = _mod_table(cc, w_mod[l], b_mod[l]).reshape(MOD_ROWS, 1, N_MOD * D_MODEL)

    q, k, v, z, gt, state_k, state_v = _inproj(rows, xp, xs, mod3, row(norm1_g[l]), w_in[l].astype(BF16),
                                               _rope_tables(dec_seq))
    lq, sg = lambda_qk[l], row(subln_g[l])
    on_p = _attention(q, k, v, lq, sg, 0, batch, seq, CTX_HEADS_PER_STEP)
    cache = (cache_k[:, l].reshape(dec_batch, past, ATTN_W), cache_v[:, l].reshape(dec_batch, past, ATTN_W))
    on_s = _attention(q, k, v, lq, sg, rows_p, dec_batch, dec_seq, LAT_HEADS_PER_STEP, cache=cache)
    x1, h2 = _merge(rows, xp, xs, on_p, on_s, z, gt, mod3, conv_w[l], row(conv_b[l]), row(conv_ln_g[l]),
                    row(conv_ln_b[l]), w_conv_out[l].astype(BF16), w_attn_proj[l].astype(BF16),
                    w_out[l].astype(BF16), row(norm2_g[l]))

    idx, gates, pos, counts = _router(rows, h2, w_router[l].astype(BF16), row(router_bias[l]))
    counts = counts[0].astype(jnp.int32)
    padded = (counts + MOE_BLOCK - 1) // MOE_BLOCK * MOE_BLOCK
    pad_end = jnp.cumsum(padded).astype(jnp.int32)
    pad_start = pad_end - padded
    idx_flat, pos_flat = idx[:, :TOP_K].reshape(-1), pos[:, :TOP_K].reshape(-1)
    n_blocks = n_rows * TOP_K // MOE_BLOCK + N_EXPERTS
    blocks = jnp.arange(n_blocks, dtype=jnp.int32)
    block_e = jnp.minimum(jnp.sum((pad_end // MOE_BLOCK)[None, :] <= blocks[:, None], axis=-1),
                          N_EXPERTS - 1).astype(jnp.int32)
    n_used = (pad_end[-1:] // MOE_BLOCK).astype(jnp.int32)
    x_sorted = _dispatch(rows, pad_end, pad_start, idx_flat, pos_flat, h2, n_blocks * MOE_BLOCK)
    y_sorted = _experts(block_e, n_used, x_sorted, w1[l], w3[l], w2[l])
    y_p, y_s = _combine(rows, pad_start, idx_flat, pos_flat, y_sorted, x1, h2, gates, mod3,
                        ws1[l].astype(BF16), ws3[l].astype(BF16), ws2[l].astype(BF16), row(normf_g))
    return (y_p.reshape(batch, seq, D_MODEL), y_s.reshape(dec_batch, dec_seq, D_MODEL),
            state_k.reshape(batch, 1, seq, N_HEADS, 2, HEAD_DIM),
            state_v.reshape(batch, 1, seq, N_HEADS, 2 * HEAD_DIM))
```

```python
import functools
import math
from typing import NamedTuple

import jax
import jax.numpy as jnp
from jax import lax
from jax.experimental import pallas as pl
from jax.experimental.pallas import tpu as pltpu

D_MODEL = 1024
GRID_W = 64
N_HEADS = 8
HEAD_DIM = 64
ATTN_W = N_HEADS * 2 * HEAD_DIM
CONV_W = 512
CONV_K = 31
N_EXPERTS = 256
TOP_K = 8
N_GROUPS = 8
TOPK_GROUPS = 4
GROUP_SIZE = N_EXPERTS // N_GROUPS
EXPERT_HIDDEN = 256
ROUTE_SCALE = 2.5
ROPE_BASE = 10000.0
EPS = 1e-6
N_MOD = 6
IN_COLS = 3 * ATTN_W + 2 * CONV_W + 2 * D_MODEL
LAM_INIT = 0.8 - 0.6 * math.exp(-0.3 * 0)
LOG2E = math.log2(math.e)

LANES = 128
SUBLANES = 8
VMEM_LIMIT = 56 * 1024 * 1024
HALO = 16
CONV_ROWS = 64
ROW_TILE = 256
CTX_HEADS_PER_STEP = 8
LAT_HEADS_PER_STEP = 2
ATTN_KEY_CHUNK = 512
MOE_BLOCK = 256
MOD_ROWS = 8
MOD_COL_TILES = 4
PACK = D_MODEL // 2 // LANES
CHUNK = 4
PAIR = SUBLANES // PACK
STAGE = ROW_TILE * TOP_K + N_EXPERTS * (CHUNK - 1)
REC = 4 * N_EXPERTS

BF16 = jnp.bfloat16
F32 = jnp.float32


class Rows(NamedTuple):
    tiles_ctx: int
    tiles_lat: int
    tiles_per_ctx: int
    tiles_per_lat: int

    @property
    def tiles(self):
        return self.tiles_ctx + self.tiles_lat

    def ctx_tile(self, i):
        return jnp.minimum(i, self.tiles_ctx - 1)

    def lat_tile(self, i):
        return jnp.maximum(i - self.tiles_ctx, 0)

    def mod_row(self, i):
        return jnp.where(i < self.tiles_ctx, 0, 1 + self.lat_tile(i) // self.tiles_per_lat)

    def seq_tile(self, i):
        is_ctx = i < self.tiles_ctx
        per = jnp.where(is_ctx, self.tiles_per_ctx, self.tiles_per_lat)
        return jnp.where(is_ctx, i % self.tiles_per_ctx, self.lat_tile(i) % self.tiles_per_lat), per


def _cparams(sem):
    return pltpu.CompilerParams(dimension_semantics=sem, vmem_limit_bytes=VMEM_LIMIT)


def _row_spec(width, tile=lambda i: i):
    return pl.BlockSpec((ROW_TILE, width), lambda i: (tile(i), 0))


def _const_spec(shape):
    return pl.BlockSpec(shape, lambda i: (0,) * len(shape))


def _mod_spec(rows):
    return pl.BlockSpec((1, 1, N_MOD * D_MODEL), lambda i: (rows.mod_row(i), 0, 0))


def _silu(x):
    return x * jax.nn.sigmoid(x)


def _dot(a, b):
    return jnp.dot(a, b, preferred_element_type=F32)


def _dot_nt(a, b):
    return lax.dot_general(a, b, (((1,), (1,)), ((), ())), preferred_element_type=F32)


def _rms(x, g):
    return x * lax.rsqrt(jnp.mean(x * x, axis=-1, keepdims=True) + EPS) * g


def _packed_rows(token, n_tokens, align):
    return pl.ds(pl.multiple_of(token * PACK, align * PACK), n_tokens * PACK)


def _store_packed(ref, token0, x):
    half = D_MODEL // 2
    lo = pltpu.bitcast(x[:, :half], jnp.uint32) >> 16
    hi = pltpu.bitcast(x[:, half:], jnp.uint32) & jnp.uint32(0xFFFF0000)
    w = lo | hi
    for s in range(PACK):
        ref[pl.ds(token0 * PACK + s, x.shape[0], stride=PACK), :] = w[:, s * LANES:(s + 1) * LANES]


def _load_packed(ref, token0, n_tokens):
    w = jnp.concatenate([ref[pl.ds(token0 * PACK + s, n_tokens, stride=PACK), :] for s in range(PACK)], axis=1)
    lo = pltpu.bitcast(w << 16, F32)
    hi = pltpu.bitcast(w & jnp.uint32(0xFFFF0000), F32)
    return jnp.concatenate([lo, hi], axis=1).astype(BF16)


def _mod_kernel(c_ref, w_ref, b_ref, o_ref):
    a = _silu(c_ref[...]).astype(BF16)
    o_ref[...] = _dot(a, w_ref[...].astype(BF16)) + b_ref[...]


def _mod_table(cc, w_mod, b_mod):
    n = w_mod.shape[1]
    tn = n // MOD_COL_TILES
    return pl.pallas_call(
        _mod_kernel,
        grid=(MOD_COL_TILES,),
        in_specs=[pl.BlockSpec((MOD_ROWS, D_MODEL), lambda j: (0, 0)),
                  pl.BlockSpec((D_MODEL, tn), lambda j: (0, j)),
                  pl.BlockSpec((1, tn), lambda j: (0, j))],
        out_specs=pl.BlockSpec((MOD_ROWS, tn), lambda j: (0, j)),
        out_shape=jax.ShapeDtypeStruct((MOD_ROWS, n), F32),
        compiler_params=_cparams(("arbitrary",)),
        name="mod_table",
    )(cc, w_mod, b_mod.reshape(1, n))


def _rope(x, cos, sin, lane_lo):
    swapped = jnp.where(lane_lo, pltpu.roll(x, LANES - 16, axis=1), pltpu.roll(x, 16, axis=1))
    return x * cos + swapped * sin


def _inproj_kernel(rows, xp_ref, xs_ref, mod_ref, g_ref, w_ref, cos_ref, sin_ref,
                   q_ref, k_ref, v_ref, z_ref, gt_ref, ks_ref, vs_ref):
    is_ctx = pl.program_id(0) < rows.tiles_ctx
    mod = mod_ref[0]
    shift, scale = mod[:, 0:D_MODEL], mod[:, D_MODEL:2 * D_MODEL]
    x = jnp.where(is_ctx, xp_ref[...], xs_ref[...])
    h = _rms(x, g_ref[...]) * (1.0 + scale) + shift
    hb = h.astype(BF16)
    q = _dot(hb, w_ref[:, 0:ATTN_W]) * (HEAD_DIM ** -0.5 * LOG2E)
    k = _dot(hb, w_ref[:, ATTN_W:2 * ATTN_W])
    v = _dot(hb, w_ref[:, 2 * ATTN_W:3 * ATTN_W])

    @pl.when(is_ctx)
    def _():
        q_ref[...] = q.astype(BF16)
        k_ref[...] = k.astype(BF16)
        ks_ref[...] = k
        vs_ref[...] = v

    @pl.when(jnp.logical_not(is_ctx))
    def _():
        cos, sin = cos_ref[...], sin_ref[...]
        lane_lo = (lax.broadcasted_iota(jnp.int32, cos.shape, 1) % 32) < 16
        for j in range(ATTN_W // LANES):
            sl = slice(j * LANES, (j + 1) * LANES)
            q_ref[:, sl] = _rope(q[:, sl], cos, sin, lane_lo).astype(BF16)
            k_ref[:, sl] = _rope(k[:, sl], cos, sin, lane_lo).astype(BF16)

    v_ref[...] = v.astype(BF16)
    u = _dot(hb, w_ref[:, 3 * ATTN_W:3 * ATTN_W + 2 * CONV_W])
    z_ref[...] = u[:, :CONV_W] * jax.nn.sigmoid(u[:, CONV_W:])
    gt_ref[...] = jax.nn.sigmoid(_dot(hb, w_ref[:, 3 * ATTN_W + 2 * CONV_W:]))


def _inproj(rows, xp, xs, mod3, norm_g, w_in_b, rope):
    n_rows = rows.tiles * ROW_TILE
    rows_p = rows.tiles_ctx * ROW_TILE
    rope_spec = pl.BlockSpec((ROW_TILE, LANES), lambda i: (rows.lat_tile(i) % rows.tiles_per_lat, 0))
    shapes = lambda w, dt, r=n_rows: jax.ShapeDtypeStruct((r, w), dt)
    return pl.pallas_call(
        functools.partial(_inproj_kernel, rows),
        grid=(rows.tiles,),
        in_specs=[_row_spec(D_MODEL, rows.ctx_tile), _row_spec(D_MODEL, rows.lat_tile), _mod_spec(rows),
                  _const_spec((1, D_MODEL)),
                  pl.BlockSpec((D_MODEL, IN_COLS), lambda i: (0, 0), pipeline_mode=pl.Buffered(1)),
                  rope_spec, rope_spec],
        out_specs=[_row_spec(ATTN_W)] * 3 + [_row_spec(CONV_W), _row_spec(2 * D_MODEL)]
                  + [_row_spec(ATTN_W, rows.ctx_tile)] * 2,
        out_shape=[shapes(ATTN_W, BF16)] * 3 + [shapes(CONV_W, F32), shapes(2 * D_MODEL, F32)]
                  + [shapes(ATTN_W, F32, rows_p)] * 2,
        compiler_params=_cparams(("arbitrary",)),
        name="inproj",
    )(xp, xs, mod3, norm_g, w_in_b, *rope)


def _rope_tables(n_tokens):
    t = jnp.arange(n_tokens, dtype=jnp.int32)
    pos = jnp.stack([t // GRID_W, t % GRID_W], axis=-1).astype(F32)
    half = HEAD_DIM // 2
    inv = ROPE_BASE ** (-jnp.arange(0, half, 2, dtype=F32) / half)
    ang = pos[:, :, None] * inv
    cos, sin = jnp.cos(ang), jnp.sin(ang)
    cos64 = jnp.concatenate([cos[:, 0], cos[:, 0], cos[:, 1], cos[:, 1]], axis=-1)
    sin64 = jnp.concatenate([-sin[:, 0], sin[:, 0], -sin[:, 1], sin[:, 1]], axis=-1)
    return jnp.tile(cos64, (1, LANES // HEAD_DIM)), jnp.tile(sin64, (1, LANES // HEAD_DIM))


def _lane_groups(x):
    return [x[:, j * LANES:(j + 1) * LANES] for j in range(x.shape[1] // LANES)]


def _attn_kernel(has_cache, heads, lq_ref, sg_ref, q_ref, k_ref, v_ref, *rest):
    if has_cache:
        ck_ref, cv_ref, o_ref, *bufs = rest
    else:
        o_ref, *bufs = rest

    def s_ref(u, mp):
        return bufs[(u % 2) * 2 + mp]

    lq = lq_ref[...]
    lam = (jnp.exp(jnp.sum(lq[0:1] * lq[1:2], axis=-1, keepdims=True))
           - jnp.exp(jnp.sum(lq[2:3] * lq[3:4], axis=-1, keepdims=True)) + LAM_INIT)
    tq, seq = q_ref.shape[0], k_ref.shape[0]
    chunks = [(off, min(ATTN_KEY_CHUNK, seq - off), False) for off in range(0, seq, ATTN_KEY_CHUNK)]
    if has_cache:
        chunks.append((seq, ck_ref.shape[0], True))
    first = lax.broadcasted_iota(jnp.int32, (tq, LANES), 1) < HEAD_DIM
    neg = jnp.full((tq, LANES), -jnp.inf, F32)
    heads_state = [dict() for _ in range(heads)]

    def lanes(u):
        return slice(u * LANES, (u + 1) * LANES)

    def load(main_ref, cache_ref, u, chunk):
        off, size, cached = chunk
        if cached:
            return cache_ref[:, lanes(u)].astype(BF16)
        return main_ref[off:off + size, lanes(u)]

    def scores(u, chunk):
        st = heads_state[u]
        if "q" not in st:
            q = q_ref[:, lanes(u)]
            zero = jnp.zeros_like(q)
            st["q"] = (jnp.where(first, q, zero), jnp.where(first, zero, q))
            st["macc"] = [neg, neg]
        kk = load(k_ref, ck_ref if has_cache else None, u, chunk)
        off, size, _ = chunk
        for mp in range(2):
            s = _dot_nt(st["q"][mp], kk)
            s_ref(u, mp)[:, off:off + size] = s
            st["macc"][mp] = functools.reduce(jnp.maximum, _lane_groups(s), st["macc"][mp])

    def exps(u, chunk):
        st = heads_state[u]
        if "m" not in st:
            st["m"] = [jnp.max(a, axis=-1, keepdims=True) for a in st["macc"]]
            st["lacc"] = [jnp.zeros((tq, LANES), F32)] * 2
        off, size, _ = chunk
        for mp in range(2):
            e = jnp.exp2(s_ref(u, mp)[:, off:off + size] - st["m"][mp])
            s_ref(u, mp)[:, off:off + size] = e
            st["lacc"][mp] = functools.reduce(jnp.add, _lane_groups(e), st["lacc"][mp])

    def values(u, chunk):
        st = heads_state[u]
        if "r" not in st:
            l1, l2 = [jnp.sum(a, axis=-1, keepdims=True) for a in st["lacc"]]
            st["r"] = (1.0 / l1, lam / l2)
            st["o"] = jnp.zeros((tq, LANES), F32)
        off, size, _ = chunk
        w = s_ref(u, 0)[:, off:off + size] * st["r"][0] - s_ref(u, 1)[:, off:off + size] * st["r"][1]
        st["o"] = st["o"] + _dot(w.astype(BF16), load(v_ref, cv_ref if has_cache else None, u, chunk))

    stages = (scores, exps, values)
    for phase in range(heads + len(stages) - 1):
        for chunk in chunks:
            for s in reversed(range(len(stages))):
                u = phase - s
                if 0 <= u < heads:
                    stages[s](u, chunk)
        u = phase - (len(stages) - 1)
        if 0 <= u < heads:
            o_ref[:, lanes(u)] = (_rms(heads_state[u]["o"], sg_ref[...]) * (1.0 - LAM_INIT)).astype(BF16)


def _attention(q, k, v, lambda_qk, subln_g, row0, batch, seq_len, heads, cache=None):
    tq = ROW_TILE
    q_tiles = seq_len // tq
    assert row0 % seq_len == 0
    seq0 = row0 // seq_len
    width = heads * LANES
    head_q = pl.BlockSpec((tq, width), lambda b, h, i: (seq0 * q_tiles + b * q_tiles + i, h))
    head_kv = pl.BlockSpec((seq_len, width), lambda b, h, i: (seq0 + b, h))
    in_specs = [pl.BlockSpec((4, HEAD_DIM), lambda b, h, i: (0, 0)),
                pl.BlockSpec((1, 2 * HEAD_DIM), lambda b, h, i: (0, 0)),
                head_q, head_kv, head_kv]
    args = [lambda_qk, subln_g, q, k, v]
    n_keys = seq_len
    if cache is not None:
        past = cache[0].shape[1]
        n_keys += past
        head_cache = pl.BlockSpec((None, past, width), lambda b, h, i: (b, 0, h))
        in_specs += [head_cache, head_cache]
        args += list(cache)
    return pl.pallas_call(
        functools.partial(_attn_kernel, cache is not None, heads),
        grid=(batch, N_HEADS // heads, q_tiles),
        in_specs=in_specs,
        out_specs=pl.BlockSpec((tq, width), lambda b, h, i: (b * q_tiles + i, h)),
        out_shape=jax.ShapeDtypeStruct((batch * seq_len, ATTN_W), BF16),
        scratch_shapes=[pltpu.VMEM((tq, n_keys), F32)] * 4,
        compiler_params=_cparams(("parallel", "parallel", "arbitrary")),
        name="attn_latent" if cache is not None else "attn_ctx",
    )(*args)


def _merge_kernel(rows, xp_ref, xs_ref, onp_ref, ons_ref, z_ref, zp_ref, zn_ref, gt_ref, mod_ref, cw_ref, cb_ref,
                  lg_ref, lb_ref, wco_ref, wap_ref, wout_ref, n2_ref, x1_ref, h2_ref, zext_ref):
    i = pl.program_id(0)
    is_ctx = i < rows.tiles_ctx
    tm = ROW_TILE
    t, per = rows.seq_tile(i)
    zext_ref[HALO:HALO + tm, :] = z_ref[...]
    zext_ref[0:HALO, :] = jnp.where(t == 0, 0.0, zp_ref[...])
    zext_ref[HALO + tm:, :] = jnp.where(t == per - 1, 0.0, zn_ref[...])
    pad = HALO - CONV_K // 2
    col_blocks = []
    for cb in range(CONV_W // LANES):
        cs = slice(cb * LANES, (cb + 1) * LANES)
        row_chunks = []
        for r0 in range(0, tm, CONV_ROWS):
            acc = jnp.zeros((CONV_ROWS, LANES), F32) + cb_ref[:, cs]
            for j in range(CONV_K):
                acc = acc + zext_ref[r0 + j + pad:r0 + j + pad + CONV_ROWS, cs] * cw_ref[j:j + 1, cs]
            row_chunks.append(acc)
        col_blocks.append(jnp.concatenate(row_chunks, axis=0))
    c = jnp.concatenate(col_blocks, axis=1)
    mu = jnp.mean(c, axis=-1, keepdims=True)
    cc = c - mu
    y = cc * lax.rsqrt(jnp.mean(cc * cc, axis=-1, keepdims=True) + EPS) * lg_ref[...] + lb_ref[...]
    conv_out = _dot(_silu(y).astype(BF16), wco_ref[...])
    o_n = jnp.where(is_ctx, onp_ref[...], ons_ref[...])
    a_br = _dot(o_n, wap_ref[...])
    gt = gt_ref[...]
    merged = gt[:, :D_MODEL] * a_br + gt[:, D_MODEL:] * conv_out
    mod = mod_ref[0]
    gate1 = mod[:, 2 * D_MODEL:3 * D_MODEL]
    shift2, scale2 = mod[:, 3 * D_MODEL:4 * D_MODEL], mod[:, 4 * D_MODEL:5 * D_MODEL]
    x = jnp.where(is_ctx, xp_ref[...], xs_ref[...])
    x1 = x + gate1 * _dot(merged.astype(BF16), wout_ref[...])
    x1_ref[...] = x1
    h2_ref[...] = (_rms(x1, n2_ref[...]) * (1.0 + scale2) + shift2).astype(BF16)


def _merge(rows, xp, xs, on_p, on_s, z, gt, mod3, conv_w, conv_b, ln_g, ln_b, wco_b, wap_b, wout_b, norm2_g):
    n_rows = rows.tiles * ROW_TILE
    hb = ROW_TILE // HALO
    n_halo_blocks = n_rows // HALO
    in_specs = [_row_spec(D_MODEL, rows.ctx_tile), _row_spec(D_MODEL, rows.lat_tile),
                _row_spec(ATTN_W, rows.ctx_tile), _row_spec(ATTN_W, rows.lat_tile),
                _row_spec(CONV_W),
                pl.BlockSpec((HALO, CONV_W), lambda i: (jnp.maximum(i * hb - 1, 0), 0)),
                pl.BlockSpec((HALO, CONV_W), lambda i: (jnp.minimum((i + 1) * hb, n_halo_blocks - 1), 0)),
                _row_spec(2 * D_MODEL), _mod_spec(rows),
                _const_spec((CONV_K, CONV_W)), _const_spec((1, CONV_W)), _const_spec((1, CONV_W)),
                _const_spec((1, CONV_W)), _const_spec((CONV_W, D_MODEL)), _const_spec((ATTN_W, D_MODEL)),
                _const_spec((D_MODEL, D_MODEL)), _const_spec((1, D_MODEL))]
    return pl.pallas_call(
        functools.partial(_merge_kernel, rows),
        grid=(rows.tiles,),
        in_specs=in_specs,
        out_specs=[_row_spec(D_MODEL), _row_spec(D_MODEL)],
        out_shape=[jax.ShapeDtypeStruct((n_rows, D_MODEL), F32), jax.ShapeDtypeStruct((n_rows, D_MODEL), BF16)],
        scratch_shapes=[pltpu.VMEM((ROW_TILE + 2 * HALO, CONV_W), F32)],
        compiler_params=_cparams(("arbitrary",)),
        name="conv_merge",
    )(xp, xs, on_p, on_s, z, z, z, gt, mod3, conv_w, conv_b, ln_g, ln_b, wco_b, wap_b, wout_b, norm2_g)


def _pack_cols(cols, shape):
    lane = lax.broadcasted_iota(jnp.int32, shape, 1)
    out = jnp.zeros(shape, F32)
    for j, col in enumerate(cols):
        out = jnp.where(lane == j, col, out)
    return out


def _router_kernel(h_ref, wr_ref, rb_ref, gate_ref, loc_ref, rec_ref, cnt_ref, run_ref):
    i = pl.program_id(0)

    @pl.when(i == 0)
    def _():
        run_ref[...] = jnp.zeros_like(run_ref)

    tm = ROW_TILE
    neg = jnp.float32(-jnp.inf)
    scores = jax.nn.sigmoid(_dot(h_ref[...], wr_ref[...]))
    biased = scores + rb_ref[...]
    lane_i = lax.broadcasted_iota(jnp.int32, scores.shape, 1)
    lane = lane_i.astype(F32)
    far = jnp.float32(2 * N_EXPERTS)

    def first_argmax(v):
        m = jnp.max(v, axis=-1, keepdims=True)
        return m, jnp.min(jnp.where(v == m, lane, far), axis=-1, keepdims=True)

    in_group, gscore = [], []
    for g in range(N_GROUPS):
        inb = (lane_i >= g * GROUP_SIZE) & (lane_i < (g + 1) * GROUP_SIZE)
        v = jnp.where(inb, biased, neg)
        m1, i1 = first_argmax(v)
        m2 = jnp.max(jnp.where(lane == i1, neg, v), axis=-1, keepdims=True)
        in_group.append(inb)
        gscore.append(m1 + m2)
    allowed = jnp.zeros(scores.shape, F32)
    for g in range(N_GROUPS):
        ahead = jnp.zeros((tm, 1), F32)
        for g2 in range(N_GROUPS):
            if g2 < g:
                ahead = ahead + (gscore[g2] >= gscore[g]).astype(F32)
            elif g2 > g:
                ahead = ahead + (gscore[g2] > gscore[g]).astype(F32)
        keep = (ahead < TOPK_GROUPS).astype(F32)
        allowed = jnp.where(in_group[g], keep, allowed)
    masked = jnp.where(allowed > 0.0, biased, neg)
    picked = jnp.zeros(scores.shape, F32)
    idxs, gates = [], []
    for _ in range(TOP_K):
        _, ik = first_argmax(masked)
        hit = lane == ik
        gates.append(jnp.sum(jnp.where(hit, scores, 0.0), axis=-1, keepdims=True))
        masked = jnp.where(hit, neg, masked)
        picked = jnp.where(hit, 1.0, picked)
        idxs.append(ik)
    gsum = functools.reduce(jnp.add, gates)
    gates = [g / gsum * ROUTE_SCALE for g in gates]
    r_i = lax.broadcasted_iota(jnp.int32, (tm, tm), 0)
    c_i = lax.broadcasted_iota(jnp.int32, (tm, tm), 1)
    before = (c_i < r_i).astype(BF16)
    local_rank = _dot(before, picked.astype(BF16))
    n_tok = jnp.sum(picked, axis=0, keepdims=True)
    n_chunks = jnp.floor((n_tok + (CHUNK - 1)) * (1.0 / CHUNK))
    lower = (lax.broadcasted_iota(jnp.int32, (N_EXPERTS, N_EXPERTS), 0)
             < lax.broadcasted_iota(jnp.int32, (N_EXPERTS, N_EXPERTS), 1)).astype(BF16)
    chunks_before = _dot(jnp.broadcast_to(n_chunks, (SUBLANES, N_EXPERTS)).astype(BF16), lower)[0:1]
    staged_off = chunks_before * CHUNK
    staged = local_rank + staged_off
    locs = [jnp.sum(jnp.where(lane == ik, staged, 0.0), axis=-1, keepdims=True) for ik in idxs]
    rec_ref[0] = jnp.concatenate([n_tok, staged_off, run_ref[...], jnp.zeros_like(n_tok)],
                                 axis=1).astype(jnp.int32)
    run_ref[...] = run_ref[...] + jnp.floor((n_tok + (PAIR - 1)) * (1.0 / PAIR)) * PAIR
    cnt_ref[...] = run_ref[...]
    shape = gate_ref.shape
    gate_ref[...] = _pack_cols(gates, shape)
    loc_ref[...] = _pack_cols(locs, shape).astype(jnp.int32)


def _router(rows, h2, w_router_b, router_bias):
    n_rows = rows.tiles * ROW_TILE
    return pl.pallas_call(
        _router_kernel,
        grid=(rows.tiles,),
        in_specs=[_row_spec(D_MODEL), _const_spec((D_MODEL, N_EXPERTS)), _const_spec((1, N_EXPERTS))],
        out_specs=[_row_spec(LANES), _row_spec(LANES), pl.BlockSpec((1, 1, REC), lambda i: (i, 0, 0)),
                   _const_spec((1, N_EXPERTS))],
        out_shape=[jax.ShapeDtypeStruct((n_rows, LANES), F32), jax.ShapeDtypeStruct((n_rows, LANES), jnp.int32),
                   jax.ShapeDtypeStruct((rows.tiles, 1, REC), jnp.int32),
                   jax.ShapeDtypeStruct((1, N_EXPERTS), F32)],
        scratch_shapes=[pltpu.VMEM((1, N_EXPERTS), F32)],
        compiler_params=_cparams(("arbitrary",)),
        name="router",
    )(h2, w_router_b, router_bias)


def _load_record(i, rec_hbm, rsm, sem_i):
    cp = pltpu.make_async_copy(rec_hbm.at[pl.ds(i * REC, REC)], rsm, sem_i)
    cp.start()
    cp.wait()


def _move_segments(ps_ref, rsm, staged_ref, sorted_hbm, sem, to_sorted):
    def segment(e, total):
        n_chunks = (rsm[e] + (CHUNK - 1)) // CHUNK
        src0 = rsm[N_EXPERTS + e]
        dst0 = ps_ref[e] + rsm[2 * N_EXPERTS + e]

        def chunk(c, carry):
            staged = staged_ref.at[_packed_rows(src0 + c * CHUNK, CHUNK, CHUNK)]
            placed = sorted_hbm.at[_packed_rows(dst0 + c * CHUNK, CHUNK, PAIR)]
            src, dst = (staged, placed) if to_sorted else (placed, staged)
            pltpu.make_async_copy(src, dst, sem).start()
            return carry

        lax.fori_loop(0, n_chunks, chunk, 0)
        return total + n_chunks

    return lax.fori_loop(0, N_EXPERTS, segment, 0)


def _wait_chunks(n, staged_ref, sorted_hbm, sem):
    def wait(j, carry):
        pltpu.make_async_copy(staged_ref.at[pl.ds(0, CHUNK * PACK)], sorted_hbm.at[pl.ds(0, CHUNK * PACK)], sem).wait()
        return carry

    lax.fori_loop(0, n, wait, 0)


def _staged_block(loc, rb):
    col = lax.broadcasted_iota(jnp.int32, (loc.shape[0], MOE_BLOCK), 1) + rb * MOE_BLOCK
    return [loc[:, k:k + 1] == col for k in range(TOP_K)]


def _dispatch_kernel(pe_ref, ps_ref, c2_ref, rec_hbm, loc_ref, h_ref, xs_hbm, rsm, stage, zbuf, sem_i, sem):
    i = pl.program_id(0)
    block_rows = MOE_BLOCK * PACK

    @pl.when(i == 0)
    def _():
        zbuf[...] = jnp.zeros_like(zbuf)
        n_blocks = xs_hbm.shape[0] // block_rows
        n_used = pe_ref[N_EXPERTS - 1] // MOE_BLOCK

        def clear_block(b, carry):
            dst = xs_hbm.at[pl.ds(pl.multiple_of(b * block_rows, block_rows), block_rows)]
            pltpu.make_async_copy(zbuf, dst, sem).start()
            return carry

        def clear_padding(e, total):
            first = (ps_ref[e] + c2_ref[e]) // MOE_BLOCK
            last = pe_ref[e] // MOE_BLOCK
            lax.fori_loop(first, last, clear_block, 0)
            return total + last - first

        def clear_wait(e, carry):
            pltpu.make_async_copy(zbuf, xs_hbm.at[pl.ds(0, block_rows)], sem).wait()
            return carry

        n_cleared = lax.fori_loop(0, N_EXPERTS, clear_padding, 0)
        lax.fori_loop(n_used, n_blocks, clear_block, 0)
        lax.fori_loop(0, n_cleared + n_blocks - n_used, clear_wait, 0)

    _load_record(i, rec_hbm, rsm, sem_i)
    loc, hb = loc_ref[...], h_ref[...]
    for rb in range(STAGE // MOE_BLOCK):
        onehot = jnp.zeros((ROW_TILE, MOE_BLOCK), F32)
        for hit in _staged_block(loc, rb):
            onehot = jnp.where(hit, 1.0, onehot)
        rows = lax.dot_general(onehot.astype(BF16), hb, (((0,), (0,)), ((), ())), preferred_element_type=F32)
        _store_packed(stage, rb * MOE_BLOCK, rows)
    n = _move_segments(ps_ref, rsm, stage, xs_hbm, sem, to_sorted=True)
    _wait_chunks(n, stage, xs_hbm, sem)


def _dispatch(rows, pad_end, pad_start, counts, rec_flat, locs, h2, n_rows_sorted):
    return pl.pallas_call(
        _dispatch_kernel,
        grid_spec=pltpu.PrefetchScalarGridSpec(
            num_scalar_prefetch=3,
            grid=(rows.tiles,),
            in_specs=[pl.BlockSpec(memory_space=pl.ANY),
                      pl.BlockSpec((ROW_TILE, LANES), lambda i, pe, ps, c2: (i, 0)),
                      pl.BlockSpec((ROW_TILE, D_MODEL), lambda i, pe, ps, c2: (i, 0))],
            out_specs=pl.BlockSpec(memory_space=pl.ANY),
            scratch_shapes=[pltpu.SMEM((REC,), jnp.int32),
                            pltpu.VMEM((STAGE * PACK, LANES), jnp.uint32),
                            pltpu.VMEM((MOE_BLOCK * PACK, LANES), jnp.uint32),
                            pltpu.SemaphoreType.DMA, pltpu.SemaphoreType.DMA]),
        out_shape=jax.ShapeDtypeStruct((n_rows_sorted * PACK, LANES), jnp.uint32),
        compiler_params=_cparams(("arbitrary",)),
        name="moe_dispatch",
    )(pad_end, pad_start, counts, rec_flat, locs, h2)


def _expert_kernel(be_ref, nu_ref, x_ref, w1_ref, w3_ref, w2_ref, y_ref, w1b, w3b, w2b):
    b = pl.program_id(0)

    @pl.when(b < nu_ref[0])
    def _():
        changed = (b == 0) | (be_ref[b] != be_ref[jnp.maximum(b - 1, 0)])

        @pl.when(changed)
        def _():
            w1b[...] = w1_ref[...].astype(BF16)
            w3b[...] = w3_ref[...].astype(BF16)
            w2b[...] = w2_ref[...].astype(BF16)

        xb = _load_packed(x_ref, 0, MOE_BLOCK)
        hid = (_silu(_dot(xb, w1b[...])) * _dot(xb, w3b[...])).astype(BF16)
        _store_packed(y_ref, 0, _dot(hid, w2b[...]).astype(BF16).astype(F32))


def _experts(block_e, n_used, xs, w1, w3, w2):
    n_blocks = xs.shape[0] // (MOE_BLOCK * PACK)
    row_map = lambda b, be, nu: (jnp.minimum(b, nu[0] - 1), 0)
    return pl.pallas_call(
        _expert_kernel,
        grid_spec=pltpu.PrefetchScalarGridSpec(
            num_scalar_prefetch=2,
            grid=(n_blocks,),
            in_specs=[pl.BlockSpec((MOE_BLOCK * PACK, LANES), row_map),
                      pl.BlockSpec((None, D_MODEL, EXPERT_HIDDEN), lambda b, be, nu: (be[b], 0, 0)),
                      pl.BlockSpec((None, D_MODEL, EXPERT_HIDDEN), lambda b, be, nu: (be[b], 0, 0)),
                      pl.BlockSpec((None, EXPERT_HIDDEN, D_MODEL), lambda b, be, nu: (be[b], 0, 0))],
            out_specs=pl.BlockSpec((MOE_BLOCK * PACK, LANES), row_map),
            scratch_shapes=[pltpu.VMEM((D_MODEL, EXPERT_HIDDEN), BF16),
                            pltpu.VMEM((D_MODEL, EXPERT_HIDDEN), BF16),
                            pltpu.VMEM((EXPERT_HIDDEN, D_MODEL), BF16)]),
        out_shape=jax.ShapeDtypeStruct(xs.shape, jnp.uint32),
        input_output_aliases={2: 0},
        compiler_params=_cparams(("arbitrary",)),
        name="moe_experts",
    )(block_e, n_used, xs, w1, w3, w2)


def _combine_kernel(rows, ps_ref, rec_hbm, ys_hbm, loc_ref, x1_ref, h_ref, gate_ref, mod_ref, ws1_ref, ws3_ref,
                    ws2_ref, nf_ref, outp_ref, outs_ref, rsm, ybuf, sem_i, sem):
    i = pl.program_id(0)

    @pl.when(i == 0)
    def _():
        ybuf[...] = jnp.zeros_like(ybuf)

    _load_record(i, rec_hbm, rsm, sem_i)
    n = _move_segments(ps_ref, rsm, ybuf, ys_hbm, sem, to_sorted=False)
    hb = h_ref[...]
    shared = _dot((_silu(_dot(hb, ws1_ref[...])) * _dot(hb, ws3_ref[...])).astype(BF16), ws2_ref[...])
    _wait_chunks(n, ybuf, ys_hbm, sem)
    loc, g = loc_ref[...], gate_ref[...]
    routed = jnp.zeros((ROW_TILE, D_MODEL), F32)
    for rb in range(STAGE // MOE_BLOCK):
        gm = jnp.zeros((ROW_TILE, MOE_BLOCK), F32)
        for k, hit in enumerate(_staged_block(loc, rb)):
            gm = jnp.where(hit, g[:, k:k + 1], gm)
        g_hi = gm.astype(BF16)
        g_lo = (gm - g_hi.astype(F32)).astype(BF16)
        yb = _load_packed(ybuf, rb * MOE_BLOCK, MOE_BLOCK)
        routed = routed + _dot(g_hi, yb) + _dot(g_lo, yb)
    gate2 = mod_ref[0][:, 5 * D_MODEL:6 * D_MODEL]
    x2 = x1_ref[...] + gate2 * (routed + shared)
    out = _rms(x2, nf_ref[...])

    @pl.when(i < rows.tiles_ctx)
    def _():
        outp_ref[...] = out

    @pl.when(i >= rows.tiles_ctx)
    def _():
        outs_ref[...] = out


def _combine(rows, pad_start, rec_flat, ys, locs, x1, h2, gates, mod3, ws1_b, ws3_b, ws2_b, normf_g):
    drop = lambda spec: pl.BlockSpec(spec.block_shape, lambda i, ps, f=spec.index_map: f(i))
    any_spec = pl.BlockSpec(memory_space=pl.ANY)
    return pl.pallas_call(
        functools.partial(_combine_kernel, rows),
        grid_spec=pltpu.PrefetchScalarGridSpec(
            num_scalar_prefetch=1,
            grid=(rows.tiles,),
            in_specs=[any_spec, any_spec]
                     + [drop(s) for s in (
                         _row_spec(LANES), _row_spec(D_MODEL), _row_spec(D_MODEL), _row_spec(LANES),
                         _mod_spec(rows),
                         _const_spec((D_MODEL, EXPERT_HIDDEN)), _const_spec((D_MODEL, EXPERT_HIDDEN)),
                         _const_spec((EXPERT_HIDDEN, D_MODEL)), _const_spec((1, D_MODEL)))],
            out_specs=[drop(_row_spec(D_MODEL, rows.ctx_tile)), drop(_row_spec(D_MODEL, rows.lat_tile))],
            scratch_shapes=[pltpu.SMEM((REC,), jnp.int32),
                            pltpu.VMEM((STAGE * PACK, LANES), jnp.uint32),
                            pltpu.SemaphoreType.DMA, pltpu.SemaphoreType.DMA]),
        out_shape=[jax.ShapeDtypeStruct((rows.tiles_ctx * ROW_TILE, D_MODEL), F32),
                   jax.ShapeDtypeStruct((rows.tiles_lat * ROW_TILE, D_MODEL), F32)],
        compiler_params=_cparams(("arbitrary",)),
        name="moe_combine",
    )(pad_start, rec_flat, ys, locs, x1, h2, gates, mod3, ws1_b, ws3_b, ws2_b, normf_g)


def kernel(x_prompt, x_sample, cache_k, cache_v, c, c_ctx, w_mod, b_mod, norm1_g, w_in, lambda_qk, subln_g,
           w_attn_proj, conv_w, conv_b, conv_ln_g, conv_ln_b, w_conv_out, w_out, norm2_g, w_router,
           router_bias, w1, w3, w2, ws1, ws3, ws2, normf_g):
    batch, seq, _ = x_prompt.shape
    dec_batch, dec_seq, _ = x_sample.shape
    past = cache_k.shape[2]
    l = 0
    rows_p, rows_s = batch * seq, dec_batch * dec_seq
    n_rows = rows_p + rows_s
    rows = Rows(rows_p // ROW_TILE, rows_s // ROW_TILE, seq // ROW_TILE, dec_seq // ROW_TILE)
    xp = x_prompt.reshape(rows_p, D_MODEL)
    xs = x_sample.reshape(rows_s, D_MODEL)
    row = lambda a: a.reshape(1, -1)

    cc = jnp.zeros((MOD_ROWS, D_MODEL), F32).at[0].set(c_ctx).at[1:1 + dec_batch].set(c)
    mod3 = _mod_table(cc, w_mod[l], b_mod[l]).reshape(MOD_ROWS, 1, N_MOD * D_MODEL)

    q, k, v, z, gt, state_k, state_v = _inproj(rows, xp, xs, mod3, row(norm1_g[l]), w_in[l].astype(BF16),
                                               _rope_tables(dec_seq))
    lq, sg = lambda_qk[l], row(subln_g[l])
    on_p = _attention(q, k, v, lq, sg, 0, batch, seq, CTX_HEADS_PER_STEP)
    cache = (cache_k[:, l].reshape(dec_batch, past, ATTN_W), cache_v[:, l].reshape(dec_batch, past, ATTN_W))
    on_s = _attention(q, k, v, lq, sg, rows_p, dec_batch, dec_seq, LAT_HEADS_PER_STEP, cache=cache)
    x1, h2 = _merge(rows, xp, xs, on_p, on_s, z, gt, mod3, conv_w[l], row(conv_b[l]), row(conv_ln_g[l]),
                    row(conv_ln_b[l]), w_conv_out[l].astype(BF16), w_attn_proj[l].astype(BF16),
                    w_out[l].astype(BF16), row(norm2_g[l]))

    gates, locs, rec, counts = _router(rows, h2, w_router[l].astype(BF16), row(router_bias[l]))
    counts = counts[0].astype(jnp.int32)
    padded = (counts + CHUNK + MOE_BLOCK - 1) // MOE_BLOCK * MOE_BLOCK
    pad_end = jnp.cumsum(padded).astype(jnp.int32)
    pad_start = pad_end - padded
    max_rows = n_rows * TOP_K + rows.tiles * N_EXPERTS * (PAIR - 1) + N_EXPERTS * (CHUNK + MOE_BLOCK - 1)
    n_blocks = -(-max_rows // MOE_BLOCK)
    blocks = jnp.arange(n_blocks, dtype=jnp.int32)
    block_e = jnp.minimum(jnp.sum((pad_end // MOE_BLOCK)[None, :] <= blocks[:, None], axis=-1),
                          N_EXPERTS - 1).astype(jnp.int32)
    n_used = (pad_end[-1:] // MOE_BLOCK).astype(jnp.int32)
    rec_flat = rec.reshape(-1)
    x_sorted = _dispatch(rows, pad_end, pad_start, counts, rec_flat, locs, h2, n_blocks * MOE_BLOCK)
    y_sorted = _experts(block_e, n_used, x_sorted, w1[l], w3[l], w2[l])
    y_p, y_s = _combine(rows, pad_start, rec_flat, y_sorted, locs, x1, h2, gates, mod3,
                        ws1[l].astype(BF16), ws3[l].astype(BF16), ws2[l].astype(BF16), row(normf_g))
    return (y_p.reshape(batch, seq, D_MODEL), y_s.reshape(dec_batch, dec_seq, D_MODEL),
            state_k.reshape(batch, 1, seq, N_HEADS, 2, HEAD_DIM),
            state_v.reshape(batch, 1, seq, N_HEADS, 2 * HEAD_DIM))
```

```python
import functools
import math
from typing import NamedTuple

import jax
import jax.numpy as jnp
from jax import lax
from jax.experimental import pallas as pl
from jax.experimental.pallas import tpu as pltpu

D_MODEL = 1024
GRID_W = 64
N_HEADS = 8
HEAD_DIM = 64
ATTN_W = N_HEADS * 2 * HEAD_DIM
CONV_W = 512
CONV_K = 31
N_EXPERTS = 256
TOP_K = 8
N_GROUPS = 8
TOPK_GROUPS = 4
GROUP_SIZE = N_EXPERTS // N_GROUPS
EXPERT_HIDDEN = 256
ROUTE_SCALE = 2.5
ROPE_BASE = 10000.0
EPS = 1e-6
N_MOD = 6
IN_COLS = 3 * ATTN_W + 2 * CONV_W + 2 * D_MODEL
LAM_INIT = 0.8 - 0.6 * math.exp(-0.3 * 0)
LOG2E = math.log2(math.e)

LANES = 128
SUBLANES = 8
VMEM_LIMIT = 56 * 1024 * 1024
HALO = 16
CONV_ROWS = 64
ROW_TILE = 256
CTX_HEADS_PER_STEP = 8
LAT_HEADS_PER_STEP = 2
ATTN_KEY_CHUNK = 512
MOE_BLOCK = 256
MOD_ROWS = 8
MOD_COL_TILES = 4
PACK = D_MODEL // 2 // LANES
CHUNK = 4
PAIR = SUBLANES // PACK
STAGE = ROW_TILE * TOP_K + N_EXPERTS * (CHUNK - 1)
REC_CHUNKS = 768
REC = 2048
assert STAGE // CHUNK <= REC_CHUNKS and REC_CHUNKS % LANES == 0

BF16 = jnp.bfloat16
F32 = jnp.float32


class Rows(NamedTuple):
    tiles_ctx: int
    tiles_lat: int
    tiles_per_ctx: int
    tiles_per_lat: int

    @property
    def tiles(self):
        return self.tiles_ctx + self.tiles_lat

    def ctx_tile(self, i):
        return jnp.minimum(i, self.tiles_ctx - 1)

    def lat_tile(self, i):
        return jnp.maximum(i - self.tiles_ctx, 0)

    def mod_row(self, i):
        return jnp.where(i < self.tiles_ctx, 0, 1 + self.lat_tile(i) // self.tiles_per_lat)

    def seq_tile(self, i):
        is_ctx = i < self.tiles_ctx
        per = jnp.where(is_ctx, self.tiles_per_ctx, self.tiles_per_lat)
        return jnp.where(is_ctx, i % self.tiles_per_ctx, self.lat_tile(i) % self.tiles_per_lat), per


def _cparams(sem):
    return pltpu.CompilerParams(dimension_semantics=sem, vmem_limit_bytes=VMEM_LIMIT)


def _row_spec(width, tile=lambda i: i):
    return pl.BlockSpec((ROW_TILE, width), lambda i: (tile(i), 0))


def _const_spec(shape):
    return pl.BlockSpec(shape, lambda i: (0,) * len(shape))


def _mod_spec(rows):
    return pl.BlockSpec((1, 1, N_MOD * D_MODEL), lambda i: (rows.mod_row(i), 0, 0))


def _silu(x):
    return x * jax.nn.sigmoid(x)


def _dot(a, b):
    return jnp.dot(a, b, preferred_element_type=F32)


def _dot_nt(a, b):
    return lax.dot_general(a, b, (((1,), (1,)), ((), ())), preferred_element_type=F32)


def _rms(x, g):
    return x * lax.rsqrt(jnp.mean(x * x, axis=-1, keepdims=True) + EPS) * g


def _packed_rows(token, n_tokens, align):
    return pl.ds(pl.multiple_of(token * PACK, align * PACK), n_tokens * PACK)


def _store_packed(ref, token0, x):
    half = D_MODEL // 2
    lo = pltpu.bitcast(x[:, :half], jnp.uint32) >> 16
    hi = pltpu.bitcast(x[:, half:], jnp.uint32) & jnp.uint32(0xFFFF0000)
    w = lo | hi
    for s in range(PACK):
        ref[pl.ds(token0 * PACK + s, x.shape[0], stride=PACK), :] = w[:, s * LANES:(s + 1) * LANES]


def _load_packed(ref, token0, n_tokens):
    w = jnp.concatenate([ref[pl.ds(token0 * PACK + s, n_tokens, stride=PACK), :] for s in range(PACK)], axis=1)
    lo = pltpu.bitcast(w << 16, F32)
    hi = pltpu.bitcast(w & jnp.uint32(0xFFFF0000), F32)
    return jnp.concatenate([lo, hi], axis=1).astype(BF16)


def _mod_kernel(c_ref, w_ref, b_ref, o_ref):
    a = _silu(c_ref[...]).astype(BF16)
    o_ref[...] = _dot(a, w_ref[...].astype(BF16)) + b_ref[...]


def _mod_table(cc, w_mod, b_mod):
    n = w_mod.shape[1]
    tn = n // MOD_COL_TILES
    return pl.pallas_call(
        _mod_kernel,
        grid=(MOD_COL_TILES,),
        in_specs=[pl.BlockSpec((MOD_ROWS, D_MODEL), lambda j: (0, 0)),
                  pl.BlockSpec((D_MODEL, tn), lambda j: (0, j)),
                  pl.BlockSpec((1, tn), lambda j: (0, j))],
        out_specs=pl.BlockSpec((MOD_ROWS, tn), lambda j: (0, j)),
        out_shape=jax.ShapeDtypeStruct((MOD_ROWS, n), F32),
        compiler_params=_cparams(("arbitrary",)),
        name="mod_table",
    )(cc, w_mod, b_mod.reshape(1, n))


def _rope(x, cos, sin, lane_lo):
    swapped = jnp.where(lane_lo, pltpu.roll(x, LANES - 16, axis=1), pltpu.roll(x, 16, axis=1))
    return x * cos + swapped * sin


def _inproj_kernel(rows, xp_ref, xs_ref, mod_ref, g_ref, w_ref, cos_ref, sin_ref,
                   q_ref, k_ref, v_ref, z_ref, gt_ref, ks_ref, vs_ref):
    is_ctx = pl.program_id(0) < rows.tiles_ctx
    mod = mod_ref[0]
    shift, scale = mod[:, 0:D_MODEL], mod[:, D_MODEL:2 * D_MODEL]
    x = jnp.where(is_ctx, xp_ref[...], xs_ref[...])
    h = _rms(x, g_ref[...]) * (1.0 + scale) + shift
    hb = h.astype(BF16)
    q = _dot(hb, w_ref[:, 0:ATTN_W]) * (HEAD_DIM ** -0.5 * LOG2E)
    k = _dot(hb, w_ref[:, ATTN_W:2 * ATTN_W])
    v = _dot(hb, w_ref[:, 2 * ATTN_W:3 * ATTN_W])

    @pl.when(is_ctx)
    def _():
        q_ref[...] = q.astype(BF16)
        k_ref[...] = k.astype(BF16)
        ks_ref[...] = k
        vs_ref[...] = v

    @pl.when(jnp.logical_not(is_ctx))
    def _():
        cos, sin = cos_ref[...], sin_ref[...]
        lane_lo = (lax.broadcasted_iota(jnp.int32, cos.shape, 1) % 32) < 16
        for j in range(ATTN_W // LANES):
            sl = slice(j * LANES, (j + 1) * LANES)
            q_ref[:, sl] = _rope(q[:, sl], cos, sin, lane_lo).astype(BF16)
            k_ref[:, sl] = _rope(k[:, sl], cos, sin, lane_lo).astype(BF16)

    v_ref[...] = v.astype(BF16)
    u = _dot(hb, w_ref[:, 3 * ATTN_W:3 * ATTN_W + 2 * CONV_W])
    z_ref[...] = u[:, :CONV_W] * jax.nn.sigmoid(u[:, CONV_W:])
    gt_ref[...] = jax.nn.sigmoid(_dot(hb, w_ref[:, 3 * ATTN_W + 2 * CONV_W:]))


def _inproj(rows, xp, xs, mod3, norm_g, w_in_b, rope):
    n_rows = rows.tiles * ROW_TILE
    rows_p = rows.tiles_ctx * ROW_TILE
    rope_spec = pl.BlockSpec((ROW_TILE, LANES), lambda i: (rows.lat_tile(i) % rows.tiles_per_lat, 0))
    shapes = lambda w, dt, r=n_rows: jax.ShapeDtypeStruct((r, w), dt)
    return pl.pallas_call(
        functools.partial(_inproj_kernel, rows),
        grid=(rows.tiles,),
        in_specs=[_row_spec(D_MODEL, rows.ctx_tile), _row_spec(D_MODEL, rows.lat_tile), _mod_spec(rows),
                  _const_spec((1, D_MODEL)),
                  pl.BlockSpec((D_MODEL, IN_COLS), lambda i: (0, 0), pipeline_mode=pl.Buffered(1)),
                  rope_spec, rope_spec],
        out_specs=[_row_spec(ATTN_W)] * 3 + [_row_spec(CONV_W), _row_spec(2 * D_MODEL)]
                  + [_row_spec(ATTN_W, rows.ctx_tile)] * 2,
        out_shape=[shapes(ATTN_W, BF16)] * 3 + [shapes(CONV_W, F32), shapes(2 * D_MODEL, F32)]
                  + [shapes(ATTN_W, F32, rows_p)] * 2,
        compiler_params=_cparams(("arbitrary",)),
        name="inproj",
    )(xp, xs, mod3, norm_g, w_in_b, *rope)


def _rope_tables(n_tokens):
    t = jnp.arange(n_tokens, dtype=jnp.int32)
    pos = jnp.stack([t // GRID_W, t % GRID_W], axis=-1).astype(F32)
    half = HEAD_DIM // 2
    inv = ROPE_BASE ** (-jnp.arange(0, half, 2, dtype=F32) / half)
    ang = pos[:, :, None] * inv
    cos, sin = jnp.cos(ang), jnp.sin(ang)
    cos64 = jnp.concatenate([cos[:, 0], cos[:, 0], cos[:, 1], cos[:, 1]], axis=-1)
    sin64 = jnp.concatenate([-sin[:, 0], sin[:, 0], -sin[:, 1], sin[:, 1]], axis=-1)
    return jnp.tile(cos64, (1, LANES // HEAD_DIM)), jnp.tile(sin64, (1, LANES // HEAD_DIM))


def _lane_groups(x):
    return [x[:, j * LANES:(j + 1) * LANES] for j in range(x.shape[1] // LANES)]


def _attn_kernel(has_cache, heads, lq_ref, sg_ref, q_ref, k_ref, v_ref, *rest):
    if has_cache:
        ck_ref, cv_ref, o_ref, *bufs = rest
    else:
        o_ref, *bufs = rest

    def s_ref(u, mp):
        return bufs[(u % 2) * 2 + mp]

    lq = lq_ref[...]
    lam = (jnp.exp(jnp.sum(lq[0:1] * lq[1:2], axis=-1, keepdims=True))
           - jnp.exp(jnp.sum(lq[2:3] * lq[3:4], axis=-1, keepdims=True)) + LAM_INIT)
    tq, seq = q_ref.shape[0], k_ref.shape[0]
    chunks = [(off, min(ATTN_KEY_CHUNK, seq - off), False) for off in range(0, seq, ATTN_KEY_CHUNK)]
    if has_cache:
        chunks.append((seq, ck_ref.shape[0], True))
    first = lax.broadcasted_iota(jnp.int32, (tq, LANES), 1) < HEAD_DIM
    neg = jnp.full((tq, LANES), -jnp.inf, F32)
    heads_state = [dict() for _ in range(heads)]

    def lanes(u):
        return slice(u * LANES, (u + 1) * LANES)

    def load(main_ref, cache_ref, u, chunk):
        off, size, cached = chunk
        if cached:
            return cache_ref[:, lanes(u)].astype(BF16)
        return main_ref[off:off + size, lanes(u)]

    def scores(u, chunk):
        st = heads_state[u]
        if "q" not in st:
            q = q_ref[:, lanes(u)]
            zero = jnp.zeros_like(q)
            st["q"] = (jnp.where(first, q, zero), jnp.where(first, zero, q))
            st["macc"] = [neg, neg]
        kk = load(k_ref, ck_ref if has_cache else None, u, chunk)
        off, size, _ = chunk
        for mp in range(2):
            s = _dot_nt(st["q"][mp], kk)
            s_ref(u, mp)[:, off:off + size] = s
            st["macc"][mp] = functools.reduce(jnp.maximum, _lane_groups(s), st["macc"][mp])

    def exps(u, chunk):
        st = heads_state[u]
        if "m" not in st:
            st["m"] = [jnp.max(a, axis=-1, keepdims=True) for a in st["macc"]]
            st["lacc"] = [jnp.zeros((tq, LANES), F32)] * 2
        off, size, _ = chunk
        for mp in range(2):
            e = jnp.exp2(s_ref(u, mp)[:, off:off + size] - st["m"][mp])
            s_ref(u, mp)[:, off:off + size] = e
            st["lacc"][mp] = functools.reduce(jnp.add, _lane_groups(e), st["lacc"][mp])

    def values(u, chunk):
        st = heads_state[u]
        if "r" not in st:
            l1, l2 = [jnp.sum(a, axis=-1, keepdims=True) for a in st["lacc"]]
            st["r"] = (1.0 / l1, lam / l2)
            st["o"] = jnp.zeros((tq, LANES), F32)
        off, size, _ = chunk
        w = s_ref(u, 0)[:, off:off + size] * st["r"][0] - s_ref(u, 1)[:, off:off + size] * st["r"][1]
        st["o"] = st["o"] + _dot(w.astype(BF16), load(v_ref, cv_ref if has_cache else None, u, chunk))

    stages = (scores, exps, values)
    for phase in range(heads + len(stages) - 1):
        for chunk in chunks:
            for s in reversed(range(len(stages))):
                u = phase - s
                if 0 <= u < heads:
                    stages[s](u, chunk)
        u = phase - (len(stages) - 1)
        if 0 <= u < heads:
            o_ref[:, lanes(u)] = (_rms(heads_state[u]["o"], sg_ref[...]) * (1.0 - LAM_INIT)).astype(BF16)


def _attention(q, k, v, lambda_qk, subln_g, row0, batch, seq_len, heads, cache=None):
    tq = ROW_TILE
    q_tiles = seq_len // tq
    assert row0 % seq_len == 0
    seq0 = row0 // seq_len
    width = heads * LANES
    head_q = pl.BlockSpec((tq, width), lambda b, h, i: (seq0 * q_tiles + b * q_tiles + i, h))
    head_kv = pl.BlockSpec((seq_len, width), lambda b, h, i: (seq0 + b, h))
    in_specs = [pl.BlockSpec((4, HEAD_DIM), lambda b, h, i: (0, 0)),
                pl.BlockSpec((1, 2 * HEAD_DIM), lambda b, h, i: (0, 0)),
                head_q, head_kv, head_kv]
    args = [lambda_qk, subln_g, q, k, v]
    n_keys = seq_len
    if cache is not None:
        past = cache[0].shape[1]
        n_keys += past
        head_cache = pl.BlockSpec((None, past, width), lambda b, h, i: (b, 0, h))
        in_specs += [head_cache, head_cache]
        args += list(cache)
    return pl.pallas_call(
        functools.partial(_attn_kernel, cache is not None, heads),
        grid=(batch, N_HEADS // heads, q_tiles),
        in_specs=in_specs,
        out_specs=pl.BlockSpec((tq, width), lambda b, h, i: (b * q_tiles + i, h)),
        out_shape=jax.ShapeDtypeStruct((batch * seq_len, ATTN_W), BF16),
        scratch_shapes=[pltpu.VMEM((tq, n_keys), F32)] * 4,
        compiler_params=_cparams(("parallel", "parallel", "arbitrary")),
        name="attn_latent" if cache is not None else "attn_ctx",
    )(*args)


def _merge_kernel(rows, xp_ref, xs_ref, onp_ref, ons_ref, z_ref, zp_ref, zn_ref, gt_ref, mod_ref, cw_ref, cb_ref,
                  lg_ref, lb_ref, wco_ref, wap_ref, wout_ref, n2_ref, x1_ref, h2_ref, zext_ref):
    i = pl.program_id(0)
    is_ctx = i < rows.tiles_ctx
    tm = ROW_TILE
    t, per = rows.seq_tile(i)
    zext_ref[HALO:HALO + tm, :] = z_ref[...]
    zext_ref[0:HALO, :] = jnp.where(t == 0, 0.0, zp_ref[...])
    zext_ref[HALO + tm:, :] = jnp.where(t == per - 1, 0.0, zn_ref[...])
    pad = HALO - CONV_K // 2
    col_blocks = []
    for cb in range(CONV_W // LANES):
        cs = slice(cb * LANES, (cb + 1) * LANES)
        row_chunks = []
        for r0 in range(0, tm, CONV_ROWS):
            acc = jnp.zeros((CONV_ROWS, LANES), F32) + cb_ref[:, cs]
            for j in range(CONV_K):
                acc = acc + zext_ref[r0 + j + pad:r0 + j + pad + CONV_ROWS, cs] * cw_ref[j:j + 1, cs]
            row_chunks.append(acc)
        col_blocks.append(jnp.concatenate(row_chunks, axis=0))
    c = jnp.concatenate(col_blocks, axis=1)
    mu = jnp.mean(c, axis=-1, keepdims=True)
    cc = c - mu
    y = cc * lax.rsqrt(jnp.mean(cc * cc, axis=-1, keepdims=True) + EPS) * lg_ref[...] + lb_ref[...]
    conv_out = _dot(_silu(y).astype(BF16), wco_ref[...])
    o_n = jnp.where(is_ctx, onp_ref[...], ons_ref[...])
    a_br = _dot(o_n, wap_ref[...])
    gt = gt_ref[...]
    merged = gt[:, :D_MODEL] * a_br + gt[:, D_MODEL:] * conv_out
    mod = mod_ref[0]
    gate1 = mod[:, 2 * D_MODEL:3 * D_MODEL]
    shift2, scale2 = mod[:, 3 * D_MODEL:4 * D_MODEL], mod[:, 4 * D_MODEL:5 * D_MODEL]
    x = jnp.where(is_ctx, xp_ref[...], xs_ref[...])
    x1 = x + gate1 * _dot(merged.astype(BF16), wout_ref[...])
    x1_ref[...] = x1
    h2_ref[...] = (_rms(x1, n2_ref[...]) * (1.0 + scale2) + shift2).astype(BF16)


def _merge(rows, xp, xs, on_p, on_s, z, gt, mod3, conv_w, conv_b, ln_g, ln_b, wco_b, wap_b, wout_b, norm2_g):
    n_rows = rows.tiles * ROW_TILE
    hb = ROW_TILE // HALO
    n_halo_blocks = n_rows // HALO
    in_specs = [_row_spec(D_MODEL, rows.ctx_tile), _row_spec(D_MODEL, rows.lat_tile),
                _row_spec(ATTN_W, rows.ctx_tile), _row_spec(ATTN_W, rows.lat_tile),
                _row_spec(CONV_W),
                pl.BlockSpec((HALO, CONV_W), lambda i: (jnp.maximum(i * hb - 1, 0), 0)),
                pl.BlockSpec((HALO, CONV_W), lambda i: (jnp.minimum((i + 1) * hb, n_halo_blocks - 1), 0)),
                _row_spec(2 * D_MODEL), _mod_spec(rows),
                _const_spec((CONV_K, CONV_W)), _const_spec((1, CONV_W)), _const_spec((1, CONV_W)),
                _const_spec((1, CONV_W)), _const_spec((CONV_W, D_MODEL)), _const_spec((ATTN_W, D_MODEL)),
                _const_spec((D_MODEL, D_MODEL)), _const_spec((1, D_MODEL))]
    return pl.pallas_call(
        functools.partial(_merge_kernel, rows),
        grid=(rows.tiles,),
        in_specs=in_specs,
        out_specs=[_row_spec(D_MODEL), _row_spec(D_MODEL)],
        out_shape=[jax.ShapeDtypeStruct((n_rows, D_MODEL), F32), jax.ShapeDtypeStruct((n_rows, D_MODEL), BF16)],
        scratch_shapes=[pltpu.VMEM((ROW_TILE + 2 * HALO, CONV_W), F32)],
        compiler_params=_cparams(("arbitrary",)),
        name="conv_merge",
    )(xp, xs, on_p, on_s, z, z, z, gt, mod3, conv_w, conv_b, ln_g, ln_b, wco_b, wap_b, wout_b, norm2_g)


def _pack_cols(cols, shape):
    lane = lax.broadcasted_iota(jnp.int32, shape, 1)
    out = jnp.zeros(shape, F32)
    for j, col in enumerate(cols):
        out = jnp.where(lane == j, col, out)
    return out


def _router_kernel(h_ref, wr_ref, rb_ref, gate_ref, loc_ref, rec_ref, cnt_ref, run_ref):
    i = pl.program_id(0)

    @pl.when(i == 0)
    def _():
        run_ref[...] = jnp.zeros_like(run_ref)

    tm = ROW_TILE
    neg = jnp.float32(-jnp.inf)
    scores = jax.nn.sigmoid(_dot(h_ref[...], wr_ref[...]))
    biased = scores + rb_ref[...]
    lane_i = lax.broadcasted_iota(jnp.int32, scores.shape, 1)
    lane = lane_i.astype(F32)
    far = jnp.float32(2 * N_EXPERTS)

    def first_argmax(v):
        m = jnp.max(v, axis=-1, keepdims=True)
        return m, jnp.min(jnp.where(v == m, lane, far), axis=-1, keepdims=True)

    in_group, gscore = [], []
    for g in range(N_GROUPS):
        inb = (lane_i >= g * GROUP_SIZE) & (lane_i < (g + 1) * GROUP_SIZE)
        v = jnp.where(inb, biased, neg)
        m1, i1 = first_argmax(v)
        m2 = jnp.max(jnp.where(lane == i1, neg, v), axis=-1, keepdims=True)
        in_group.append(inb)
        gscore.append(m1 + m2)
    allowed = jnp.zeros(scores.shape, F32)
    for g in range(N_GROUPS):
        ahead = jnp.zeros((tm, 1), F32)
        for g2 in range(N_GROUPS):
            if g2 < g:
                ahead = ahead + (gscore[g2] >= gscore[g]).astype(F32)
            elif g2 > g:
                ahead = ahead + (gscore[g2] > gscore[g]).astype(F32)
        keep = (ahead < TOPK_GROUPS).astype(F32)
        allowed = jnp.where(in_group[g], keep, allowed)
    masked = jnp.where(allowed > 0.0, biased, neg)
    picked = jnp.zeros(scores.shape, F32)
    idxs, gates = [], []
    for _ in range(TOP_K):
        _, ik = first_argmax(masked)
        hit = lane == ik
        gates.append(jnp.sum(jnp.where(hit, scores, 0.0), axis=-1, keepdims=True))
        masked = jnp.where(hit, neg, masked)
        picked = jnp.where(hit, 1.0, picked)
        idxs.append(ik)
    gsum = functools.reduce(jnp.add, gates)
    gates = [g / gsum * ROUTE_SCALE for g in gates]
    r_i = lax.broadcasted_iota(jnp.int32, (tm, tm), 0)
    c_i = lax.broadcasted_iota(jnp.int32, (tm, tm), 1)
    before = (c_i < r_i).astype(BF16)
    local_rank = _dot(before, picked.astype(BF16))
    n_tok = jnp.sum(picked, axis=0, keepdims=True)
    n_chunks = jnp.floor((n_tok + (CHUNK - 1)) * (1.0 / CHUNK))
    lower = (lax.broadcasted_iota(jnp.int32, (N_EXPERTS, N_EXPERTS), 0)
             < lax.broadcasted_iota(jnp.int32, (N_EXPERTS, N_EXPERTS), 1)).astype(BF16)
    chunks_before = _dot(jnp.broadcast_to(n_chunks, (SUBLANES, N_EXPERTS)).astype(BF16), lower)[0:1]
    staged_off = chunks_before * CHUNK
    staged = local_rank + staged_off
    locs = [jnp.sum(jnp.where(lane == ik, staged, 0.0), axis=-1, keepdims=True) for ik in idxs]
    e_r = lax.broadcasted_iota(jnp.int32, (N_EXPERTS, N_EXPERTS), 0)
    e_c = lax.broadcasted_iota(jnp.int32, (N_EXPERTS, N_EXPERTS), 1)
    column = lambda v: jnp.sum(jnp.where(e_r == e_c, v, 0.0), axis=1, keepdims=True)
    first = column(chunks_before)
    last = first + column(n_chunks)
    region_off = column(run_ref[...] - staged_off)
    j = lax.broadcasted_iota(jnp.int32, (N_EXPERTS, REC_CHUNKS), 1).astype(F32)
    owns = (first <= j) & (j < last)
    expert = lax.broadcasted_iota(jnp.int32, (N_EXPERTS, REC_CHUNKS), 0).astype(F32)
    chunk_expert = jnp.sum(jnp.where(owns, expert, 0.0), axis=0, keepdims=True)
    chunk_off = jnp.sum(jnp.where(owns, region_off, 0.0), axis=0, keepdims=True) + j[0:1] * CHUNK
    total = jnp.broadcast_to(jnp.sum(n_chunks, axis=1, keepdims=True), (1, REC - 2 * REC_CHUNKS))
    rec_ref[0] = jnp.concatenate([chunk_expert, chunk_off, total], axis=1).astype(jnp.int32)
    run_ref[...] = run_ref[...] + jnp.floor((n_tok + (PAIR - 1)) * (1.0 / PAIR)) * PAIR
    cnt_ref[...] = run_ref[...]
    shape = gate_ref.shape
    gate_ref[...] = _pack_cols(gates, shape)
    loc_ref[...] = _pack_cols(locs, shape).astype(jnp.int32)


def _router(rows, h2, w_router_b, router_bias):
    n_rows = rows.tiles * ROW_TILE
    return pl.pallas_call(
        _router_kernel,
        grid=(rows.tiles,),
        in_specs=[_row_spec(D_MODEL), _const_spec((D_MODEL, N_EXPERTS)), _const_spec((1, N_EXPERTS))],
        out_specs=[_row_spec(LANES), _row_spec(LANES), pl.BlockSpec((1, 1, REC), lambda i: (i, 0, 0)),
                   _const_spec((1, N_EXPERTS))],
        out_shape=[jax.ShapeDtypeStruct((n_rows, LANES), F32), jax.ShapeDtypeStruct((n_rows, LANES), jnp.int32),
                   jax.ShapeDtypeStruct((rows.tiles, 1, REC), jnp.int32),
                   jax.ShapeDtypeStruct((1, N_EXPERTS), F32)],
        scratch_shapes=[pltpu.VMEM((1, N_EXPERTS), F32)],
        compiler_params=_cparams(("arbitrary",)),
        name="router",
    )(h2, w_router_b, router_bias)


def _load_record(i, rec_hbm, rsm, sem_i):
    cp = pltpu.make_async_copy(rec_hbm.at[pl.ds(i * REC, REC)], rsm, sem_i)
    cp.start()
    cp.wait()


def _move_chunks(ps_ref, rsm, staged_ref, sorted_hbm, sem, chunks, to_sorted):
    total = rsm[2 * REC_CHUNKS]
    spare0 = sorted_hbm.shape[0] // PACK - STAGE
    for j in chunks:
        placed_tok = jnp.where(j < total, ps_ref[rsm[j]] + rsm[REC_CHUNKS + j], spare0 + j * CHUNK)
        staged = staged_ref.at[pl.ds(j * CHUNK * PACK, CHUNK * PACK)]
        placed = sorted_hbm.at[_packed_rows(placed_tok, CHUNK, PAIR)]
        src, dst = (staged, placed) if to_sorted else (placed, staged)
        pltpu.make_async_copy(src, dst, sem).start()


def _wait_all_chunks(staged_ref, sorted_hbm, sem):
    pltpu.make_async_copy(staged_ref, sorted_hbm.at[pl.ds(0, STAGE * PACK)], sem).wait()


def _staged_block(loc, rb):
    col = lax.broadcasted_iota(jnp.int32, (loc.shape[0], MOE_BLOCK), 1) + rb * MOE_BLOCK
    return [loc[:, k:k + 1] == col for k in range(TOP_K)]


def _dispatch_kernel(pe_ref, ps_ref, c2_ref, rec_hbm, loc_ref, h_ref, xs_hbm, rsm, stage, zbuf, sem_i, sem):
    i = pl.program_id(0)
    block_rows = MOE_BLOCK * PACK

    @pl.when(i == 0)
    def _():
        zbuf[...] = jnp.zeros_like(zbuf)
        n_blocks = xs_hbm.shape[0] // block_rows
        n_used = pe_ref[N_EXPERTS - 1] // MOE_BLOCK

        def clear_block(b, carry):
            dst = xs_hbm.at[pl.ds(pl.multiple_of(b * block_rows, block_rows), block_rows)]
            pltpu.make_async_copy(zbuf, dst, sem).start()
            return carry

        def clear_padding(e, total):
            first = (ps_ref[e] + c2_ref[e]) // MOE_BLOCK
            last = pe_ref[e] // MOE_BLOCK
            lax.fori_loop(first, last, clear_block, 0)
            return total + last - first

        def clear_wait(e, carry):
            pltpu.make_async_copy(zbuf, xs_hbm.at[pl.ds(0, block_rows)], sem).wait()
            return carry

        n_cleared = lax.fori_loop(0, N_EXPERTS, clear_padding, 0)
        lax.fori_loop(n_used, n_blocks, clear_block, 0)
        lax.fori_loop(0, n_cleared + n_blocks - n_used, clear_wait, 0)

    _load_record(i, rec_hbm, rsm, sem_i)
    loc, hb = loc_ref[...], h_ref[...]
    for rb in range(STAGE // MOE_BLOCK):
        onehot = jnp.zeros((ROW_TILE, MOE_BLOCK), F32)
        for hit in _staged_block(loc, rb):
            onehot = jnp.where(hit, 1.0, onehot)
        rows = lax.dot_general(onehot.astype(BF16), hb, (((0,), (0,)), ((), ())), preferred_element_type=F32)
        _store_packed(stage, rb * MOE_BLOCK, rows)
        per_block = MOE_BLOCK // CHUNK
        _move_chunks(ps_ref, rsm, stage, xs_hbm, sem, range(rb * per_block, (rb + 1) * per_block), to_sorted=True)
    _wait_all_chunks(stage, xs_hbm, sem)


def _dispatch(rows, pad_end, pad_start, counts, rec_flat, locs, h2, n_rows_sorted):
    return pl.pallas_call(
        _dispatch_kernel,
        grid_spec=pltpu.PrefetchScalarGridSpec(
            num_scalar_prefetch=3,
            grid=(rows.tiles,),
            in_specs=[pl.BlockSpec(memory_space=pl.ANY),
                      pl.BlockSpec((ROW_TILE, LANES), lambda i, pe, ps, c2: (i, 0)),
                      pl.BlockSpec((ROW_TILE, D_MODEL), lambda i, pe, ps, c2: (i, 0))],
            out_specs=pl.BlockSpec(memory_space=pl.ANY),
            scratch_shapes=[pltpu.SMEM((REC,), jnp.int32),
                            pltpu.VMEM((STAGE * PACK, LANES), jnp.uint32),
                            pltpu.VMEM((MOE_BLOCK * PACK, LANES), jnp.uint32),
                            pltpu.SemaphoreType.DMA, pltpu.SemaphoreType.DMA]),
        out_shape=jax.ShapeDtypeStruct(((n_rows_sorted + STAGE) * PACK, LANES), jnp.uint32),
        compiler_params=_cparams(("arbitrary",)),
        name="moe_dispatch",
    )(pad_end, pad_start, counts, rec_flat, locs, h2)


def _expert_kernel(be_ref, nu_ref, x_ref, w1_ref, w3_ref, w2_ref, y_ref, w1b, w3b, w2b):
    b = pl.program_id(0)

    @pl.when(b < nu_ref[0])
    def _():
        changed = (b == 0) | (be_ref[b] != be_ref[jnp.maximum(b - 1, 0)])

        @pl.when(changed)
        def _():
            w1b[...] = w1_ref[...].astype(BF16)
            w3b[...] = w3_ref[...].astype(BF16)
            w2b[...] = w2_ref[...].astype(BF16)

        xb = _load_packed(x_ref, 0, MOE_BLOCK)
        hid = (_silu(_dot(xb, w1b[...])) * _dot(xb, w3b[...])).astype(BF16)
        _store_packed(y_ref, 0, _dot(hid, w2b[...]).astype(BF16).astype(F32))


def _experts(block_e, n_used, xs, w1, w3, w2):
    n_blocks = (xs.shape[0] // PACK - STAGE) // MOE_BLOCK
    row_map = lambda b, be, nu: (jnp.minimum(b, nu[0] - 1), 0)
    return pl.pallas_call(
        _expert_kernel,
        grid_spec=pltpu.PrefetchScalarGridSpec(
            num_scalar_prefetch=2,
            grid=(n_blocks,),
            in_specs=[pl.BlockSpec((MOE_BLOCK * PACK, LANES), row_map),
                      pl.BlockSpec((None, D_MODEL, EXPERT_HIDDEN), lambda b, be, nu: (be[b], 0, 0)),
                      pl.BlockSpec((None, D_MODEL, EXPERT_HIDDEN), lambda b, be, nu: (be[b], 0, 0)),
                      pl.BlockSpec((None, EXPERT_HIDDEN, D_MODEL), lambda b, be, nu: (be[b], 0, 0))],
            out_specs=pl.BlockSpec((MOE_BLOCK * PACK, LANES), row_map),
            scratch_shapes=[pltpu.VMEM((D_MODEL, EXPERT_HIDDEN), BF16),
                            pltpu.VMEM((D_MODEL, EXPERT_HIDDEN), BF16),
                            pltpu.VMEM((EXPERT_HIDDEN, D_MODEL), BF16)]),
        out_shape=jax.ShapeDtypeStruct(xs.shape, jnp.uint32),
        input_output_aliases={2: 0},
        compiler_params=_cparams(("arbitrary",)),
        name="moe_experts",
    )(block_e, n_used, xs, w1, w3, w2)


def _combine_kernel(rows, ps_ref, rec_hbm, ys_hbm, loc_ref, x1_ref, h_ref, gate_ref, mod_ref, ws1_ref, ws3_ref,
                    ws2_ref, nf_ref, outp_ref, outs_ref, rsm, ybuf, sem_i, sem):
    i = pl.program_id(0)
    _load_record(i, rec_hbm, rsm, sem_i)
    _move_chunks(ps_ref, rsm, ybuf, ys_hbm, sem, range(STAGE // CHUNK), to_sorted=False)
    hb = h_ref[...]
    shared = _dot((_silu(_dot(hb, ws1_ref[...])) * _dot(hb, ws3_ref[...])).astype(BF16), ws2_ref[...])
    _wait_all_chunks(ybuf, ys_hbm, sem)
    loc, g = loc_ref[...], gate_ref[...]
    routed = jnp.zeros((ROW_TILE, D_MODEL), F32)
    for rb in range(STAGE // MOE_BLOCK):
        gm = jnp.zeros((ROW_TILE, MOE_BLOCK), F32)
        for k, hit in enumerate(_staged_block(loc, rb)):
            gm = jnp.where(hit, g[:, k:k + 1], gm)
        g_hi = gm.astype(BF16)
        g_lo = (gm - g_hi.astype(F32)).astype(BF16)
        yb = _load_packed(ybuf, rb * MOE_BLOCK, MOE_BLOCK)
        routed = routed + _dot(g_hi, yb) + _dot(g_lo, yb)
    gate2 = mod_ref[0][:, 5 * D_MODEL:6 * D_MODEL]
    x2 = x1_ref[...] + gate2 * (routed + shared)
    out = _rms(x2, nf_ref[...])

    @pl.when(i < rows.tiles_ctx)
    def _():
        outp_ref[...] = out

    @pl.when(i >= rows.tiles_ctx)
    def _():
        outs_ref[...] = out


def _combine(rows, pad_start, rec_flat, ys, locs, x1, h2, gates, mod3, ws1_b, ws3_b, ws2_b, normf_g):
    drop = lambda spec: pl.BlockSpec(spec.block_shape, lambda i, ps, f=spec.index_map: f(i))
    any_spec = pl.BlockSpec(memory_space=pl.ANY)
    return pl.pallas_call(
        functools.partial(_combine_kernel, rows),
        grid_spec=pltpu.PrefetchScalarGridSpec(
            num_scalar_prefetch=1,
            grid=(rows.tiles,),
            in_specs=[any_spec, any_spec]
                     + [drop(s) for s in (
                         _row_spec(LANES), _row_spec(D_MODEL), _row_spec(D_MODEL), _row_spec(LANES),
                         _mod_spec(rows),
                         _const_spec((D_MODEL, EXPERT_HIDDEN)), _const_spec((D_MODEL, EXPERT_HIDDEN)),
                         _const_spec((EXPERT_HIDDEN, D_MODEL)), _const_spec((1, D_MODEL)))],
            out_specs=[drop(_row_spec(D_MODEL, rows.ctx_tile)), drop(_row_spec(D_MODEL, rows.lat_tile))],
            scratch_shapes=[pltpu.SMEM((REC,), jnp.int32),
                            pltpu.VMEM((STAGE * PACK, LANES), jnp.uint32),
                            pltpu.SemaphoreType.DMA, pltpu.SemaphoreType.DMA]),
        out_shape=[jax.ShapeDtypeStruct((rows.tiles_ctx * ROW_TILE, D_MODEL), F32),
                   jax.ShapeDtypeStruct((rows.tiles_lat * ROW_TILE, D_MODEL), F32)],
        compiler_params=_cparams(("arbitrary",)),
        name="moe_combine",
    )(pad_start, rec_flat, ys, locs, x1, h2, gates, mod3, ws1_b, ws3_b, ws2_b, normf_g)


def kernel(x_prompt, x_sample, cache_k, cache_v, c, c_ctx, w_mod, b_mod, norm1_g, w_in, lambda_qk, subln_g,
           w_attn_proj, conv_w, conv_b, conv_ln_g, conv_ln_b, w_conv_out, w_out, norm2_g, w_router,
           router_bias, w1, w3, w2, ws1, ws3, ws2, normf_g):
    batch, seq, _ = x_prompt.shape
    dec_batch, dec_seq, _ = x_sample.shape
    past = cache_k.shape[2]
    l = 0
    rows_p, rows_s = batch * seq, dec_batch * dec_seq
    n_rows = rows_p + rows_s
    rows = Rows(rows_p // ROW_TILE, rows_s // ROW_TILE, seq // ROW_TILE, dec_seq // ROW_TILE)
    xp = x_prompt.reshape(rows_p, D_MODEL)
    xs = x_sample.reshape(rows_s, D_MODEL)
    row = lambda a: a.reshape(1, -1)

    cc = jnp.zeros((MOD_ROWS, D_MODEL), F32).at[0].set(c_ctx).at[1:1 + dec_batch].set(c)
    mod3 = _mod_table(cc, w_mod[l], b_mod[l]).reshape(MOD_ROWS, 1, N_MOD * D_MODEL)

    q, k, v, z, gt, state_k, state_v = _inproj(rows, xp, xs, mod3, row(norm1_g[l]), w_in[l].astype(BF16),
                                               _rope_tables(dec_seq))
    lq, sg = lambda_qk[l], row(subln_g[l])
    on_p = _attention(q, k, v, lq, sg, 0, batch, seq, CTX_HEADS_PER_STEP)
    cache = (cache_k[:, l].reshape(dec_batch, past, ATTN_W), cache_v[:, l].reshape(dec_batch, past, ATTN_W))
    on_s = _attention(q, k, v, lq, sg, rows_p, dec_batch, dec_seq, LAT_HEADS_PER_STEP, cache=cache)
    x1, h2 = _merge(rows, xp, xs, on_p, on_s, z, gt, mod3, conv_w[l], row(conv_b[l]), row(conv_ln_g[l]),
                    row(conv_ln_b[l]), w_conv_out[l].astype(BF16), w_attn_proj[l].astype(BF16),
                    w_out[l].astype(BF16), row(norm2_g[l]))

    gates, locs, rec, counts = _router(rows, h2, w_router[l].astype(BF16), row(router_bias[l]))
    counts = counts[0].astype(jnp.int32)
    padded = (counts + CHUNK + MOE_BLOCK - 1) // MOE_BLOCK * MOE_BLOCK
    pad_end = jnp.cumsum(padded).astype(jnp.int32)
    pad_start = pad_end - padded
    max_rows = n_rows * TOP_K + rows.tiles * N_EXPERTS * (PAIR - 1) + N_EXPERTS * (CHUNK + MOE_BLOCK - 1)
    n_blocks = -(-max_rows // MOE_BLOCK)
    blocks = jnp.arange(n_blocks, dtype=jnp.int32)
    block_e = jnp.minimum(jnp.sum((pad_end // MOE_BLOCK)[None, :] <= blocks[:, None], axis=-1),
                          N_EXPERTS - 1).astype(jnp.int32)
    n_used = (pad_end[-1:] // MOE_BLOCK).astype(jnp.int32)
    rec_flat = rec.reshape(-1)
    x_sorted = _dispatch(rows, pad_end, pad_start, counts, rec_flat, locs, h2, n_blocks * MOE_BLOCK)
    y_sorted = _experts(block_e, n_used, x_sorted, w1[l], w3[l], w2[l])
    y_p, y_s = _combine(rows, pad_start, rec_flat, y_sorted, locs, x1, h2, gates, mod3,
                        ws1[l].astype(BF16), ws3[l].astype(BF16), ws2[l].astype(BF16), row(normf_g))
    return (y_p.reshape(batch, seq, D_MODEL), y_s.reshape(dec_batch, dec_seq, D_MODEL),
            state_k.reshape(batch, 1, seq, N_HEADS, 2, HEAD_DIM),
            state_v.reshape(batch, 1, seq, N_HEADS, 2 * HEAD_DIM))
```

```python
import functools
import math
from typing import NamedTuple

import jax
import jax.numpy as jnp
from jax import lax
from jax.experimental import pallas as pl
from jax.experimental.pallas import tpu as pltpu

D_MODEL = 1024
GRID_W = 64
N_HEADS = 8
HEAD_DIM = 64
ATTN_W = N_HEADS * 2 * HEAD_DIM
CONV_W = 512
CONV_K = 31
N_EXPERTS = 256
TOP_K = 8
N_GROUPS = 8
TOPK_GROUPS = 4
GROUP_SIZE = N_EXPERTS // N_GROUPS
EXPERT_HIDDEN = 256
ROUTE_SCALE = 2.5
ROPE_BASE = 10000.0
EPS = 1e-6
N_MOD = 6
IN_COLS = 3 * ATTN_W + 2 * CONV_W + 2 * D_MODEL
LAM_INIT = 0.8 - 0.6 * math.exp(-0.3 * 0)
LOG2E = math.log2(math.e)

LANES = 128
SUBLANES = 8
VMEM_LIMIT = 56 * 1024 * 1024
HALO = 16
CONV_ROWS = 64
ROW_TILE = 256
CTX_HEADS_PER_STEP = 8
LAT_HEADS_PER_STEP = 2
ATTN_KEY_CHUNK = 512
MOE_BLOCK = 256
MOD_ROWS = 8
MOD_COL_TILES = 4
PACK = D_MODEL // 2 // LANES
CHUNK = 4
PAIR = SUBLANES // PACK
STAGE = ROW_TILE * TOP_K + N_EXPERTS * (CHUNK - 1)
REC_CHUNKS = 768
REC = 2048
assert STAGE // CHUNK <= REC_CHUNKS and REC_CHUNKS % LANES == 0

BF16 = jnp.bfloat16
F32 = jnp.float32


class Rows(NamedTuple):
    tiles_ctx: int
    tiles_lat: int
    tiles_per_ctx: int
    tiles_per_lat: int

    @property
    def tiles(self):
        return self.tiles_ctx + self.tiles_lat

    def ctx_tile(self, i):
        return jnp.minimum(i, self.tiles_ctx - 1)

    def lat_tile(self, i):
        return jnp.maximum(i - self.tiles_ctx, 0)

    def mod_row(self, i):
        return jnp.where(i < self.tiles_ctx, 0, 1 + self.lat_tile(i) // self.tiles_per_lat)

    def seq_tile(self, i):
        is_ctx = i < self.tiles_ctx
        per = jnp.where(is_ctx, self.tiles_per_ctx, self.tiles_per_lat)
        return jnp.where(is_ctx, i % self.tiles_per_ctx, self.lat_tile(i) % self.tiles_per_lat), per


def _cparams(sem):
    return pltpu.CompilerParams(dimension_semantics=sem, vmem_limit_bytes=VMEM_LIMIT)


def _row_spec(width, tile=lambda i: i):
    return pl.BlockSpec((ROW_TILE, width), lambda i: (tile(i), 0))


def _const_spec(shape):
    return pl.BlockSpec(shape, lambda i: (0,) * len(shape))


def _mod_spec(rows):
    return pl.BlockSpec((1, 1, N_MOD * D_MODEL), lambda i: (rows.mod_row(i), 0, 0))


def _silu(x):
    return x * jax.nn.sigmoid(x)


def _dot(a, b):
    return jnp.dot(a, b, preferred_element_type=F32)


def _dot_nt(a, b):
    return lax.dot_general(a, b, (((1,), (1,)), ((), ())), preferred_element_type=F32)


def _rms(x, g):
    return x * lax.rsqrt(jnp.mean(x * x, axis=-1, keepdims=True) + EPS) * g


def _packed_rows(token, n_tokens, align):
    return pl.ds(pl.multiple_of(token * PACK, align * PACK), n_tokens * PACK)


def _store_packed(ref, token0, x):
    half = D_MODEL // 2
    lo = pltpu.bitcast(x[:, :half], jnp.uint32) >> 16
    hi = pltpu.bitcast(x[:, half:], jnp.uint32) & jnp.uint32(0xFFFF0000)
    w = lo | hi
    for s in range(PACK):
        ref[pl.ds(token0 * PACK + s, x.shape[0], stride=PACK), :] = w[:, s * LANES:(s + 1) * LANES]


def _load_packed(ref, token0, n_tokens):
    w = jnp.concatenate([ref[pl.ds(token0 * PACK + s, n_tokens, stride=PACK), :] for s in range(PACK)], axis=1)
    lo = pltpu.bitcast(w << 16, F32)
    hi = pltpu.bitcast(w & jnp.uint32(0xFFFF0000), F32)
    return jnp.concatenate([lo, hi], axis=1).astype(BF16)


def _mod_kernel(c_ref, w_ref, b_ref, o_ref):
    a = _silu(c_ref[...]).astype(BF16)
    o_ref[...] = _dot(a, w_ref[...].astype(BF16)) + b_ref[...]


def _mod_table(cc, w_mod, b_mod):
    n = w_mod.shape[1]
    tn = n // MOD_COL_TILES
    return pl.pallas_call(
        _mod_kernel,
        grid=(MOD_COL_TILES,),
        in_specs=[pl.BlockSpec((MOD_ROWS, D_MODEL), lambda j: (0, 0)),
                  pl.BlockSpec((D_MODEL, tn), lambda j: (0, j)),
                  pl.BlockSpec((1, tn), lambda j: (0, j))],
        out_specs=pl.BlockSpec((MOD_ROWS, tn), lambda j: (0, j)),
        out_shape=jax.ShapeDtypeStruct((MOD_ROWS, n), F32),
        compiler_params=_cparams(("arbitrary",)),
        name="mod_table",
    )(cc, w_mod, b_mod.reshape(1, n))


def _rope(x, cos, sin, lane_lo):
    swapped = jnp.where(lane_lo, pltpu.roll(x, LANES - 16, axis=1), pltpu.roll(x, 16, axis=1))
    return x * cos + swapped * sin


def _inproj_kernel(rows, xp_ref, xs_ref, mod_ref, g_ref, w_ref, cos_ref, sin_ref,
                   q_ref, k_ref, v_ref, z_ref, gt_ref, ks_ref, vs_ref):
    is_ctx = pl.program_id(0) < rows.tiles_ctx
    mod = mod_ref[0]
    shift, scale = mod[:, 0:D_MODEL], mod[:, D_MODEL:2 * D_MODEL]
    x = jnp.where(is_ctx, xp_ref[...], xs_ref[...])
    h = _rms(x, g_ref[...]) * (1.0 + scale) + shift
    hb = h.astype(BF16)
    q = _dot(hb, w_ref[:, 0:ATTN_W]) * (HEAD_DIM ** -0.5 * LOG2E)
    k = _dot(hb, w_ref[:, ATTN_W:2 * ATTN_W])
    v = _dot(hb, w_ref[:, 2 * ATTN_W:3 * ATTN_W])

    @pl.when(is_ctx)
    def _():
        q_ref[...] = q.astype(BF16)
        k_ref[...] = k.astype(BF16)
        ks_ref[...] = k
        vs_ref[...] = v

    @pl.when(jnp.logical_not(is_ctx))
    def _():
        cos, sin = cos_ref[...], sin_ref[...]
        lane_lo = (lax.broadcasted_iota(jnp.int32, cos.shape, 1) % 32) < 16
        for j in range(ATTN_W // LANES):
            sl = slice(j * LANES, (j + 1) * LANES)
            q_ref[:, sl] = _rope(q[:, sl], cos, sin, lane_lo).astype(BF16)
            k_ref[:, sl] = _rope(k[:, sl], cos, sin, lane_lo).astype(BF16)

    v_ref[...] = v.astype(BF16)
    u = _dot(hb, w_ref[:, 3 * ATTN_W:3 * ATTN_W + 2 * CONV_W])
    z_ref[...] = u[:, :CONV_W] * jax.nn.sigmoid(u[:, CONV_W:])
    gt_ref[...] = jax.nn.sigmoid(_dot(hb, w_ref[:, 3 * ATTN_W + 2 * CONV_W:]))


def _inproj(rows, xp, xs, mod3, norm_g, w_in_b, rope):
    n_rows = rows.tiles * ROW_TILE
    rows_p = rows.tiles_ctx * ROW_TILE
    rope_spec = pl.BlockSpec((ROW_TILE, LANES), lambda i: (rows.lat_tile(i) % rows.tiles_per_lat, 0))
    shapes = lambda w, dt, r=n_rows: jax.ShapeDtypeStruct((r, w), dt)
    return pl.pallas_call(
        functools.partial(_inproj_kernel, rows),
        grid=(rows.tiles,),
        in_specs=[_row_spec(D_MODEL, rows.ctx_tile), _row_spec(D_MODEL, rows.lat_tile), _mod_spec(rows),
                  _const_spec((1, D_MODEL)),
                  pl.BlockSpec((D_MODEL, IN_COLS), lambda i: (0, 0), pipeline_mode=pl.Buffered(1)),
                  rope_spec, rope_spec],
        out_specs=[_row_spec(ATTN_W)] * 3 + [_row_spec(CONV_W), _row_spec(2 * D_MODEL)]
                  + [_row_spec(ATTN_W, rows.ctx_tile)] * 2,
        out_shape=[shapes(ATTN_W, BF16)] * 3 + [shapes(CONV_W, F32), shapes(2 * D_MODEL, F32)]
                  + [shapes(ATTN_W, F32, rows_p)] * 2,
        compiler_params=_cparams(("arbitrary",)),
        name="inproj",
    )(xp, xs, mod3, norm_g, w_in_b, *rope)


def _rope_tables(n_tokens):
    t = jnp.arange(n_tokens, dtype=jnp.int32)
    pos = jnp.stack([t // GRID_W, t % GRID_W], axis=-1).astype(F32)
    half = HEAD_DIM // 2
    inv = ROPE_BASE ** (-jnp.arange(0, half, 2, dtype=F32) / half)
    ang = pos[:, :, None] * inv
    cos, sin = jnp.cos(ang), jnp.sin(ang)
    cos64 = jnp.concatenate([cos[:, 0], cos[:, 0], cos[:, 1], cos[:, 1]], axis=-1)
    sin64 = jnp.concatenate([-sin[:, 0], sin[:, 0], -sin[:, 1], sin[:, 1]], axis=-1)
    return jnp.tile(cos64, (1, LANES // HEAD_DIM)), jnp.tile(sin64, (1, LANES // HEAD_DIM))


def _lane_groups(x):
    return [x[:, j * LANES:(j + 1) * LANES] for j in range(x.shape[1] // LANES)]


def _attn_kernel(has_cache, heads, lq_ref, sg_ref, q_ref, k_ref, v_ref, *rest):
    if has_cache:
        ck_ref, cv_ref, o_ref, *bufs = rest
    else:
        o_ref, *bufs = rest

    def s_ref(u, mp):
        return bufs[(u % 2) * 2 + mp]

    lq = lq_ref[...]
    lam = (jnp.exp(jnp.sum(lq[0:1] * lq[1:2], axis=-1, keepdims=True))
           - jnp.exp(jnp.sum(lq[2:3] * lq[3:4], axis=-1, keepdims=True)) + LAM_INIT)
    tq, seq = q_ref.shape[0], k_ref.shape[0]
    chunks = [(off, min(ATTN_KEY_CHUNK, seq - off), False) for off in range(0, seq, ATTN_KEY_CHUNK)]
    if has_cache:
        chunks.append((seq, ck_ref.shape[0], True))
    first = lax.broadcasted_iota(jnp.int32, (tq, LANES), 1) < HEAD_DIM
    neg = jnp.full((tq, LANES), -jnp.inf, F32)
    heads_state = [dict() for _ in range(heads)]

    def lanes(u):
        return slice(u * LANES, (u + 1) * LANES)

    def load(main_ref, cache_ref, u, chunk):
        off, size, cached = chunk
        if cached:
            return cache_ref[:, lanes(u)].astype(BF16)
        return main_ref[off:off + size, lanes(u)]

    def scores(u, chunk):
        st = heads_state[u]
        if "q" not in st:
            q = q_ref[:, lanes(u)]
            zero = jnp.zeros_like(q)
            st["q"] = (jnp.where(first, q, zero), jnp.where(first, zero, q))
            st["macc"] = [neg, neg]
        kk = load(k_ref, ck_ref if has_cache else None, u, chunk)
        off, size, _ = chunk
        for mp in range(2):
            s = _dot_nt(st["q"][mp], kk)
            s_ref(u, mp)[:, off:off + size] = s
            st["macc"][mp] = functools.reduce(jnp.maximum, _lane_groups(s), st["macc"][mp])

    def exps(u, chunk):
        st = heads_state[u]
        if "m" not in st:
            st["m"] = [jnp.max(a, axis=-1, keepdims=True) for a in st["macc"]]
            st["lacc"] = [jnp.zeros((tq, LANES), F32)] * 2
        off, size, _ = chunk
        for mp in range(2):
            e = jnp.exp2(s_ref(u, mp)[:, off:off + size] - st["m"][mp])
            s_ref(u, mp)[:, off:off + size] = e
            st["lacc"][mp] = functools.reduce(jnp.add, _lane_groups(e), st["lacc"][mp])

    def values(u, chunk):
        st = heads_state[u]
        if "r" not in st:
            l1, l2 = [jnp.sum(a, axis=-1, keepdims=True) for a in st["lacc"]]
            st["r"] = (1.0 / l1, lam / l2)
            st["o"] = jnp.zeros((tq, LANES), F32)
        off, size, _ = chunk
        w = s_ref(u, 0)[:, off:off + size] * st["r"][0] - s_ref(u, 1)[:, off:off + size] * st["r"][1]
        st["o"] = st["o"] + _dot(w.astype(BF16), load(v_ref, cv_ref if has_cache else None, u, chunk))

    stages = (scores, exps, values)
    for phase in range(heads + len(stages) - 1):
        for chunk in chunks:
            for s in reversed(range(len(stages))):
                u = phase - s
                if 0 <= u < heads:
                    stages[s](u, chunk)
        u = phase - (len(stages) - 1)
        if 0 <= u < heads:
            o_ref[:, lanes(u)] = (_rms(heads_state[u]["o"], sg_ref[...]) * (1.0 - LAM_INIT)).astype(BF16)


def _attention(q, k, v, lambda_qk, subln_g, row0, batch, seq_len, heads, cache=None):
    tq = ROW_TILE
    q_tiles = seq_len // tq
    assert row0 % seq_len == 0
    seq0 = row0 // seq_len
    width = heads * LANES
    head_q = pl.BlockSpec((tq, width), lambda b, h, i: (seq0 * q_tiles + b * q_tiles + i, h))
    head_kv = pl.BlockSpec((seq_len, width), lambda b, h, i: (seq0 + b, h))
    in_specs = [pl.BlockSpec((4, HEAD_DIM), lambda b, h, i: (0, 0)),
                pl.BlockSpec((1, 2 * HEAD_DIM), lambda b, h, i: (0, 0)),
                head_q, head_kv, head_kv]
    args = [lambda_qk, subln_g, q, k, v]
    n_keys = seq_len
    if cache is not None:
        past = cache[0].shape[1]
        n_keys += past
        head_cache = pl.BlockSpec((None, past, width), lambda b, h, i: (b, 0, h))
        in_specs += [head_cache, head_cache]
        args += list(cache)
    return pl.pallas_call(
        functools.partial(_attn_kernel, cache is not None, heads),
        grid=(batch, N_HEADS // heads, q_tiles),
        in_specs=in_specs,
        out_specs=pl.BlockSpec((tq, width), lambda b, h, i: (b * q_tiles + i, h)),
        out_shape=jax.ShapeDtypeStruct((batch * seq_len, ATTN_W), BF16),
        scratch_shapes=[pltpu.VMEM((tq, n_keys), F32)] * 4,
        compiler_params=_cparams(("parallel", "parallel", "arbitrary")),
        name="attn_latent" if cache is not None else "attn_ctx",
    )(*args)


def _merge_kernel(rows, xp_ref, xs_ref, onp_ref, ons_ref, z_ref, zp_ref, zn_ref, gt_ref, mod_ref, cw_ref, cb_ref,
                  lg_ref, lb_ref, wco_ref, wap_ref, wout_ref, n2_ref, x1_ref, h2_ref, zext_ref):
    i = pl.program_id(0)
    is_ctx = i < rows.tiles_ctx
    tm = ROW_TILE
    t, per = rows.seq_tile(i)
    zext_ref[HALO:HALO + tm, :] = z_ref[...]
    zext_ref[0:HALO, :] = jnp.where(t == 0, 0.0, zp_ref[...])
    zext_ref[HALO + tm:, :] = jnp.where(t == per - 1, 0.0, zn_ref[...])
    pad = HALO - CONV_K // 2
    col_blocks = []
    for cb in range(CONV_W // LANES):
        cs = slice(cb * LANES, (cb + 1) * LANES)
        row_chunks = []
        for r0 in range(0, tm, CONV_ROWS):
            acc = jnp.zeros((CONV_ROWS, LANES), F32) + cb_ref[:, cs]
            for j in range(CONV_K):
                acc = acc + zext_ref[r0 + j + pad:r0 + j + pad + CONV_ROWS, cs] * cw_ref[j:j + 1, cs]
            row_chunks.append(acc)
        col_blocks.append(jnp.concatenate(row_chunks, axis=0))
    c = jnp.concatenate(col_blocks, axis=1)
    mu = jnp.mean(c, axis=-1, keepdims=True)
    cc = c - mu
    y = cc * lax.rsqrt(jnp.mean(cc * cc, axis=-1, keepdims=True) + EPS) * lg_ref[...] + lb_ref[...]
    conv_out = _dot(_silu(y).astype(BF16), wco_ref[...])
    o_n = jnp.where(is_ctx, onp_ref[...], ons_ref[...])
    a_br = _dot(o_n, wap_ref[...])
    gt = gt_ref[...]
    merged = gt[:, :D_MODEL] * a_br + gt[:, D_MODEL:] * conv_out
    mod = mod_ref[0]
    gate1 = mod[:, 2 * D_MODEL:3 * D_MODEL]
    shift2, scale2 = mod[:, 3 * D_MODEL:4 * D_MODEL], mod[:, 4 * D_MODEL:5 * D_MODEL]
    x = jnp.where(is_ctx, xp_ref[...], xs_ref[...])
    x1 = x + gate1 * _dot(merged.astype(BF16), wout_ref[...])
    x1_ref[...] = x1
    h2_ref[...] = (_rms(x1, n2_ref[...]) * (1.0 + scale2) + shift2).astype(BF16)


def _merge(rows, xp, xs, on_p, on_s, z, gt, mod3, conv_w, conv_b, ln_g, ln_b, wco_b, wap_b, wout_b, norm2_g):
    n_rows = rows.tiles * ROW_TILE
    hb = ROW_TILE // HALO
    n_halo_blocks = n_rows // HALO
    in_specs = [_row_spec(D_MODEL, rows.ctx_tile), _row_spec(D_MODEL, rows.lat_tile),
                _row_spec(ATTN_W, rows.ctx_tile), _row_spec(ATTN_W, rows.lat_tile),
                _row_spec(CONV_W),
                pl.BlockSpec((HALO, CONV_W), lambda i: (jnp.maximum(i * hb - 1, 0), 0)),
                pl.BlockSpec((HALO, CONV_W), lambda i: (jnp.minimum((i + 1) * hb, n_halo_blocks - 1), 0)),
                _row_spec(2 * D_MODEL), _mod_spec(rows),
                _const_spec((CONV_K, CONV_W)), _const_spec((1, CONV_W)), _const_spec((1, CONV_W)),
                _const_spec((1, CONV_W)), _const_spec((CONV_W, D_MODEL)), _const_spec((ATTN_W, D_MODEL)),
                _const_spec((D_MODEL, D_MODEL)), _const_spec((1, D_MODEL))]
    return pl.pallas_call(
        functools.partial(_merge_kernel, rows),
        grid=(rows.tiles,),
        in_specs=in_specs,
        out_specs=[_row_spec(D_MODEL), _row_spec(D_MODEL)],
        out_shape=[jax.ShapeDtypeStruct((n_rows, D_MODEL), F32), jax.ShapeDtypeStruct((n_rows, D_MODEL), BF16)],
        scratch_shapes=[pltpu.VMEM((ROW_TILE + 2 * HALO, CONV_W), F32)],
        compiler_params=_cparams(("arbitrary",)),
        name="conv_merge",
    )(xp, xs, on_p, on_s, z, z, z, gt, mod3, conv_w, conv_b, ln_g, ln_b, wco_b, wap_b, wout_b, norm2_g)


def _pack_cols(cols, shape):
    lane = lax.broadcasted_iota(jnp.int32, shape, 1)
    out = jnp.zeros(shape, F32)
    for j, col in enumerate(cols):
        out = jnp.where(lane == j, col, out)
    return out


def _router_kernel(h_ref, wr_ref, rb_ref, gate_ref, loc_ref, rec_ref, cnt_ref, run_ref):
    i = pl.program_id(0)

    @pl.when(i == 0)
    def _():
        run_ref[...] = jnp.zeros_like(run_ref)

    tm = ROW_TILE
    neg = jnp.float32(-jnp.inf)
    scores = jax.nn.sigmoid(_dot(h_ref[...], wr_ref[...]))
    biased = scores + rb_ref[...]
    lane_i = lax.broadcasted_iota(jnp.int32, scores.shape, 1)
    lane = lane_i.astype(F32)
    far = jnp.float32(2 * N_EXPERTS)

    def first_argmax(v):
        m = jnp.max(v, axis=-1, keepdims=True)
        return m, jnp.min(jnp.where(v == m, lane, far), axis=-1, keepdims=True)

    in_group, gscore = [], []
    for g in range(N_GROUPS):
        inb = (lane_i >= g * GROUP_SIZE) & (lane_i < (g + 1) * GROUP_SIZE)
        v = jnp.where(inb, biased, neg)
        m1, i1 = first_argmax(v)
        m2 = jnp.max(jnp.where(lane == i1, neg, v), axis=-1, keepdims=True)
        in_group.append(inb)
        gscore.append(m1 + m2)
    allowed = jnp.zeros(scores.shape, F32)
    for g in range(N_GROUPS):
        ahead = jnp.zeros((tm, 1), F32)
        for g2 in range(N_GROUPS):
            if g2 < g:
                ahead = ahead + (gscore[g2] >= gscore[g]).astype(F32)
            elif g2 > g:
                ahead = ahead + (gscore[g2] > gscore[g]).astype(F32)
        keep = (ahead < TOPK_GROUPS).astype(F32)
        allowed = jnp.where(in_group[g], keep, allowed)
    masked = jnp.where(allowed > 0.0, biased, neg)
    picked = jnp.zeros(scores.shape, F32)
    idxs, gates = [], []
    for _ in range(TOP_K):
        _, ik = first_argmax(masked)
        hit = lane == ik
        gates.append(jnp.sum(jnp.where(hit, scores, 0.0), axis=-1, keepdims=True))
        masked = jnp.where(hit, neg, masked)
        picked = jnp.where(hit, 1.0, picked)
        idxs.append(ik)
    gsum = functools.reduce(jnp.add, gates)
    gates = [g / gsum * ROUTE_SCALE for g in gates]
    r_i = lax.broadcasted_iota(jnp.int32, (tm, tm), 0)
    c_i = lax.broadcasted_iota(jnp.int32, (tm, tm), 1)
    before = (c_i < r_i).astype(BF16)
    local_rank = _dot(before, picked.astype(BF16))
    n_tok = jnp.sum(picked, axis=0, keepdims=True)
    n_chunks = jnp.floor((n_tok + (CHUNK - 1)) * (1.0 / CHUNK))
    lower = (lax.broadcasted_iota(jnp.int32, (N_EXPERTS, N_EXPERTS), 0)
             < lax.broadcasted_iota(jnp.int32, (N_EXPERTS, N_EXPERTS), 1)).astype(BF16)
    chunks_before = _dot(jnp.broadcast_to(n_chunks, (SUBLANES, N_EXPERTS)).astype(BF16), lower)[0:1]
    staged_off = chunks_before * CHUNK
    staged = local_rank + staged_off
    locs = [jnp.sum(jnp.where(lane == ik, staged, 0.0), axis=-1, keepdims=True) for ik in idxs]
    e_r = lax.broadcasted_iota(jnp.int32, (N_EXPERTS, N_EXPERTS), 0)
    e_c = lax.broadcasted_iota(jnp.int32, (N_EXPERTS, N_EXPERTS), 1)
    column = lambda v: jnp.sum(jnp.where(e_r == e_c, v, 0.0), axis=1, keepdims=True)
    first = column(chunks_before)
    last = first + column(n_chunks)
    region_off = column(run_ref[...] - staged_off)
    j = lax.broadcasted_iota(jnp.int32, (N_EXPERTS, REC_CHUNKS), 1).astype(F32)
    owns = (first <= j) & (j < last)
    expert = lax.broadcasted_iota(jnp.int32, (N_EXPERTS, REC_CHUNKS), 0).astype(F32)
    chunk_expert = jnp.sum(jnp.where(owns, expert, 0.0), axis=0, keepdims=True)
    chunk_off = jnp.sum(jnp.where(owns, region_off, 0.0), axis=0, keepdims=True) + j[0:1] * CHUNK
    total = jnp.broadcast_to(jnp.sum(n_chunks, axis=1, keepdims=True), (1, REC - 2 * REC_CHUNKS))
    rec_ref[0] = jnp.concatenate([chunk_expert, chunk_off, total], axis=1).astype(jnp.int32)
    run_ref[...] = run_ref[...] + jnp.floor((n_tok + (PAIR - 1)) * (1.0 / PAIR)) * PAIR
    cnt_ref[...] = run_ref[...]
    shape = gate_ref.shape
    gate_ref[...] = _pack_cols(gates, shape)
    loc_ref[...] = _pack_cols(locs, shape).astype(jnp.int32)


def _router(rows, h2, w_router_b, router_bias):
    n_rows = rows.tiles * ROW_TILE
    return pl.pallas_call(
        _router_kernel,
        grid=(rows.tiles,),
        in_specs=[_row_spec(D_MODEL), _const_spec((D_MODEL, N_EXPERTS)), _const_spec((1, N_EXPERTS))],
        out_specs=[_row_spec(LANES), _row_spec(LANES), pl.BlockSpec((1, 1, REC), lambda i: (i, 0, 0)),
                   _const_spec((1, N_EXPERTS))],
        out_shape=[jax.ShapeDtypeStruct((n_rows, LANES), F32), jax.ShapeDtypeStruct((n_rows, LANES), jnp.int32),
                   jax.ShapeDtypeStruct((rows.tiles, 1, REC), jnp.int32),
                   jax.ShapeDtypeStruct((1, N_EXPERTS), F32)],
        scratch_shapes=[pltpu.VMEM((1, N_EXPERTS), F32)],
        compiler_params=_cparams(("arbitrary",)),
        name="router",
    )(h2, w_router_b, router_bias)


def _load_record(i, rec_hbm, rsm, sem_i):
    cp = pltpu.make_async_copy(rec_hbm.at[pl.ds(i * REC, REC)], rsm, sem_i)
    cp.start()
    cp.wait()


def _move_chunks(ps_ref, rsm, staged_ref, sorted_hbm, sem, chunks, to_sorted):
    total = rsm[2 * REC_CHUNKS]
    spare0 = sorted_hbm.shape[0] // PACK - STAGE
    for j in chunks:
        placed_tok = jnp.where(j < total, ps_ref[rsm[j]] + rsm[REC_CHUNKS + j], spare0 + j * CHUNK)
        staged = staged_ref.at[pl.ds(j * CHUNK * PACK, CHUNK * PACK)]
        placed = sorted_hbm.at[_packed_rows(placed_tok, CHUNK, PAIR)]
        src, dst = (staged, placed) if to_sorted else (placed, staged)
        pltpu.make_async_copy(src, dst, sem).start()


def _wait_all_chunks(staged_ref, sorted_hbm, sem):
    pltpu.make_async_copy(staged_ref, sorted_hbm.at[pl.ds(0, STAGE * PACK)], sem).wait()


def _staged_block(loc, rb):
    col = lax.broadcasted_iota(jnp.int32, (loc.shape[0], MOE_BLOCK), 1) + rb * MOE_BLOCK
    return [loc[:, k:k + 1] == col for k in range(TOP_K)]


def _dispatch_kernel(pe_ref, ps_ref, c2_ref, rec_hbm, loc_ref, h_ref, xs_hbm, rsm, stage, zbuf, sem_i, sem):
    i = pl.program_id(0)
    block_rows = MOE_BLOCK * PACK

    @pl.when(i == 0)
    def _():
        zbuf[...] = jnp.zeros_like(zbuf)
        n_blocks = xs_hbm.shape[0] // block_rows
        n_used = pe_ref[N_EXPERTS - 1] // MOE_BLOCK

        def clear_block(b, carry):
            dst = xs_hbm.at[pl.ds(pl.multiple_of(b * block_rows, block_rows), block_rows)]
            pltpu.make_async_copy(zbuf, dst, sem).start()
            return carry

        def clear_padding(e, total):
            first = (ps_ref[e] + c2_ref[e]) // MOE_BLOCK
            last = pe_ref[e] // MOE_BLOCK
            lax.fori_loop(first, last, clear_block, 0)
            return total + last - first

        def clear_wait(e, carry):
            pltpu.make_async_copy(zbuf, xs_hbm.at[pl.ds(0, block_rows)], sem).wait()
            return carry

        n_cleared = lax.fori_loop(0, N_EXPERTS, clear_padding, 0)
        lax.fori_loop(n_used, n_blocks, clear_block, 0)
        lax.fori_loop(0, n_cleared + n_blocks - n_used, clear_wait, 0)

    _load_record(i, rec_hbm, rsm, sem_i)
    loc, hb = loc_ref[...], h_ref[...]
    for rb in range(STAGE // MOE_BLOCK):
        onehot = jnp.zeros((ROW_TILE, MOE_BLOCK), F32)
        for hit in _staged_block(loc, rb):
            onehot = jnp.where(hit, 1.0, onehot)
        rows = lax.dot_general(onehot.astype(BF16), hb, (((0,), (0,)), ((), ())), preferred_element_type=F32)
        _store_packed(stage, rb * MOE_BLOCK, rows)
        per_block = MOE_BLOCK // CHUNK
        _move_chunks(ps_ref, rsm, stage, xs_hbm, sem, range(rb * per_block, (rb + 1) * per_block), to_sorted=True)
    _wait_all_chunks(stage, xs_hbm, sem)


def _dispatch(rows, pad_end, pad_start, counts, rec_flat, locs, h2, n_rows_sorted):
    return pl.pallas_call(
        _dispatch_kernel,
        grid_spec=pltpu.PrefetchScalarGridSpec(
            num_scalar_prefetch=3,
            grid=(rows.tiles,),
            in_specs=[pl.BlockSpec(memory_space=pl.ANY),
                      pl.BlockSpec((ROW_TILE, LANES), lambda i, pe, ps, c2: (i, 0)),
                      pl.BlockSpec((ROW_TILE, D_MODEL), lambda i, pe, ps, c2: (i, 0))],
            out_specs=pl.BlockSpec(memory_space=pl.ANY),
            scratch_shapes=[pltpu.SMEM((REC,), jnp.int32),
                            pltpu.VMEM((STAGE * PACK, LANES), jnp.uint32),
                            pltpu.VMEM((MOE_BLOCK * PACK, LANES), jnp.uint32),
                            pltpu.SemaphoreType.DMA, pltpu.SemaphoreType.DMA]),
        out_shape=jax.ShapeDtypeStruct(((n_rows_sorted + STAGE) * PACK, LANES), jnp.uint32),
        compiler_params=_cparams(("arbitrary",)),
        name="moe_dispatch",
    )(pad_end, pad_start, counts, rec_flat, locs, h2)


def _expert_kernel(pe_ref, ps_ref, xs_hbm, w1_ref, w3_ref, w2_ref, ys_hbm, xbuf, ybuf, w1b, w3b, w2b,
                   sem_in, sem_out):
    e = pl.program_id(0)
    block0 = ps_ref[e] // MOE_BLOCK
    n_blocks = pe_ref[e] // MOE_BLOCK - block0
    block_rows = MOE_BLOCK * PACK

    def rows_of(j):
        return pl.ds(pl.multiple_of((block0 + j) * block_rows, block_rows), block_rows)

    def copy_in(j, slot):
        return pltpu.make_async_copy(xs_hbm.at[rows_of(j)], xbuf.at[slot], sem_in.at[slot])

    def copy_out(j, slot):
        return pltpu.make_async_copy(ybuf.at[slot], ys_hbm.at[rows_of(j)], sem_out.at[slot])

    copy_in(0, 0).start()
    w1b[...] = w1_ref[...].astype(BF16)
    w3b[...] = w3_ref[...].astype(BF16)
    w2b[...] = w2_ref[...].astype(BF16)

    def block(j, carry):
        slot = j % 2
        copy_in(j, slot).wait()

        @pl.when(j + 1 < n_blocks)
        def _():
            copy_in(j + 1, 1 - slot).start()

        @pl.when(j >= 2)
        def _():
            copy_out(j - 2, slot).wait()

        xb = _load_packed(xbuf.at[slot], 0, MOE_BLOCK)
        hid = (_silu(_dot(xb, w1b[...])) * _dot(xb, w3b[...])).astype(BF16)
        _store_packed(ybuf.at[slot], 0, _dot(hid, w2b[...]).astype(BF16).astype(F32))
        copy_out(j, slot).start()
        return carry

    lax.fori_loop(0, n_blocks, block, 0)

    @pl.when(n_blocks >= 2)
    def _():
        copy_out(n_blocks - 2, n_blocks % 2).wait()

    copy_out(n_blocks - 1, (n_blocks - 1) % 2).wait()


def _experts(pad_end, pad_start, xs, w1, w3, w2):
    expert_spec = lambda shape: pl.BlockSpec((None,) + shape, lambda e, pe, ps: (e, 0, 0))
    any_spec = pl.BlockSpec(memory_space=pl.ANY)
    return pl.pallas_call(
        _expert_kernel,
        grid_spec=pltpu.PrefetchScalarGridSpec(
            num_scalar_prefetch=2,
            grid=(N_EXPERTS,),
            in_specs=[any_spec, expert_spec((D_MODEL, EXPERT_HIDDEN)), expert_spec((D_MODEL, EXPERT_HIDDEN)),
                      expert_spec((EXPERT_HIDDEN, D_MODEL))],
            out_specs=any_spec,
            scratch_shapes=[pltpu.VMEM((2, MOE_BLOCK * PACK, LANES), jnp.uint32),
                            pltpu.VMEM((2, MOE_BLOCK * PACK, LANES), jnp.uint32),
                            pltpu.VMEM((D_MODEL, EXPERT_HIDDEN), BF16),
                            pltpu.VMEM((D_MODEL, EXPERT_HIDDEN), BF16),
                            pltpu.VMEM((EXPERT_HIDDEN, D_MODEL), BF16),
                            pltpu.SemaphoreType.DMA((2,)), pltpu.SemaphoreType.DMA((2,))]),
        out_shape=jax.ShapeDtypeStruct(xs.shape, jnp.uint32),
        input_output_aliases={2: 0},
        compiler_params=_cparams(("arbitrary",)),
        name="moe_experts",
    )(pad_end, pad_start, xs, w1, w3, w2)


def _combine_kernel(rows, ps_ref, rec_hbm, ys_hbm, loc_ref, x1_ref, h_ref, gate_ref, mod_ref, ws1_ref, ws3_ref,
                    ws2_ref, nf_ref, outp_ref, outs_ref, rsm, ybuf, sem_i, sem):
    i = pl.program_id(0)
    _load_record(i, rec_hbm, rsm, sem_i)
    _move_chunks(ps_ref, rsm, ybuf, ys_hbm, sem, range(STAGE // CHUNK), to_sorted=False)
    hb = h_ref[...]
    shared = _dot((_silu(_dot(hb, ws1_ref[...])) * _dot(hb, ws3_ref[...])).astype(BF16), ws2_ref[...])
    _wait_all_chunks(ybuf, ys_hbm, sem)
    loc, g = loc_ref[...], gate_ref[...]
    routed = jnp.zeros((ROW_TILE, D_MODEL), F32)
    for rb in range(STAGE // MOE_BLOCK):
        gm = jnp.zeros((ROW_TILE, MOE_BLOCK), F32)
        for k, hit in enumerate(_staged_block(loc, rb)):
            gm = jnp.where(hit, g[:, k:k + 1], gm)
        g_hi = gm.astype(BF16)
        g_lo = (gm - g_hi.astype(F32)).astype(BF16)
        yb = _load_packed(ybuf, rb * MOE_BLOCK, MOE_BLOCK)
        routed = routed + _dot(g_hi, yb) + _dot(g_lo, yb)
    gate2 = mod_ref[0][:, 5 * D_MODEL:6 * D_MODEL]
    x2 = x1_ref[...] + gate2 * (routed + shared)
    out = _rms(x2, nf_ref[...])

    @pl.when(i < rows.tiles_ctx)
    def _():
        outp_ref[...] = out

    @pl.when(i >= rows.tiles_ctx)
    def _():
        outs_ref[...] = out


def _combine(rows, pad_start, rec_flat, ys, locs, x1, h2, gates, mod3, ws1_b, ws3_b, ws2_b, normf_g):
    drop = lambda spec: pl.BlockSpec(spec.block_shape, lambda i, ps, f=spec.index_map: f(i))
    any_spec = pl.BlockSpec(memory_space=pl.ANY)
    return pl.pallas_call(
        functools.partial(_combine_kernel, rows),
        grid_spec=pltpu.PrefetchScalarGridSpec(
            num_scalar_prefetch=1,
            grid=(rows.tiles,),
            in_specs=[any_spec, any_spec]
                     + [drop(s) for s in (
                         _row_spec(LANES), _row_spec(D_MODEL), _row_spec(D_MODEL), _row_spec(LANES),
                         _mod_spec(rows),
                         _const_spec((D_MODEL, EXPERT_HIDDEN)), _const_spec((D_MODEL, EXPERT_HIDDEN)),
                         _const_spec((EXPERT_HIDDEN, D_MODEL)), _const_spec((1, D_MODEL)))],
            out_specs=[drop(_row_spec(D_MODEL, rows.ctx_tile)), drop(_row_spec(D_MODEL, rows.lat_tile))],
            scratch_shapes=[pltpu.SMEM((REC,), jnp.int32),
                            pltpu.VMEM((STAGE * PACK, LANES), jnp.uint32),
                            pltpu.SemaphoreType.DMA, pltpu.SemaphoreType.DMA]),
        out_shape=[jax.ShapeDtypeStruct((rows.tiles_ctx * ROW_TILE, D_MODEL), F32),
                   jax.ShapeDtypeStruct((rows.tiles_lat * ROW_TILE, D_MODEL), F32)],
        compiler_params=_cparams(("arbitrary",)),
        name="moe_combine",
    )(pad_start, rec_flat, ys, locs, x1, h2, gates, mod3, ws1_b, ws3_b, ws2_b, normf_g)


def kernel(x_prompt, x_sample, cache_k, cache_v, c, c_ctx, w_mod, b_mod, norm1_g, w_in, lambda_qk, subln_g,
           w_attn_proj, conv_w, conv_b, conv_ln_g, conv_ln_b, w_conv_out, w_out, norm2_g, w_router,
           router_bias, w1, w3, w2, ws1, ws3, ws2, normf_g):
    batch, seq, _ = x_prompt.shape
    dec_batch, dec_seq, _ = x_sample.shape
    past = cache_k.shape[2]
    l = 0
    rows_p, rows_s = batch * seq, dec_batch * dec_seq
    n_rows = rows_p + rows_s
    rows = Rows(rows_p // ROW_TILE, rows_s // ROW_TILE, seq // ROW_TILE, dec_seq // ROW_TILE)
    xp = x_prompt.reshape(rows_p, D_MODEL)
    xs = x_sample.reshape(rows_s, D_MODEL)
    row = lambda a: a.reshape(1, -1)

    cc = jnp.zeros((MOD_ROWS, D_MODEL), F32).at[0].set(c_ctx).at[1:1 + dec_batch].set(c)
    mod3 = _mod_table(cc, w_mod[l], b_mod[l]).reshape(MOD_ROWS, 1, N_MOD * D_MODEL)

    q, k, v, z, gt, state_k, state_v = _inproj(rows, xp, xs, mod3, row(norm1_g[l]), w_in[l].astype(BF16),
                                               _rope_tables(dec_seq))
    lq, sg = lambda_qk[l], row(subln_g[l])
    on_p = _attention(q, k, v, lq, sg, 0, batch, seq, CTX_HEADS_PER_STEP)
    cache = (cache_k[:, l].reshape(dec_batch, past, ATTN_W), cache_v[:, l].reshape(dec_batch, past, ATTN_W))
    on_s = _attention(q, k, v, lq, sg, rows_p, dec_batch, dec_seq, LAT_HEADS_PER_STEP, cache=cache)
    x1, h2 = _merge(rows, xp, xs, on_p, on_s, z, gt, mod3, conv_w[l], row(conv_b[l]), row(conv_ln_g[l]),
                    row(conv_ln_b[l]), w_conv_out[l].astype(BF16), w_attn_proj[l].astype(BF16),
                    w_out[l].astype(BF16), row(norm2_g[l]))

    gates, locs, rec, counts = _router(rows, h2, w_router[l].astype(BF16), row(router_bias[l]))
    counts = counts[0].astype(jnp.int32)
    padded = (counts + CHUNK + MOE_BLOCK - 1) // MOE_BLOCK * MOE_BLOCK
    pad_end = jnp.cumsum(padded).astype(jnp.int32)
    pad_start = pad_end - padded
    max_rows = n_rows * TOP_K + rows.tiles * N_EXPERTS * (PAIR - 1) + N_EXPERTS * (CHUNK + MOE_BLOCK - 1)
    n_blocks = -(-max_rows // MOE_BLOCK)
    rec_flat = rec.reshape(-1)
    x_sorted = _dispatch(rows, pad_end, pad_start, counts, rec_flat, locs, h2, n_blocks * MOE_BLOCK)
    y_sorted = _experts(pad_end, pad_start, x_sorted, w1[l], w3[l], w2[l])
    y_p, y_s = _combine(rows, pad_start, rec_flat, y_sorted, locs, x1, h2, gates, mod3,
                        ws1[l].astype(BF16), ws3[l].astype(BF16), ws2[l].astype(BF16), row(normf_g))
    return (y_p.reshape(batch, seq, D_MODEL), y_s.reshape(dec_batch, dec_seq, D_MODEL),
            state_k.reshape(batch, 1, seq, N_HEADS, 2, HEAD_DIM),
            state_v.reshape(batch, 1, seq, N_HEADS, 2 * HEAD_DIM))
```

```python
import functools
import math
from typing import NamedTuple

import jax
import jax.numpy as jnp
from jax import lax
from jax.experimental import pallas as pl
from jax.experimental.pallas import tpu as pltpu

D_MODEL = 1024
GRID_W = 64
N_HEADS = 8
HEAD_DIM = 64
ATTN_W = N_HEADS * 2 * HEAD_DIM
CONV_W = 512
CONV_K = 31
N_EXPERTS = 256
TOP_K = 8
N_GROUPS = 8
TOPK_GROUPS = 4
GROUP_SIZE = N_EXPERTS // N_GROUPS
EXPERT_HIDDEN = 256
ROUTE_SCALE = 2.5
ROPE_BASE = 10000.0
EPS = 1e-6
N_MOD = 6
IN_COLS = 3 * ATTN_W + 2 * CONV_W + 2 * D_MODEL
LAM_INIT = 0.8 - 0.6 * math.exp(-0.3 * 0)
LOG2E = math.log2(math.e)

LANES = 128
SUBLANES = 8
VMEM_LIMIT = 56 * 1024 * 1024
HALO = 16
CONV_ROWS = 64
ROW_TILE = 256
CTX_HEADS_PER_STEP = 8
LAT_HEADS_PER_STEP = 2
ATTN_KEY_CHUNK = 512
MOE_BLOCK = 256
EXPERT_AHEAD = 3
EXPERT_RING = EXPERT_AHEAD + 1
MOD_ROWS = 8
MOD_COL_TILES = 4
PACK = D_MODEL // 2 // LANES
CHUNK = 4
PAIR = SUBLANES // PACK
STAGE = ROW_TILE * TOP_K + N_EXPERTS * (CHUNK - 1)
REC_CHUNKS = 768
REC = 2048
assert STAGE // CHUNK <= REC_CHUNKS and REC_CHUNKS % LANES == 0

BF16 = jnp.bfloat16
F32 = jnp.float32


class Rows(NamedTuple):
    tiles_ctx: int
    tiles_lat: int
    tiles_per_ctx: int
    tiles_per_lat: int

    @property
    def tiles(self):
        return self.tiles_ctx + self.tiles_lat

    def ctx_tile(self, i):
        return jnp.minimum(i, self.tiles_ctx - 1)

    def lat_tile(self, i):
        return jnp.maximum(i - self.tiles_ctx, 0)

    def mod_row(self, i):
        return jnp.where(i < self.tiles_ctx, 0, 1 + self.lat_tile(i) // self.tiles_per_lat)

    def seq_tile(self, i):
        is_ctx = i < self.tiles_ctx
        per = jnp.where(is_ctx, self.tiles_per_ctx, self.tiles_per_lat)
        return jnp.where(is_ctx, i % self.tiles_per_ctx, self.lat_tile(i) % self.tiles_per_lat), per


def _cparams(sem):
    return pltpu.CompilerParams(dimension_semantics=sem, vmem_limit_bytes=VMEM_LIMIT)


def _row_spec(width, tile=lambda i: i):
    return pl.BlockSpec((ROW_TILE, width), lambda i: (tile(i), 0))


def _const_spec(shape):
    return pl.BlockSpec(shape, lambda i: (0,) * len(shape))


def _mod_spec(rows):
    return pl.BlockSpec((1, 1, N_MOD * D_MODEL), lambda i: (rows.mod_row(i), 0, 0))


def _silu(x):
    return x * jax.nn.sigmoid(x)


def _dot(a, b):
    return jnp.dot(a, b, preferred_element_type=F32)


def _dot_nt(a, b):
    return lax.dot_general(a, b, (((1,), (1,)), ((), ())), preferred_element_type=F32)


def _rms(x, g):
    return x * lax.rsqrt(jnp.mean(x * x, axis=-1, keepdims=True) + EPS) * g


def _packed_rows(token, n_tokens, align):
    return pl.ds(pl.multiple_of(token * PACK, align * PACK), n_tokens * PACK)


def _store_packed(ref, token0, x):
    half = D_MODEL // 2
    lo = pltpu.bitcast(x[:, :half], jnp.uint32) >> 16
    hi = pltpu.bitcast(x[:, half:], jnp.uint32) & jnp.uint32(0xFFFF0000)
    w = lo | hi
    for s in range(PACK):
        ref[pl.ds(token0 * PACK + s, x.shape[0], stride=PACK), :] = w[:, s * LANES:(s + 1) * LANES]


def _load_packed(ref, token0, n_tokens):
    w = jnp.concatenate([ref[pl.ds(token0 * PACK + s, n_tokens, stride=PACK), :] for s in range(PACK)], axis=1)
    lo = pltpu.bitcast(w << 16, F32)
    hi = pltpu.bitcast(w & jnp.uint32(0xFFFF0000), F32)
    return jnp.concatenate([lo, hi], axis=1).astype(BF16)


def _mod_kernel(c_ref, w_ref, b_ref, o_ref):
    a = _silu(c_ref[...]).astype(BF16)
    o_ref[...] = _dot(a, w_ref[...].astype(BF16)) + b_ref[...]


def _mod_table(cc, w_mod, b_mod):
    n = w_mod.shape[1]
    tn = n // MOD_COL_TILES
    return pl.pallas_call(
        _mod_kernel,
        grid=(MOD_COL_TILES,),
        in_specs=[pl.BlockSpec((MOD_ROWS, D_MODEL), lambda j: (0, 0)),
                  pl.BlockSpec((D_MODEL, tn), lambda j: (0, j)),
                  pl.BlockSpec((1, tn), lambda j: (0, j))],
        out_specs=pl.BlockSpec((MOD_ROWS, tn), lambda j: (0, j)),
        out_shape=jax.ShapeDtypeStruct((MOD_ROWS, n), F32),
        compiler_params=_cparams(("arbitrary",)),
        name="mod_table",
    )(cc, w_mod, b_mod.reshape(1, n))


def _rope(x, cos, sin, lane_lo):
    swapped = jnp.where(lane_lo, pltpu.roll(x, LANES - 16, axis=1), pltpu.roll(x, 16, axis=1))
    return x * cos + swapped * sin


def _inproj_kernel(rows, xp_ref, xs_ref, mod_ref, g_ref, w_ref, cos_ref, sin_ref,
                   q_ref, k_ref, v_ref, z_ref, gt_ref, ks_ref, vs_ref):
    is_ctx = pl.program_id(0) < rows.tiles_ctx
    mod = mod_ref[0]
    shift, scale = mod[:, 0:D_MODEL], mod[:, D_MODEL:2 * D_MODEL]
    x = jnp.where(is_ctx, xp_ref[...], xs_ref[...])
    h = _rms(x, g_ref[...]) * (1.0 + scale) + shift
    hb = h.astype(BF16)
    q = _dot(hb, w_ref[:, 0:ATTN_W]) * (HEAD_DIM ** -0.5 * LOG2E)
    k = _dot(hb, w_ref[:, ATTN_W:2 * ATTN_W])
    v = _dot(hb, w_ref[:, 2 * ATTN_W:3 * ATTN_W])

    @pl.when(is_ctx)
    def _():
        q_ref[...] = q.astype(BF16)
        k_ref[...] = k.astype(BF16)
        ks_ref[...] = k
        vs_ref[...] = v

    @pl.when(jnp.logical_not(is_ctx))
    def _():
        cos, sin = cos_ref[...], sin_ref[...]
        lane_lo = (lax.broadcasted_iota(jnp.int32, cos.shape, 1) % 32) < 16
        for j in range(ATTN_W // LANES):
            sl = slice(j * LANES, (j + 1) * LANES)
            q_ref[:, sl] = _rope(q[:, sl], cos, sin, lane_lo).astype(BF16)
            k_ref[:, sl] = _rope(k[:, sl], cos, sin, lane_lo).astype(BF16)

    v_ref[...] = v.astype(BF16)
    u = _dot(hb, w_ref[:, 3 * ATTN_W:3 * ATTN_W + 2 * CONV_W])
    z_ref[...] = u[:, :CONV_W] * jax.nn.sigmoid(u[:, CONV_W:])
    gt_ref[...] = jax.nn.sigmoid(_dot(hb, w_ref[:, 3 * ATTN_W + 2 * CONV_W:]))


def _inproj(rows, xp, xs, mod3, norm_g, w_in_b, rope):
    n_rows = rows.tiles * ROW_TILE
    rows_p = rows.tiles_ctx * ROW_TILE
    rope_spec = pl.BlockSpec((ROW_TILE, LANES), lambda i: (rows.lat_tile(i) % rows.tiles_per_lat, 0))
    shapes = lambda w, dt, r=n_rows: jax.ShapeDtypeStruct((r, w), dt)
    return pl.pallas_call(
        functools.partial(_inproj_kernel, rows),
        grid=(rows.tiles,),
        in_specs=[_row_spec(D_MODEL, rows.ctx_tile), _row_spec(D_MODEL, rows.lat_tile), _mod_spec(rows),
                  _const_spec((1, D_MODEL)),
                  pl.BlockSpec((D_MODEL, IN_COLS), lambda i: (0, 0), pipeline_mode=pl.Buffered(1)),
                  rope_spec, rope_spec],
        out_specs=[_row_spec(ATTN_W)] * 3 + [_row_spec(CONV_W), _row_spec(2 * D_MODEL)]
                  + [_row_spec(ATTN_W, rows.ctx_tile)] * 2,
        out_shape=[shapes(ATTN_W, BF16)] * 3 + [shapes(CONV_W, F32), shapes(2 * D_MODEL, F32)]
                  + [shapes(ATTN_W, F32, rows_p)] * 2,
        compiler_params=_cparams(("arbitrary",)),
        name="inproj",
    )(xp, xs, mod3, norm_g, w_in_b, *rope)


def _rope_tables(n_tokens):
    t = jnp.arange(n_tokens, dtype=jnp.int32)
    pos = jnp.stack([t // GRID_W, t % GRID_W], axis=-1).astype(F32)
    half = HEAD_DIM // 2
    inv = ROPE_BASE ** (-jnp.arange(0, half, 2, dtype=F32) / half)
    ang = pos[:, :, None] * inv
    cos, sin = jnp.cos(ang), jnp.sin(ang)
    cos64 = jnp.concatenate([cos[:, 0], cos[:, 0], cos[:, 1], cos[:, 1]], axis=-1)
    sin64 = jnp.concatenate([-sin[:, 0], sin[:, 0], -sin[:, 1], sin[:, 1]], axis=-1)
    return jnp.tile(cos64, (1, LANES // HEAD_DIM)), jnp.tile(sin64, (1, LANES // HEAD_DIM))


def _lane_groups(x):
    return [x[:, j * LANES:(j + 1) * LANES] for j in range(x.shape[1] // LANES)]


def _attn_kernel(has_cache, heads, lq_ref, sg_ref, q_ref, k_ref, v_ref, *rest):
    if has_cache:
        ck_ref, cv_ref, o_ref, *bufs = rest
    else:
        o_ref, *bufs = rest

    def s_ref(u, mp):
        return bufs[(u % 2) * 2 + mp]

    lq = lq_ref[...]
    lam = (jnp.exp(jnp.sum(lq[0:1] * lq[1:2], axis=-1, keepdims=True))
           - jnp.exp(jnp.sum(lq[2:3] * lq[3:4], axis=-1, keepdims=True)) + LAM_INIT)
    tq, seq = q_ref.shape[0], k_ref.shape[0]
    chunks = [(off, min(ATTN_KEY_CHUNK, seq - off), False) for off in range(0, seq, ATTN_KEY_CHUNK)]
    if has_cache:
        chunks.append((seq, ck_ref.shape[0], True))
    first = lax.broadcasted_iota(jnp.int32, (tq, LANES), 1) < HEAD_DIM
    neg = jnp.full((tq, LANES), -jnp.inf, F32)
    heads_state = [dict() for _ in range(heads)]

    def lanes(u):
        return slice(u * LANES, (u + 1) * LANES)

    def load(main_ref, cache_ref, u, chunk):
        off, size, cached = chunk
        if cached:
            return cache_ref[:, lanes(u)].astype(BF16)
        return main_ref[off:off + size, lanes(u)]

    def scores(u, chunk):
        st = heads_state[u]
        if "q" not in st:
            q = q_ref[:, lanes(u)]
            zero = jnp.zeros_like(q)
            st["q"] = (jnp.where(first, q, zero), jnp.where(first, zero, q))
            st["macc"] = [neg, neg]
        kk = load(k_ref, ck_ref if has_cache else None, u, chunk)
        off, size, _ = chunk
        for mp in range(2):
            s = _dot_nt(st["q"][mp], kk)
            s_ref(u, mp)[:, off:off + size] = s
            st["macc"][mp] = functools.reduce(jnp.maximum, _lane_groups(s), st["macc"][mp])

    def exps(u, chunk):
        st = heads_state[u]
        if "m" not in st:
            st["m"] = [jnp.max(a, axis=-1, keepdims=True) for a in st["macc"]]
            st["lacc"] = [jnp.zeros((tq, LANES), F32)] * 2
        off, size, _ = chunk
        for mp in range(2):
            e = jnp.exp2(s_ref(u, mp)[:, off:off + size] - st["m"][mp])
            s_ref(u, mp)[:, off:off + size] = e
            st["lacc"][mp] = functools.reduce(jnp.add, _lane_groups(e), st["lacc"][mp])

    def values(u, chunk):
        st = heads_state[u]
        if "r" not in st:
            l1, l2 = [jnp.sum(a, axis=-1, keepdims=True) for a in st["lacc"]]
            st["r"] = (1.0 / l1, lam / l2)
            st["o"] = jnp.zeros((tq, LANES), F32)
        off, size, _ = chunk
        w = s_ref(u, 0)[:, off:off + size] * st["r"][0] - s_ref(u, 1)[:, off:off + size] * st["r"][1]
        st["o"] = st["o"] + _dot(w.astype(BF16), load(v_ref, cv_ref if has_cache else None, u, chunk))

    stages = (scores, exps, values)
    for phase in range(heads + len(stages) - 1):
        for chunk in chunks:
            for s in reversed(range(len(stages))):
                u = phase - s
                if 0 <= u < heads:
                    stages[s](u, chunk)
        u = phase - (len(stages) - 1)
        if 0 <= u < heads:
            o_ref[:, lanes(u)] = (_rms(heads_state[u]["o"], sg_ref[...]) * (1.0 - LAM_INIT)).astype(BF16)


def _attention(q, k, v, lambda_qk, subln_g, row0, batch, seq_len, heads, cache=None):
    tq = ROW_TILE
    q_tiles = seq_len // tq
    assert row0 % seq_len == 0
    seq0 = row0 // seq_len
    width = heads * LANES
    head_q = pl.BlockSpec((tq, width), lambda b, h, i: (seq0 * q_tiles + b * q_tiles + i, h))
    head_kv = pl.BlockSpec((seq_len, width), lambda b, h, i: (seq0 + b, h))
    in_specs = [pl.BlockSpec((4, HEAD_DIM), lambda b, h, i: (0, 0)),
                pl.BlockSpec((1, 2 * HEAD_DIM), lambda b, h, i: (0, 0)),
                head_q, head_kv, head_kv]
    args = [lambda_qk, subln_g, q, k, v]
    n_keys = seq_len
    if cache is not None:
        past = cache[0].shape[1]
        n_keys += past
        head_cache = pl.BlockSpec((None, past, width), lambda b, h, i: (b, 0, h))
        in_specs += [head_cache, head_cache]
        args += list(cache)
    return pl.pallas_call(
        functools.partial(_attn_kernel, cache is not None, heads),
        grid=(batch, N_HEADS // heads, q_tiles),
        in_specs=in_specs,
        out_specs=pl.BlockSpec((tq, width), lambda b, h, i: (b * q_tiles + i, h)),
        out_shape=jax.ShapeDtypeStruct((batch * seq_len, ATTN_W), BF16),
        scratch_shapes=[pltpu.VMEM((tq, n_keys), F32)] * 4,
        compiler_params=_cparams(("parallel", "parallel", "arbitrary")),
        name="attn_latent" if cache is not None else "attn_ctx",
    )(*args)


def _merge_kernel(rows, xp_ref, xs_ref, onp_ref, ons_ref, z_ref, zp_ref, zn_ref, gt_ref, mod_ref, cw_ref, cb_ref,
                  lg_ref, lb_ref, wco_ref, wap_ref, wout_ref, n2_ref, x1_ref, h2_ref, zext_ref):
    i = pl.program_id(0)
    is_ctx = i < rows.tiles_ctx
    tm = ROW_TILE
    t, per = rows.seq_tile(i)
    zext_ref[HALO:HALO + tm, :] = z_ref[...]
    zext_ref[0:HALO, :] = jnp.where(t == 0, 0.0, zp_ref[...])
    zext_ref[HALO + tm:, :] = jnp.where(t == per - 1, 0.0, zn_ref[...])
    pad = HALO - CONV_K // 2
    col_blocks = []
    for cb in range(CONV_W // LANES):
        cs = slice(cb * LANES, (cb + 1) * LANES)
        row_chunks = []
        for r0 in range(0, tm, CONV_ROWS):
            acc = jnp.zeros((CONV_ROWS, LANES), F32) + cb_ref[:, cs]
            for j in range(CONV_K):
                acc = acc + zext_ref[r0 + j + pad:r0 + j + pad + CONV_ROWS, cs] * cw_ref[j:j + 1, cs]
            row_chunks.append(acc)
        col_blocks.append(jnp.concatenate(row_chunks, axis=0))
    c = jnp.concatenate(col_blocks, axis=1)
    mu = jnp.mean(c, axis=-1, keepdims=True)
    cc = c - mu
    y = cc * lax.rsqrt(jnp.mean(cc * cc, axis=-1, keepdims=True) + EPS) * lg_ref[...] + lb_ref[...]
    conv_out = _dot(_silu(y).astype(BF16), wco_ref[...])
    o_n = jnp.where(is_ctx, onp_ref[...], ons_ref[...])
    a_br = _dot(o_n, wap_ref[...])
    gt = gt_ref[...]
    merged = gt[:, :D_MODEL] * a_br + gt[:, D_MODEL:] * conv_out
    mod = mod_ref[0]
    gate1 = mod[:, 2 * D_MODEL:3 * D_MODEL]
    shift2, scale2 = mod[:, 3 * D_MODEL:4 * D_MODEL], mod[:, 4 * D_MODEL:5 * D_MODEL]
    x = jnp.where(is_ctx, xp_ref[...], xs_ref[...])
    x1 = x + gate1 * _dot(merged.astype(BF16), wout_ref[...])
    x1_ref[...] = x1
    h2_ref[...] = (_rms(x1, n2_ref[...]) * (1.0 + scale2) + shift2).astype(BF16)


def _merge(rows, xp, xs, on_p, on_s, z, gt, mod3, conv_w, conv_b, ln_g, ln_b, wco_b, wap_b, wout_b, norm2_g):
    n_rows = rows.tiles * ROW_TILE
    hb = ROW_TILE // HALO
    n_halo_blocks = n_rows // HALO
    in_specs = [_row_spec(D_MODEL, rows.ctx_tile), _row_spec(D_MODEL, rows.lat_tile),
                _row_spec(ATTN_W, rows.ctx_tile), _row_spec(ATTN_W, rows.lat_tile),
                _row_spec(CONV_W),
                pl.BlockSpec((HALO, CONV_W), lambda i: (jnp.maximum(i * hb - 1, 0), 0)),
                pl.BlockSpec((HALO, CONV_W), lambda i: (jnp.minimum((i + 1) * hb, n_halo_blocks - 1), 0)),
                _row_spec(2 * D_MODEL), _mod_spec(rows),
                _const_spec((CONV_K, CONV_W)), _const_spec((1, CONV_W)), _const_spec((1, CONV_W)),
                _const_spec((1, CONV_W)), _const_spec((CONV_W, D_MODEL)), _const_spec((ATTN_W, D_MODEL)),
                _const_spec((D_MODEL, D_MODEL)), _const_spec((1, D_MODEL))]
    return pl.pallas_call(
        functools.partial(_merge_kernel, rows),
        grid=(rows.tiles,),
        in_specs=in_specs,
        out_specs=[_row_spec(D_MODEL), _row_spec(D_MODEL)],
        out_shape=[jax.ShapeDtypeStruct((n_rows, D_MODEL), F32), jax.ShapeDtypeStruct((n_rows, D_MODEL), BF16)],
        scratch_shapes=[pltpu.VMEM((ROW_TILE + 2 * HALO, CONV_W), F32)],
        compiler_params=_cparams(("arbitrary",)),
        name="conv_merge",
    )(xp, xs, on_p, on_s, z, z, z, gt, mod3, conv_w, conv_b, ln_g, ln_b, wco_b, wap_b, wout_b, norm2_g)


def _pack_cols(cols, shape):
    lane = lax.broadcasted_iota(jnp.int32, shape, 1)
    out = jnp.zeros(shape, F32)
    for j, col in enumerate(cols):
        out = jnp.where(lane == j, col, out)
    return out


def _router_kernel(h_ref, wr_ref, rb_ref, gate_ref, loc_ref, rec_ref, cnt_ref, run_ref):
    i = pl.program_id(0)

    @pl.when(i == 0)
    def _():
        run_ref[...] = jnp.zeros_like(run_ref)

    tm = ROW_TILE
    neg = jnp.float32(-jnp.inf)
    scores = jax.nn.sigmoid(_dot(h_ref[...], wr_ref[...]))
    biased = scores + rb_ref[...]
    lane_i = lax.broadcasted_iota(jnp.int32, scores.shape, 1)
    lane = lane_i.astype(F32)
    far = jnp.float32(2 * N_EXPERTS)

    def first_argmax(v):
        m = jnp.max(v, axis=-1, keepdims=True)
        return m, jnp.min(jnp.where(v == m, lane, far), axis=-1, keepdims=True)

    in_group, gscore = [], []
    for g in range(N_GROUPS):
        inb = (lane_i >= g * GROUP_SIZE) & (lane_i < (g + 1) * GROUP_SIZE)
        v = jnp.where(inb, biased, neg)
        m1, i1 = first_argmax(v)
        m2 = jnp.max(jnp.where(lane == i1, neg, v), axis=-1, keepdims=True)
        in_group.append(inb)
        gscore.append(m1 + m2)
    allowed = jnp.zeros(scores.shape, F32)
    for g in range(N_GROUPS):
        ahead = jnp.zeros((tm, 1), F32)
        for g2 in range(N_GROUPS):
            if g2 < g:
                ahead = ahead + (gscore[g2] >= gscore[g]).astype(F32)
            elif g2 > g:
                ahead = ahead + (gscore[g2] > gscore[g]).astype(F32)
        keep = (ahead < TOPK_GROUPS).astype(F32)
        allowed = jnp.where(in_group[g], keep, allowed)
    masked = jnp.where(allowed > 0.0, biased, neg)
    picked = jnp.zeros(scores.shape, F32)
    idxs, gates = [], []
    for _ in range(TOP_K):
        _, ik = first_argmax(masked)
        hit = lane == ik
        gates.append(jnp.sum(jnp.where(hit, scores, 0.0), axis=-1, keepdims=True))
        masked = jnp.where(hit, neg, masked)
        picked = jnp.where(hit, 1.0, picked)
        idxs.append(ik)
    gsum = functools.reduce(jnp.add, gates)
    gates = [g / gsum * ROUTE_SCALE for g in gates]
    r_i = lax.broadcasted_iota(jnp.int32, (tm, tm), 0)
    c_i = lax.broadcasted_iota(jnp.int32, (tm, tm), 1)
    before = (c_i < r_i).astype(BF16)
    local_rank = _dot(before, picked.astype(BF16))
    n_tok = jnp.sum(picked, axis=0, keepdims=True)
    n_chunks = jnp.floor((n_tok + (CHUNK - 1)) * (1.0 / CHUNK))
    lower = (lax.broadcasted_iota(jnp.int32, (N_EXPERTS, N_EXPERTS), 0)
             < lax.broadcasted_iota(jnp.int32, (N_EXPERTS, N_EXPERTS), 1)).astype(BF16)
    chunks_before = _dot(jnp.broadcast_to(n_chunks, (SUBLANES, N_EXPERTS)).astype(BF16), lower)[0:1]
    staged_off = chunks_before * CHUNK
    staged = local_rank + staged_off
    locs = [jnp.sum(jnp.where(lane == ik, staged, 0.0), axis=-1, keepdims=True) for ik in idxs]
    e_r = lax.broadcasted_iota(jnp.int32, (N_EXPERTS, N_EXPERTS), 0)
    e_c = lax.broadcasted_iota(jnp.int32, (N_EXPERTS, N_EXPERTS), 1)
    column = lambda v: jnp.sum(jnp.where(e_r == e_c, v, 0.0), axis=1, keepdims=True)
    first = column(chunks_before)
    last = first + column(n_chunks)
    region_off = column(run_ref[...] - staged_off)
    j = lax.broadcasted_iota(jnp.int32, (N_EXPERTS, REC_CHUNKS), 1).astype(F32)
    owns = (first <= j) & (j < last)
    expert = lax.broadcasted_iota(jnp.int32, (N_EXPERTS, REC_CHUNKS), 0).astype(F32)
    chunk_expert = jnp.sum(jnp.where(owns, expert, 0.0), axis=0, keepdims=True)
    chunk_off = jnp.sum(jnp.where(owns, region_off, 0.0), axis=0, keepdims=True) + j[0:1] * CHUNK
    total = jnp.broadcast_to(jnp.sum(n_chunks, axis=1, keepdims=True), (1, REC - 2 * REC_CHUNKS))
    rec_ref[0] = jnp.concatenate([chunk_expert, chunk_off, total], axis=1).astype(jnp.int32)
    run_ref[...] = run_ref[...] + jnp.floor((n_tok + (PAIR - 1)) * (1.0 / PAIR)) * PAIR
    cnt_ref[...] = run_ref[...]
    shape = gate_ref.shape
    gate_ref[...] = _pack_cols(gates, shape)
    loc_ref[...] = _pack_cols(locs, shape).astype(jnp.int32)


def _router(rows, h2, w_router_b, router_bias):
    n_rows = rows.tiles * ROW_TILE
    return pl.pallas_call(
        _router_kernel,
        grid=(rows.tiles,),
        in_specs=[_row_spec(D_MODEL), _const_spec((D_MODEL, N_EXPERTS)), _const_spec((1, N_EXPERTS))],
        out_specs=[_row_spec(LANES), _row_spec(LANES), pl.BlockSpec((1, 1, REC), lambda i: (i, 0, 0)),
                   _const_spec((1, N_EXPERTS))],
        out_shape=[jax.ShapeDtypeStruct((n_rows, LANES), F32), jax.ShapeDtypeStruct((n_rows, LANES), jnp.int32),
                   jax.ShapeDtypeStruct((rows.tiles, 1, REC), jnp.int32),
                   jax.ShapeDtypeStruct((1, N_EXPERTS), F32)],
        scratch_shapes=[pltpu.VMEM((1, N_EXPERTS), F32)],
        compiler_params=_cparams(("arbitrary",)),
        name="router",
    )(h2, w_router_b, router_bias)


def _load_record(i, rec_hbm, rsm, sem_i):
    cp = pltpu.make_async_copy(rec_hbm.at[pl.ds(i * REC, REC)], rsm, sem_i)
    cp.start()
    cp.wait()


def _move_chunks(ps_ref, rsm, staged_ref, sorted_hbm, sem, chunks, to_sorted):
    total = rsm[2 * REC_CHUNKS]
    spare0 = sorted_hbm.shape[0] // PACK - STAGE
    for j in chunks:
        placed_tok = jnp.where(j < total, ps_ref[rsm[j]] + rsm[REC_CHUNKS + j], spare0 + j * CHUNK)
        staged = staged_ref.at[pl.ds(j * CHUNK * PACK, CHUNK * PACK)]
        placed = sorted_hbm.at[_packed_rows(placed_tok, CHUNK, PAIR)]
        src, dst = (staged, placed) if to_sorted else (placed, staged)
        pltpu.make_async_copy(src, dst, sem).start()


def _wait_all_chunks(staged_ref, sorted_hbm, sem):
    pltpu.make_async_copy(staged_ref, sorted_hbm.at[pl.ds(0, STAGE * PACK)], sem).wait()


def _staged_block(loc, rb):
    col = lax.broadcasted_iota(jnp.int32, (loc.shape[0], MOE_BLOCK), 1) + rb * MOE_BLOCK
    return [loc[:, k:k + 1] == col for k in range(TOP_K)]


def _dispatch_kernel(pe_ref, ps_ref, c2_ref, rec_hbm, loc_ref, h_ref, xs_hbm, rsm, stage, zbuf, sem_i, sem):
    i = pl.program_id(0)
    block_rows = MOE_BLOCK * PACK

    @pl.when(i == 0)
    def _():
        zbuf[...] = jnp.zeros_like(zbuf)
        n_blocks = xs_hbm.shape[0] // block_rows
        n_used = pe_ref[N_EXPERTS - 1] // MOE_BLOCK

        def clear_block(b, carry):
            dst = xs_hbm.at[pl.ds(pl.multiple_of(b * block_rows, block_rows), block_rows)]
            pltpu.make_async_copy(zbuf, dst, sem).start()
            return carry

        def clear_padding(e, total):
            first = (ps_ref[e] + c2_ref[e]) // MOE_BLOCK
            last = pe_ref[e] // MOE_BLOCK
            lax.fori_loop(first, last, clear_block, 0)
            return total + last - first

        def clear_wait(e, carry):
            pltpu.make_async_copy(zbuf, xs_hbm.at[pl.ds(0, block_rows)], sem).wait()
            return carry

        n_cleared = lax.fori_loop(0, N_EXPERTS, clear_padding, 0)
        lax.fori_loop(n_used, n_blocks, clear_block, 0)
        lax.fori_loop(0, n_cleared + n_blocks - n_used, clear_wait, 0)

    _load_record(i, rec_hbm, rsm, sem_i)
    loc, hb = loc_ref[...], h_ref[...]
    for rb in range(STAGE // MOE_BLOCK):
        onehot = jnp.zeros((ROW_TILE, MOE_BLOCK), F32)
        for hit in _staged_block(loc, rb):
            onehot = jnp.where(hit, 1.0, onehot)
        rows = lax.dot_general(onehot.astype(BF16), hb, (((0,), (0,)), ((), ())), preferred_element_type=F32)
        _store_packed(stage, rb * MOE_BLOCK, rows)
        per_block = MOE_BLOCK // CHUNK
        _move_chunks(ps_ref, rsm, stage, xs_hbm, sem, range(rb * per_block, (rb + 1) * per_block), to_sorted=True)
    _wait_all_chunks(stage, xs_hbm, sem)


def _dispatch(rows, pad_end, pad_start, counts, rec_flat, locs, h2, n_rows_sorted):
    return pl.pallas_call(
        _dispatch_kernel,
        grid_spec=pltpu.PrefetchScalarGridSpec(
            num_scalar_prefetch=3,
            grid=(rows.tiles,),
            in_specs=[pl.BlockSpec(memory_space=pl.ANY),
                      pl.BlockSpec((ROW_TILE, LANES), lambda i, pe, ps, c2: (i, 0)),
                      pl.BlockSpec((ROW_TILE, D_MODEL), lambda i, pe, ps, c2: (i, 0))],
            out_specs=pl.BlockSpec(memory_space=pl.ANY),
            scratch_shapes=[pltpu.SMEM((REC,), jnp.int32),
                            pltpu.VMEM((STAGE * PACK, LANES), jnp.uint32),
                            pltpu.VMEM((MOE_BLOCK * PACK, LANES), jnp.uint32),
                            pltpu.SemaphoreType.DMA, pltpu.SemaphoreType.DMA]),
        out_shape=jax.ShapeDtypeStruct(((n_rows_sorted + STAGE) * PACK, LANES), jnp.uint32),
        compiler_params=_cparams(("arbitrary",)),
        name="moe_dispatch",
    )(pad_end, pad_start, counts, rec_flat, locs, h2)


def _expert_kernel(pe_ref, ps_ref, xs_hbm, w1_ref, w3_ref, w2_ref, ys_hbm, xbuf, ybuf, w1b, w3b, w2b,
                   sem_in, sem_out):
    e = pl.program_id(0)
    first, last = ps_ref[e] // MOE_BLOCK, pe_ref[e] // MOE_BLOCK
    n_used = pe_ref[N_EXPERTS - 1] // MOE_BLOCK
    block_rows = MOE_BLOCK * PACK

    def rows_of(g):
        return pl.ds(pl.multiple_of(g * block_rows, block_rows), block_rows)

    def copy_in(g):
        slot = g % EXPERT_RING
        return pltpu.make_async_copy(xs_hbm.at[rows_of(g)], xbuf.at[slot], sem_in.at[slot])

    def copy_out(g):
        slot = g % EXPERT_RING
        return pltpu.make_async_copy(ybuf.at[slot], ys_hbm.at[rows_of(g)], sem_out.at[slot])

    @pl.when(e == 0)
    def _():
        for g in range(EXPERT_AHEAD):
            @pl.when(g < n_used)
            def _():
                copy_in(g).start()

    w1b[...] = w1_ref[...].astype(BF16)
    w3b[...] = w3_ref[...].astype(BF16)
    w2b[...] = w2_ref[...].astype(BF16)

    def block(g, carry):
        copy_in(g).wait()

        @pl.when(g + EXPERT_AHEAD < n_used)
        def _():
            copy_in(g + EXPERT_AHEAD).start()

        @pl.when(g >= EXPERT_RING)
        def _():
            copy_out(g - EXPERT_RING).wait()

        slot = g % EXPERT_RING
        xb = _load_packed(xbuf.at[slot], 0, MOE_BLOCK)
        hid = (_silu(_dot(xb, w1b[...])) * _dot(xb, w3b[...])).astype(BF16)
        _store_packed(ybuf.at[slot], 0, _dot(hid, w2b[...]).astype(BF16).astype(F32))
        copy_out(g).start()
        return carry

    lax.fori_loop(first, last, block, 0)

    @pl.when(e == N_EXPERTS - 1)
    def _():
        for back in range(EXPERT_RING, 0, -1):
            @pl.when(n_used >= back)
            def _():
                copy_out(n_used - back).wait()


def _experts(pad_end, pad_start, xs, w1, w3, w2):
    expert_spec = lambda shape: pl.BlockSpec((None,) + shape, lambda e, pe, ps: (e, 0, 0))
    any_spec = pl.BlockSpec(memory_space=pl.ANY)
    return pl.pallas_call(
        _expert_kernel,
        grid_spec=pltpu.PrefetchScalarGridSpec(
            num_scalar_prefetch=2,
            grid=(N_EXPERTS,),
            in_specs=[any_spec, expert_spec((D_MODEL, EXPERT_HIDDEN)), expert_spec((D_MODEL, EXPERT_HIDDEN)),
                      expert_spec((EXPERT_HIDDEN, D_MODEL))],
            out_specs=any_spec,
            scratch_shapes=[pltpu.VMEM((EXPERT_RING, MOE_BLOCK * PACK, LANES), jnp.uint32),
                            pltpu.VMEM((EXPERT_RING, MOE_BLOCK * PACK, LANES), jnp.uint32),
                            pltpu.VMEM((D_MODEL, EXPERT_HIDDEN), BF16),
                            pltpu.VMEM((D_MODEL, EXPERT_HIDDEN), BF16),
                            pltpu.VMEM((EXPERT_HIDDEN, D_MODEL), BF16),
                            pltpu.SemaphoreType.DMA((EXPERT_RING,)), pltpu.SemaphoreType.DMA((EXPERT_RING,))]),
        out_shape=jax.ShapeDtypeStruct(xs.shape, jnp.uint32),
        input_output_aliases={2: 0},
        compiler_params=_cparams(("arbitrary",)),
        name="moe_experts",
    )(pad_end, pad_start, xs, w1, w3, w2)


def _combine_kernel(rows, ps_ref, rec_hbm, ys_hbm, loc_ref, x1_ref, h_ref, gate_ref, mod_ref, ws1_ref, ws3_ref,
                    ws2_ref, nf_ref, outp_ref, outs_ref, rsm, ybuf, sem_i, sem):
    i = pl.program_id(0)
    _load_record(i, rec_hbm, rsm, sem_i)
    _move_chunks(ps_ref, rsm, ybuf, ys_hbm, sem, range(STAGE // CHUNK), to_sorted=False)
    hb = h_ref[...]
    shared = _dot((_silu(_dot(hb, ws1_ref[...])) * _dot(hb, ws3_ref[...])).astype(BF16), ws2_ref[...])
    _wait_all_chunks(ybuf, ys_hbm, sem)
    loc, g = loc_ref[...], gate_ref[...]
    routed = jnp.zeros((ROW_TILE, D_MODEL), F32)
    for rb in range(STAGE // MOE_BLOCK):
        gm = jnp.zeros((ROW_TILE, MOE_BLOCK), F32)
        for k, hit in enumerate(_staged_block(loc, rb)):
            gm = jnp.where(hit, g[:, k:k + 1], gm)
        g_hi = gm.astype(BF16)
        g_lo = (gm - g_hi.astype(F32)).astype(BF16)
        yb = _load_packed(ybuf, rb * MOE_BLOCK, MOE_BLOCK)
        routed = routed + _dot(g_hi, yb) + _dot(g_lo, yb)
    gate2 = mod_ref[0][:, 5 * D_MODEL:6 * D_MODEL]
    x2 = x1_ref[...] + gate2 * (routed + shared)
    out = _rms(x2, nf_ref[...])

    @pl.when(i < rows.tiles_ctx)
    def _():
        outp_ref[...] = out

    @pl.when(i >= rows.tiles_ctx)
    def _():
        outs_ref[...] = out


def _combine(rows, pad_start, rec_flat, ys, locs, x1, h2, gates, mod3, ws1_b, ws3_b, ws2_b, normf_g):
    drop = lambda spec: pl.BlockSpec(spec.block_shape, lambda i, ps, f=spec.index_map: f(i))
    any_spec = pl.BlockSpec(memory_space=pl.ANY)
    return pl.pallas_call(
        functools.partial(_combine_kernel, rows),
        grid_spec=pltpu.PrefetchScalarGridSpec(
            num_scalar_prefetch=1,
            grid=(rows.tiles,),
            in_specs=[any_spec, any_spec]
                     + [drop(s) for s in (
                         _row_spec(LANES), _row_spec(D_MODEL), _row_spec(D_MODEL), _row_spec(LANES),
                         _mod_spec(rows),
                         _const_spec((D_MODEL, EXPERT_HIDDEN)), _const_spec((D_MODEL, EXPERT_HIDDEN)),
                         _const_spec((EXPERT_HIDDEN, D_MODEL)), _const_spec((1, D_MODEL)))],
            out_specs=[drop(_row_spec(D_MODEL, rows.ctx_tile)), drop(_row_spec(D_MODEL, rows.lat_tile))],
            scratch_shapes=[pltpu.SMEM((REC,), jnp.int32),
                            pltpu.VMEM((STAGE * PACK, LANES), jnp.uint32),
                            pltpu.SemaphoreType.DMA, pltpu.SemaphoreType.DMA]),
        out_shape=[jax.ShapeDtypeStruct((rows.tiles_ctx * ROW_TILE, D_MODEL), F32),
                   jax.ShapeDtypeStruct((rows.tiles_lat * ROW_TILE, D_MODEL), F32)],
        compiler_params=_cparams(("arbitrary",)),
        name="moe_combine",
    )(pad_start, rec_flat, ys, locs, x1, h2, gates, mod3, ws1_b, ws3_b, ws2_b, normf_g)


def kernel(x_prompt, x_sample, cache_k, cache_v, c, c_ctx, w_mod, b_mod, norm1_g, w_in, lambda_qk, subln_g,
           w_attn_proj, conv_w, conv_b, conv_ln_g, conv_ln_b, w_conv_out, w_out, norm2_g, w_router,
           router_bias, w1, w3, w2, ws1, ws3, ws2, normf_g):
    batch, seq, _ = x_prompt.shape
    dec_batch, dec_seq, _ = x_sample.shape
    past = cache_k.shape[2]
    l = 0
    rows_p, rows_s = batch * seq, dec_batch * dec_seq
    n_rows = rows_p + rows_s
    rows = Rows(rows_p // ROW_TILE, rows_s // ROW_TILE, seq // ROW_TILE, dec_seq // ROW_TILE)
    xp = x_prompt.reshape(rows_p, D_MODEL)
    xs = x_sample.reshape(rows_s, D_MODEL)
    row = lambda a: a.reshape(1, -1)

    cc = jnp.zeros((MOD_ROWS, D_MODEL), F32).at[0].set(c_ctx).at[1:1 + dec_batch].set(c)
    mod3 = _mod_table(cc, w_mod[l], b_mod[l]).reshape(MOD_ROWS, 1, N_MOD * D_MODEL)

    q, k, v, z, gt, state_k, state_v = _inproj(rows, xp, xs, mod3, row(norm1_g[l]), w_in[l].astype(BF16),
                                               _rope_tables(dec_seq))
    lq, sg = lambda_qk[l], row(subln_g[l])
    on_p = _attention(q, k, v, lq, sg, 0, batch, seq, CTX_HEADS_PER_STEP)
    cache = (cache_k[:, l].reshape(dec_batch, past, ATTN_W), cache_v[:, l].reshape(dec_batch, past, ATTN_W))
    on_s = _attention(q, k, v, lq, sg, rows_p, dec_batch, dec_seq, LAT_HEADS_PER_STEP, cache=cache)
    x1, h2 = _merge(rows, xp, xs, on_p, on_s, z, gt, mod3, conv_w[l], row(conv_b[l]), row(conv_ln_g[l]),
                    row(conv_ln_b[l]), w_conv_out[l].astype(BF16), w_attn_proj[l].astype(BF16),
                    w_out[l].astype(BF16), row(norm2_g[l]))

    gates, locs, rec, counts = _router(rows, h2, w_router[l].astype(BF16), row(router_bias[l]))
    counts = counts[0].astype(jnp.int32)
    padded = (counts + CHUNK + MOE_BLOCK - 1) // MOE_BLOCK * MOE_BLOCK
    pad_end = jnp.cumsum(padded).astype(jnp.int32)
    pad_start = pad_end - padded
    max_rows = n_rows * TOP_K + rows.tiles * N_EXPERTS * (PAIR - 1) + N_EXPERTS * (CHUNK + MOE_BLOCK - 1)
    n_blocks = -(-max_rows // MOE_BLOCK)
    rec_flat = rec.reshape(-1)
    x_sorted = _dispatch(rows, pad_end, pad_start, counts, rec_flat, locs, h2, n_blocks * MOE_BLOCK)
    y_sorted = _experts(pad_end, pad_start, x_sorted, w1[l], w3[l], w2[l])
    y_p, y_s = _combine(rows, pad_start, rec_flat, y_sorted, locs, x1, h2, gates, mod3,
                        ws1[l].astype(BF16), ws3[l].astype(BF16), ws2[l].astype(BF16), row(normf_g))
    return (y_p.reshape(batch, seq, D_MODEL), y_s.reshape(dec_batch, dec_seq, D_MODEL),
            state_k.reshape(batch, 1, seq, N_HEADS, 2, HEAD_DIM),
            state_v.reshape(batch, 1, seq, N_HEADS, 2 * HEAD_DIM))
```

```python
import functools
import math
from typing import NamedTuple

import jax
import jax.numpy as jnp
from jax import lax
from jax.experimental import pallas as pl
from jax.experimental.pallas import tpu as pltpu

D_MODEL = 1024
GRID_W = 64
N_HEADS = 8
HEAD_DIM = 64
ATTN_W = N_HEADS * 2 * HEAD_DIM
CONV_W = 512
CONV_K = 31
N_EXPERTS = 256
TOP_K = 8
N_GROUPS = 8
TOPK_GROUPS = 4
GROUP_SIZE = N_EXPERTS // N_GROUPS
EXPERT_HIDDEN = 256
ROUTE_SCALE = 2.5
ROPE_BASE = 10000.0
EPS = 1e-6
N_MOD = 6
IN_COLS = 3 * ATTN_W + 2 * CONV_W + 2 * D_MODEL
LAM_INIT = 0.8 - 0.6 * math.exp(-0.3 * 0)
LOG2E = math.log2(math.e)

LANES = 128
SUBLANES = 8
VMEM_LIMIT = 56 * 1024 * 1024
HALO = 16
CONV_ROWS = 64
ROW_TILE = 256
CTX_HEADS_PER_STEP = 8
LAT_HEADS_PER_STEP = 2
ATTN_KEY_CHUNK = 512
MOE_BLOCK = 256
EXPERT_AHEAD = 3
EXPERT_RING = EXPERT_AHEAD + 1
MOD_ROWS = 8
MOD_COL_TILES = 4
PACK = D_MODEL // 2 // LANES
CHUNK = 4
PAIR = SUBLANES // PACK
STAGE = ROW_TILE * TOP_K + N_EXPERTS * (CHUNK - 1)
REC_CHUNKS = 768
REC = 2048
assert STAGE // CHUNK <= REC_CHUNKS and REC_CHUNKS % LANES == 0

BF16 = jnp.bfloat16
F32 = jnp.float32


class Rows(NamedTuple):
    tiles_ctx: int
    tiles_lat: int
    tiles_per_ctx: int
    tiles_per_lat: int

    @property
    def tiles(self):
        return self.tiles_ctx + self.tiles_lat

    def ctx_tile(self, i):
        return jnp.minimum(i, self.tiles_ctx - 1)

    def lat_tile(self, i):
        return jnp.maximum(i - self.tiles_ctx, 0)

    def mod_row(self, i):
        return jnp.where(i < self.tiles_ctx, 0, 1 + self.lat_tile(i) // self.tiles_per_lat)

    def seq_tile(self, i):
        is_ctx = i < self.tiles_ctx
        per = jnp.where(is_ctx, self.tiles_per_ctx, self.tiles_per_lat)
        return jnp.where(is_ctx, i % self.tiles_per_ctx, self.lat_tile(i) % self.tiles_per_lat), per


def _cparams(sem):
    return pltpu.CompilerParams(dimension_semantics=sem, vmem_limit_bytes=VMEM_LIMIT)


def _row_spec(width, tile=lambda i: i):
    return pl.BlockSpec((ROW_TILE, width), lambda i: (tile(i), 0))


def _const_spec(shape):
    return pl.BlockSpec(shape, lambda i: (0,) * len(shape))


def _mod_spec(rows):
    return pl.BlockSpec((1, 1, N_MOD * D_MODEL), lambda i: (rows.mod_row(i), 0, 0))


def _silu(x):
    return x * jax.nn.sigmoid(x)


def _dot(a, b):
    return jnp.dot(a, b, preferred_element_type=F32)


def _dot_nt(a, b):
    return lax.dot_general(a, b, (((1,), (1,)), ((), ())), preferred_element_type=F32)


def _rms(x, g):
    return x * lax.rsqrt(jnp.mean(x * x, axis=-1, keepdims=True) + EPS) * g


def _packed_rows(token, n_tokens, align):
    return pl.ds(pl.multiple_of(token * PACK, align * PACK), n_tokens * PACK)


def _store_packed(ref, token0, x):
    half = D_MODEL // 2
    lo = pltpu.bitcast(x[:, :half], jnp.uint32) >> 16
    hi = pltpu.bitcast(x[:, half:], jnp.uint32) & jnp.uint32(0xFFFF0000)
    w = lo | hi
    for s in range(PACK):
        ref[pl.ds(token0 * PACK + s, x.shape[0], stride=PACK), :] = w[:, s * LANES:(s + 1) * LANES]


def _load_packed(ref, token0, n_tokens):
    w = jnp.concatenate([ref[pl.ds(token0 * PACK + s, n_tokens, stride=PACK), :] for s in range(PACK)], axis=1)
    lo = pltpu.bitcast(w << 16, F32)
    hi = pltpu.bitcast(w & jnp.uint32(0xFFFF0000), F32)
    return jnp.concatenate([lo, hi], axis=1).astype(BF16)


def _mod_kernel(c_ref, w_ref, b_ref, o_ref):
    a = _silu(c_ref[...]).astype(BF16)
    o_ref[...] = _dot(a, w_ref[...].astype(BF16)) + b_ref[...]


def _mod_table(cc, w_mod, b_mod):
    n = w_mod.shape[1]
    tn = n // MOD_COL_TILES
    return pl.pallas_call(
        _mod_kernel,
        grid=(MOD_COL_TILES,),
        in_specs=[pl.BlockSpec((MOD_ROWS, D_MODEL), lambda j: (0, 0)),
                  pl.BlockSpec((D_MODEL, tn), lambda j: (0, j)),
                  pl.BlockSpec((1, tn), lambda j: (0, j))],
        out_specs=pl.BlockSpec((MOD_ROWS, tn), lambda j: (0, j)),
        out_shape=jax.ShapeDtypeStruct((MOD_ROWS, n), F32),
        compiler_params=_cparams(("arbitrary",)),
        name="mod_table",
    )(cc, w_mod, b_mod.reshape(1, n))


def _rope(x, cos, sin, lane_lo):
    swapped = jnp.where(lane_lo, pltpu.roll(x, LANES - 16, axis=1), pltpu.roll(x, 16, axis=1))
    return x * cos + swapped * sin


def _inproj_kernel(rows, xp_ref, xs_ref, mod_ref, g_ref, w_ref, cos_ref, sin_ref,
                   q_ref, k_ref, v_ref, z_ref, gt_ref, ks_ref, vs_ref):
    is_ctx = pl.program_id(0) < rows.tiles_ctx
    mod = mod_ref[0]
    shift, scale = mod[:, 0:D_MODEL], mod[:, D_MODEL:2 * D_MODEL]
    x = jnp.where(is_ctx, xp_ref[...], xs_ref[...])
    h = _rms(x, g_ref[...]) * (1.0 + scale) + shift
    hb = h.astype(BF16)
    q = _dot(hb, w_ref[:, 0:ATTN_W]) * (HEAD_DIM ** -0.5 * LOG2E)
    k = _dot(hb, w_ref[:, ATTN_W:2 * ATTN_W])
    v = _dot(hb, w_ref[:, 2 * ATTN_W:3 * ATTN_W])

    @pl.when(is_ctx)
    def _():
        q_ref[...] = q.astype(BF16)
        k_ref[...] = k.astype(BF16)
        ks_ref[...] = k
        vs_ref[...] = v

    @pl.when(jnp.logical_not(is_ctx))
    def _():
        cos, sin = cos_ref[...], sin_ref[...]
        lane_lo = (lax.broadcasted_iota(jnp.int32, cos.shape, 1) % 32) < 16
        for j in range(ATTN_W // LANES):
            sl = slice(j * LANES, (j + 1) * LANES)
            q_ref[:, sl] = _rope(q[:, sl], cos, sin, lane_lo).astype(BF16)
            k_ref[:, sl] = _rope(k[:, sl], cos, sin, lane_lo).astype(BF16)

    v_ref[...] = v.astype(BF16)
    u = _dot(hb, w_ref[:, 3 * ATTN_W:3 * ATTN_W + 2 * CONV_W])
    z_ref[...] = u[:, :CONV_W] * jax.nn.sigmoid(u[:, CONV_W:])
    gt_ref[...] = jax.nn.sigmoid(_dot(hb, w_ref[:, 3 * ATTN_W + 2 * CONV_W:]))


def _inproj(rows, xp, xs, mod3, norm_g, w_in_b, rope):
    n_rows = rows.tiles * ROW_TILE
    rows_p = rows.tiles_ctx * ROW_TILE
    rope_spec = pl.BlockSpec((ROW_TILE, LANES), lambda i: (rows.lat_tile(i) % rows.tiles_per_lat, 0))
    shapes = lambda w, dt, r=n_rows: jax.ShapeDtypeStruct((r, w), dt)
    return pl.pallas_call(
        functools.partial(_inproj_kernel, rows),
        grid=(rows.tiles,),
        in_specs=[_row_spec(D_MODEL, rows.ctx_tile), _row_spec(D_MODEL, rows.lat_tile), _mod_spec(rows),
                  _const_spec((1, D_MODEL)),
                  pl.BlockSpec((D_MODEL, IN_COLS), lambda i: (0, 0), pipeline_mode=pl.Buffered(1)),
                  rope_spec, rope_spec],
        out_specs=[_row_spec(ATTN_W)] * 3 + [_row_spec(CONV_W), _row_spec(2 * D_MODEL)]
                  + [_row_spec(ATTN_W, rows.ctx_tile)] * 2,
        out_shape=[shapes(ATTN_W, BF16)] * 3 + [shapes(CONV_W, F32), shapes(2 * D_MODEL, F32)]
                  + [shapes(ATTN_W, F32, rows_p)] * 2,
        compiler_params=_cparams(("arbitrary",)),
        name="inproj",
    )(xp, xs, mod3, norm_g, w_in_b, *rope)


def _rope_tables(n_tokens):
    t = jnp.arange(n_tokens, dtype=jnp.int32)
    pos = jnp.stack([t // GRID_W, t % GRID_W], axis=-1).astype(F32)
    half = HEAD_DIM // 2
    inv = ROPE_BASE ** (-jnp.arange(0, half, 2, dtype=F32) / half)
    ang = pos[:, :, None] * inv
    cos, sin = jnp.cos(ang), jnp.sin(ang)
    cos64 = jnp.concatenate([cos[:, 0], cos[:, 0], cos[:, 1], cos[:, 1]], axis=-1)
    sin64 = jnp.concatenate([-sin[:, 0], sin[:, 0], -sin[:, 1], sin[:, 1]], axis=-1)
    return jnp.tile(cos64, (1, LANES // HEAD_DIM)), jnp.tile(sin64, (1, LANES // HEAD_DIM))


def _lane_groups(x):
    return [x[:, j * LANES:(j + 1) * LANES] for j in range(x.shape[1] // LANES)]


def _attn_kernel(has_cache, heads, lq_ref, sg_ref, q_ref, k_ref, v_ref, *rest):
    if has_cache:
        ck_ref, cv_ref, o_ref, *bufs = rest
    else:
        o_ref, *bufs = rest

    def s_ref(u, mp):
        return bufs[(u % 2) * 2 + mp]

    lq = lq_ref[...]
    lam = (jnp.exp(jnp.sum(lq[0:1] * lq[1:2], axis=-1, keepdims=True))
           - jnp.exp(jnp.sum(lq[2:3] * lq[3:4], axis=-1, keepdims=True)) + LAM_INIT)
    tq, seq = q_ref.shape[0], k_ref.shape[0]
    chunks = [(off, min(ATTN_KEY_CHUNK, seq - off), False) for off in range(0, seq, ATTN_KEY_CHUNK)]
    if has_cache:
        chunks.append((seq, ck_ref.shape[0], True))
    first = lax.broadcasted_iota(jnp.int32, (tq, LANES), 1) < HEAD_DIM
    neg = jnp.full((tq, LANES), -jnp.inf, F32)
    heads_state = [dict() for _ in range(heads)]

    def lanes(u):
        return slice(u * LANES, (u + 1) * LANES)

    def load(main_ref, cache_ref, u, chunk):
        off, size, cached = chunk
        if cached:
            return cache_ref[:, lanes(u)].astype(BF16)
        return main_ref[off:off + size, lanes(u)]

    def scores(u, chunk):
        st = heads_state[u]
        if "q" not in st:
            q = q_ref[:, lanes(u)]
            zero = jnp.zeros_like(q)
            st["q"] = (jnp.where(first, q, zero), jnp.where(first, zero, q))
            st["macc"] = [neg, neg]
        kk = load(k_ref, ck_ref if has_cache else None, u, chunk)
        off, size, _ = chunk
        for mp in range(2):
            s = _dot_nt(st["q"][mp], kk)
            s_ref(u, mp)[:, off:off + size] = s
            st["macc"][mp] = functools.reduce(jnp.maximum, _lane_groups(s), st["macc"][mp])

    def exps(u, chunk):
        st = heads_state[u]
        if "m" not in st:
            st["m"] = [jnp.max(a, axis=-1, keepdims=True) for a in st["macc"]]
            st["lacc"] = [jnp.zeros((tq, LANES), F32)] * 2
        off, size, _ = chunk
        for mp in range(2):
            e = jnp.exp2(s_ref(u, mp)[:, off:off + size] - st["m"][mp])
            s_ref(u, mp)[:, off:off + size] = e
            st["lacc"][mp] = functools.reduce(jnp.add, _lane_groups(e), st["lacc"][mp])

    def values(u, chunk):
        st = heads_state[u]
        if "r" not in st:
            l1, l2 = [jnp.sum(a, axis=-1, keepdims=True) for a in st["lacc"]]
            st["r"] = (1.0 / l1, lam / l2)
            st["o"] = jnp.zeros((tq, LANES), F32)
        off, size, _ = chunk
        w = s_ref(u, 0)[:, off:off + size] * st["r"][0] - s_ref(u, 1)[:, off:off + size] * st["r"][1]
        st["o"] = st["o"] + _dot(w.astype(BF16), load(v_ref, cv_ref if has_cache else None, u, chunk))

    stages = (scores, exps, values)
    for phase in range(heads + len(stages) - 1):
        for chunk in chunks:
            for s in reversed(range(len(stages))):
                u = phase - s
                if 0 <= u < heads:
                    stages[s](u, chunk)
        u = phase - (len(stages) - 1)
        if 0 <= u < heads:
            o_ref[:, lanes(u)] = (_rms(heads_state[u]["o"], sg_ref[...]) * (1.0 - LAM_INIT)).astype(BF16)


def _attention(q, k, v, lambda_qk, subln_g, row0, batch, seq_len, heads, cache=None):
    tq = ROW_TILE
    q_tiles = seq_len // tq
    assert row0 % seq_len == 0
    seq0 = row0 // seq_len
    width = heads * LANES
    head_q = pl.BlockSpec((tq, width), lambda b, h, i: (seq0 * q_tiles + b * q_tiles + i, h))
    head_kv = pl.BlockSpec((seq_len, width), lambda b, h, i: (seq0 + b, h))
    in_specs = [pl.BlockSpec((4, HEAD_DIM), lambda b, h, i: (0, 0)),
                pl.BlockSpec((1, 2 * HEAD_DIM), lambda b, h, i: (0, 0)),
                head_q, head_kv, head_kv]
    args = [lambda_qk, subln_g, q, k, v]
    n_keys = seq_len
    if cache is not None:
        past = cache[0].shape[1]
        n_keys += past
        head_cache = pl.BlockSpec((None, past, width), lambda b, h, i: (b, 0, h))
        in_specs += [head_cache, head_cache]
        args += list(cache)
    return pl.pallas_call(
        functools.partial(_attn_kernel, cache is not None, heads),
        grid=(batch, N_HEADS // heads, q_tiles),
        in_specs=in_specs,
        out_specs=pl.BlockSpec((tq, width), lambda b, h, i: (b * q_tiles + i, h)),
        out_shape=jax.ShapeDtypeStruct((batch * seq_len, ATTN_W), BF16),
        scratch_shapes=[pltpu.VMEM((tq, n_keys), F32)] * 4,
        compiler_params=_cparams(("parallel", "parallel", "arbitrary")),
        name="attn_latent" if cache is not None else "attn_ctx",
    )(*args)


def _merge_kernel(rows, xp_ref, xs_ref, onp_ref, ons_ref, z_ref, zp_ref, zn_ref, gt_ref, mod_ref, cw_ref, cb_ref,
                  lg_ref, lb_ref, wco_ref, wap_ref, wout_ref, n2_ref, x1_ref, h2_ref, zext_ref):
    i = pl.program_id(0)
    is_ctx = i < rows.tiles_ctx
    tm = ROW_TILE
    t, per = rows.seq_tile(i)
    zext_ref[HALO:HALO + tm, :] = z_ref[...]
    zext_ref[0:HALO, :] = jnp.where(t == 0, 0.0, zp_ref[...])
    zext_ref[HALO + tm:, :] = jnp.where(t == per - 1, 0.0, zn_ref[...])
    pad = HALO - CONV_K // 2
    col_blocks = []
    for cb in range(CONV_W // LANES):
        cs = slice(cb * LANES, (cb + 1) * LANES)
        row_chunks = []
        for r0 in range(0, tm, CONV_ROWS):
            acc = jnp.zeros((CONV_ROWS, LANES), F32) + cb_ref[:, cs]
            for j in range(CONV_K):
                acc = acc + zext_ref[r0 + j + pad:r0 + j + pad + CONV_ROWS, cs] * cw_ref[j:j + 1, cs]
            row_chunks.append(acc)
        col_blocks.append(jnp.concatenate(row_chunks, axis=0))
    c = jnp.concatenate(col_blocks, axis=1)
    mu = jnp.mean(c, axis=-1, keepdims=True)
    cc = c - mu
    y = cc * lax.rsqrt(jnp.mean(cc * cc, axis=-1, keepdims=True) + EPS) * lg_ref[...] + lb_ref[...]
    conv_out = _dot(_silu(y).astype(BF16), wco_ref[...])
    o_n = jnp.where(is_ctx, onp_ref[...], ons_ref[...])
    a_br = _dot(o_n, wap_ref[...])
    gt = gt_ref[...]
    merged = gt[:, :D_MODEL] * a_br + gt[:, D_MODEL:] * conv_out
    mod = mod_ref[0]
    gate1 = mod[:, 2 * D_MODEL:3 * D_MODEL]
    shift2, scale2 = mod[:, 3 * D_MODEL:4 * D_MODEL], mod[:, 4 * D_MODEL:5 * D_MODEL]
    x = jnp.where(is_ctx, xp_ref[...], xs_ref[...])
    x1 = x + gate1 * _dot(merged.astype(BF16), wout_ref[...])
    x1_ref[...] = x1
    h2_ref[...] = (_rms(x1, n2_ref[...]) * (1.0 + scale2) + shift2).astype(BF16)


def _merge(rows, xp, xs, on_p, on_s, z, gt, mod3, conv_w, conv_b, ln_g, ln_b, wco_b, wap_b, wout_b, norm2_g):
    n_rows = rows.tiles * ROW_TILE
    hb = ROW_TILE // HALO
    n_halo_blocks = n_rows // HALO
    in_specs = [_row_spec(D_MODEL, rows.ctx_tile), _row_spec(D_MODEL, rows.lat_tile),
                _row_spec(ATTN_W, rows.ctx_tile), _row_spec(ATTN_W, rows.lat_tile),
                _row_spec(CONV_W),
                pl.BlockSpec((HALO, CONV_W), lambda i: (jnp.maximum(i * hb - 1, 0), 0)),
                pl.BlockSpec((HALO, CONV_W), lambda i: (jnp.minimum((i + 1) * hb, n_halo_blocks - 1), 0)),
                _row_spec(2 * D_MODEL), _mod_spec(rows),
                _const_spec((CONV_K, CONV_W)), _const_spec((1, CONV_W)), _const_spec((1, CONV_W)),
                _const_spec((1, CONV_W)), _const_spec((CONV_W, D_MODEL)), _const_spec((ATTN_W, D_MODEL)),
                _const_spec((D_MODEL, D_MODEL)), _const_spec((1, D_MODEL))]
    return pl.pallas_call(
        functools.partial(_merge_kernel, rows),
        grid=(rows.tiles,),
        in_specs=in_specs,
        out_specs=[_row_spec(D_MODEL), _row_spec(D_MODEL)],
        out_shape=[jax.ShapeDtypeStruct((n_rows, D_MODEL), F32), jax.ShapeDtypeStruct((n_rows, D_MODEL), BF16)],
        scratch_shapes=[pltpu.VMEM((ROW_TILE + 2 * HALO, CONV_W), F32)],
        compiler_params=_cparams(("arbitrary",)),
        name="conv_merge",
    )(xp, xs, on_p, on_s, z, z, z, gt, mod3, conv_w, conv_b, ln_g, ln_b, wco_b, wap_b, wout_b, norm2_g)


def _pack_cols(cols, shape):
    lane = lax.broadcasted_iota(jnp.int32, shape, 1)
    out = jnp.zeros(shape, F32)
    for j, col in enumerate(cols):
        out = jnp.where(lane == j, col, out)
    return out


def _router_kernel(h_ref, wr_ref, rb_ref, gate_ref, loc_ref, rec_ref, cnt_ref, run_ref):
    i = pl.program_id(0)

    @pl.when(i == 0)
    def _():
        run_ref[...] = jnp.zeros_like(run_ref)

    tm = ROW_TILE
    neg = jnp.float32(-jnp.inf)
    scores = jax.nn.sigmoid(_dot(h_ref[...], wr_ref[...]))
    biased = scores + rb_ref[...]
    lane_i = lax.broadcasted_iota(jnp.int32, scores.shape, 1)
    lane = lane_i.astype(F32)
    far = jnp.float32(2 * N_EXPERTS)

    def first_argmax(v):
        m = jnp.max(v, axis=-1, keepdims=True)
        return m, jnp.min(jnp.where(v == m, lane, far), axis=-1, keepdims=True)

    in_group, gscore = [], []
    for g in range(N_GROUPS):
        inb = (lane_i >= g * GROUP_SIZE) & (lane_i < (g + 1) * GROUP_SIZE)
        v = jnp.where(inb, biased, neg)
        m1, i1 = first_argmax(v)
        m2 = jnp.max(jnp.where(lane == i1, neg, v), axis=-1, keepdims=True)
        in_group.append(inb)
        gscore.append(m1 + m2)
    allowed = jnp.zeros(scores.shape, F32)
    for g in range(N_GROUPS):
        ahead = jnp.zeros((tm, 1), F32)
        for g2 in range(N_GROUPS):
            if g2 < g:
                ahead = ahead + (gscore[g2] >= gscore[g]).astype(F32)
            elif g2 > g:
                ahead = ahead + (gscore[g2] > gscore[g]).astype(F32)
        keep = (ahead < TOPK_GROUPS).astype(F32)
        allowed = jnp.where(in_group[g], keep, allowed)
    masked = jnp.where(allowed > 0.0, biased, neg)
    picked = jnp.zeros(scores.shape, F32)
    idxs, gates = [], []
    for _ in range(TOP_K):
        _, ik = first_argmax(masked)
        hit = lane == ik
        gates.append(jnp.sum(jnp.where(hit, scores, 0.0), axis=-1, keepdims=True))
        masked = jnp.where(hit, neg, masked)
        picked = jnp.where(hit, 1.0, picked)
        idxs.append(ik)
    gsum = functools.reduce(jnp.add, gates)
    gates = [g / gsum * ROUTE_SCALE for g in gates]
    r_i = lax.broadcasted_iota(jnp.int32, (tm, tm), 0)
    c_i = lax.broadcasted_iota(jnp.int32, (tm, tm), 1)
    before = (c_i < r_i).astype(BF16)
    local_rank = _dot(before, picked.astype(BF16))
    n_tok = jnp.sum(picked, axis=0, keepdims=True)
    n_chunks = jnp.floor((n_tok + (CHUNK - 1)) * (1.0 / CHUNK))
    lower = (lax.broadcasted_iota(jnp.int32, (N_EXPERTS, N_EXPERTS), 0)
             < lax.broadcasted_iota(jnp.int32, (N_EXPERTS, N_EXPERTS), 1)).astype(BF16)
    chunks_before = _dot(jnp.broadcast_to(n_chunks, (SUBLANES, N_EXPERTS)).astype(BF16), lower)[0:1]
    staged_off = chunks_before * CHUNK
    staged = local_rank + staged_off
    locs = [jnp.sum(jnp.where(lane == ik, staged, 0.0), axis=-1, keepdims=True) for ik in idxs]
    e_r = lax.broadcasted_iota(jnp.int32, (N_EXPERTS, N_EXPERTS), 0)
    e_c = lax.broadcasted_iota(jnp.int32, (N_EXPERTS, N_EXPERTS), 1)
    column = lambda v: jnp.sum(jnp.where(e_r == e_c, v, 0.0), axis=1, keepdims=True)
    first = column(chunks_before)
    last = first + column(n_chunks)
    region_off = column(run_ref[...] - staged_off)
    j = lax.broadcasted_iota(jnp.int32, (N_EXPERTS, REC_CHUNKS), 1).astype(F32)
    owns = (first <= j) & (j < last)
    expert = lax.broadcasted_iota(jnp.int32, (N_EXPERTS, REC_CHUNKS), 0).astype(F32)
    chunk_expert = jnp.sum(jnp.where(owns, expert, 0.0), axis=0, keepdims=True)
    chunk_off = jnp.sum(jnp.where(owns, region_off, 0.0), axis=0, keepdims=True) + j[0:1] * CHUNK
    total = jnp.broadcast_to(jnp.sum(n_chunks, axis=1, keepdims=True), (1, REC - 2 * REC_CHUNKS))
    rec_ref[0] = jnp.concatenate([chunk_expert, chunk_off, total], axis=1).astype(jnp.int32)
    run_ref[...] = run_ref[...] + jnp.floor((n_tok + (PAIR - 1)) * (1.0 / PAIR)) * PAIR
    cnt_ref[...] = run_ref[...]
    shape = gate_ref.shape
    gate_ref[...] = _pack_cols(gates, shape)
    loc_ref[...] = _pack_cols(locs, shape).astype(jnp.int32)


def _router(rows, h2, w_router_b, router_bias):
    n_rows = rows.tiles * ROW_TILE
    return pl.pallas_call(
        _router_kernel,
        grid=(rows.tiles,),
        in_specs=[_row_spec(D_MODEL), _const_spec((D_MODEL, N_EXPERTS)), _const_spec((1, N_EXPERTS))],
        out_specs=[_row_spec(LANES), _row_spec(LANES), pl.BlockSpec((1, 1, REC), lambda i: (i, 0, 0)),
                   _const_spec((1, N_EXPERTS))],
        out_shape=[jax.ShapeDtypeStruct((n_rows, LANES), F32), jax.ShapeDtypeStruct((n_rows, LANES), jnp.int32),
                   jax.ShapeDtypeStruct((rows.tiles, 1, REC), jnp.int32),
                   jax.ShapeDtypeStruct((1, N_EXPERTS), F32)],
        scratch_shapes=[pltpu.VMEM((1, N_EXPERTS), F32)],
        compiler_params=_cparams(("arbitrary",)),
        name="router",
    )(h2, w_router_b, router_bias)


def _record_copy(i, rec_hbm, rsm, sem_i, base=0):
    base = base if isinstance(base, int) else pl.multiple_of(base, REC)
    return pltpu.make_async_copy(rec_hbm.at[pl.ds(i * REC, REC)], rsm.at[pl.ds(base, REC)], sem_i)


def _load_record(i, rec_hbm, rsm, sem_i, base=0):
    cp = _record_copy(i, rec_hbm, rsm, sem_i, base)
    cp.start()
    cp.wait()


def _move_chunks(ps_ref, rsm, staged_ref, sorted_hbm, sem, chunks, to_sorted, base=0):
    total = rsm[base + 2 * REC_CHUNKS]
    spare0 = sorted_hbm.shape[0] // PACK - STAGE
    for j in chunks:
        placed_tok = jnp.where(j < total, ps_ref[rsm[base + j]] + rsm[base + REC_CHUNKS + j], spare0 + j * CHUNK)
        staged = staged_ref.at[pl.ds(j * CHUNK * PACK, CHUNK * PACK)]
        placed = sorted_hbm.at[_packed_rows(placed_tok, CHUNK, PAIR)]
        src, dst = (staged, placed) if to_sorted else (placed, staged)
        pltpu.make_async_copy(src, dst, sem).start()


def _wait_all_chunks(staged_ref, sorted_hbm, sem):
    pltpu.make_async_copy(staged_ref, sorted_hbm.at[pl.ds(0, STAGE * PACK)], sem).wait()


def _staged_block(loc, rb):
    col = lax.broadcasted_iota(jnp.int32, (loc.shape[0], MOE_BLOCK), 1) + rb * MOE_BLOCK
    return [loc[:, k:k + 1] == col for k in range(TOP_K)]


def _dispatch_kernel(pe_ref, ps_ref, c2_ref, rec_hbm, loc_ref, h_ref, xs_hbm, rsm, stage, zbuf, sem_i, sem):
    i = pl.program_id(0)
    block_rows = MOE_BLOCK * PACK

    @pl.when(i == 0)
    def _():
        zbuf[...] = jnp.zeros_like(zbuf)
        n_blocks = xs_hbm.shape[0] // block_rows
        n_used = pe_ref[N_EXPERTS - 1] // MOE_BLOCK

        def clear_block(b, carry):
            dst = xs_hbm.at[pl.ds(pl.multiple_of(b * block_rows, block_rows), block_rows)]
            pltpu.make_async_copy(zbuf, dst, sem.at[0]).start()
            return carry

        def clear_padding(e, total):
            first = (ps_ref[e] + c2_ref[e]) // MOE_BLOCK
            last = pe_ref[e] // MOE_BLOCK
            lax.fori_loop(first, last, clear_block, 0)
            return total + last - first

        def clear_wait(e, carry):
            pltpu.make_async_copy(zbuf, xs_hbm.at[pl.ds(0, block_rows)], sem.at[0]).wait()
            return carry

        n_cleared = lax.fori_loop(0, N_EXPERTS, clear_padding, 0)
        lax.fori_loop(n_used, n_blocks, clear_block, 0)
        lax.fori_loop(0, n_cleared + n_blocks - n_used, clear_wait, 0)

    _load_record(i, rec_hbm, rsm, sem_i)
    slot = i % 2
    loc, hb = loc_ref[...], h_ref[...]

    def stage_block(rb):
        onehot = jnp.zeros((ROW_TILE, MOE_BLOCK), F32)
        for hit in _staged_block(loc, rb):
            onehot = jnp.where(hit, 1.0, onehot)
        rows = lax.dot_general(onehot.astype(BF16), hb, (((0,), (0,)), ((), ())), preferred_element_type=F32)
        _store_packed(stage.at[slot], rb * MOE_BLOCK, rows)

    n_blocks, n_chunks = STAGE // MOE_BLOCK, STAGE // CHUNK
    n_early = n_blocks // 2 + 1
    for rb in range(n_early):
        stage_block(rb)

    @pl.when(i >= 1)
    def _():
        _wait_all_chunks(stage.at[1 - slot], xs_hbm, sem.at[1 - slot])

    per_round = -(-n_chunks // (n_blocks - n_early))
    for r, rb in enumerate(range(n_early, n_blocks)):
        stage_block(rb)
        chunks = range(r * per_round, min((r + 1) * per_round, n_chunks))
        assert (chunks[-1] * CHUNK) // MOE_BLOCK <= rb
        _move_chunks(ps_ref, rsm, stage.at[slot], xs_hbm, sem.at[slot], chunks, to_sorted=True)

    @pl.when(i == pl.num_programs(0) - 1)
    def _():
        _wait_all_chunks(stage.at[slot], xs_hbm, sem.at[slot])


def _dispatch(rows, pad_end, pad_start, counts, rec_flat, locs, h2, n_rows_sorted):
    return pl.pallas_call(
        _dispatch_kernel,
        grid_spec=pltpu.PrefetchScalarGridSpec(
            num_scalar_prefetch=3,
            grid=(rows.tiles,),
            in_specs=[pl.BlockSpec(memory_space=pl.ANY),
                      pl.BlockSpec((ROW_TILE, LANES), lambda i, pe, ps, c2: (i, 0)),
                      pl.BlockSpec((ROW_TILE, D_MODEL), lambda i, pe, ps, c2: (i, 0))],
            out_specs=pl.BlockSpec(memory_space=pl.ANY),
            scratch_shapes=[pltpu.SMEM((REC,), jnp.int32),
                            pltpu.VMEM((2, STAGE * PACK, LANES), jnp.uint32),
                            pltpu.VMEM((MOE_BLOCK * PACK, LANES), jnp.uint32),
                            pltpu.SemaphoreType.DMA, pltpu.SemaphoreType.DMA((2,))]),
        out_shape=jax.ShapeDtypeStruct(((n_rows_sorted + STAGE) * PACK, LANES), jnp.uint32),
        compiler_params=_cparams(("arbitrary",)),
        name="moe_dispatch",
    )(pad_end, pad_start, counts, rec_flat, locs, h2)


def _expert_kernel(pe_ref, ps_ref, xs_hbm, w1_ref, w3_ref, w2_ref, ys_hbm, xbuf, ybuf, w1b, w3b, w2b,
                   sem_in, sem_out):
    e = pl.program_id(0)
    first, last = ps_ref[e] // MOE_BLOCK, pe_ref[e] // MOE_BLOCK
    n_used = pe_ref[N_EXPERTS - 1] // MOE_BLOCK
    block_rows = MOE_BLOCK * PACK

    def rows_of(g):
        return pl.ds(pl.multiple_of(g * block_rows, block_rows), block_rows)

    def copy_in(g):
        slot = g % EXPERT_RING
        return pltpu.make_async_copy(xs_hbm.at[rows_of(g)], xbuf.at[slot], sem_in.at[slot])

    def copy_out(g):
        slot = g % EXPERT_RING
        return pltpu.make_async_copy(ybuf.at[slot], ys_hbm.at[rows_of(g)], sem_out.at[slot])

    @pl.when(e == 0)
    def _():
        for g in range(EXPERT_AHEAD):
            @pl.when(g < n_used)
            def _():
                copy_in(g).start()

    w1b[...] = w1_ref[...].astype(BF16)
    w3b[...] = w3_ref[...].astype(BF16)
    w2b[...] = w2_ref[...].astype(BF16)

    def block(g, carry):
        copy_in(g).wait()

        @pl.when(g + EXPERT_AHEAD < n_used)
        def _():
            copy_in(g + EXPERT_AHEAD).start()

        @pl.when(g >= EXPERT_RING)
        def _():
            copy_out(g - EXPERT_RING).wait()

        slot = g % EXPERT_RING
        xb = _load_packed(xbuf.at[slot], 0, MOE_BLOCK)
        hid = (_silu(_dot(xb, w1b[...])) * _dot(xb, w3b[...])).astype(BF16)
        _store_packed(ybuf.at[slot], 0, _dot(hid, w2b[...]).astype(BF16).astype(F32))
        copy_out(g).start()
        return carry

    lax.fori_loop(first, last, block, 0)

    @pl.when(e == N_EXPERTS - 1)
    def _():
        for back in range(EXPERT_RING, 0, -1):
            @pl.when(n_used >= back)
            def _():
                copy_out(n_used - back).wait()


def _experts(pad_end, pad_start, xs, w1, w3, w2):
    expert_spec = lambda shape: pl.BlockSpec((None,) + shape, lambda e, pe, ps: (e, 0, 0))
    any_spec = pl.BlockSpec(memory_space=pl.ANY)
    return pl.pallas_call(
        _expert_kernel,
        grid_spec=pltpu.PrefetchScalarGridSpec(
            num_scalar_prefetch=2,
            grid=(N_EXPERTS,),
            in_specs=[any_spec, expert_spec((D_MODEL, EXPERT_HIDDEN)), expert_spec((D_MODEL, EXPERT_HIDDEN)),
                      expert_spec((EXPERT_HIDDEN, D_MODEL))],
            out_specs=any_spec,
            scratch_shapes=[pltpu.VMEM((EXPERT_RING, MOE_BLOCK * PACK, LANES), jnp.uint32),
                            pltpu.VMEM((EXPERT_RING, MOE_BLOCK * PACK, LANES), jnp.uint32),
                            pltpu.VMEM((D_MODEL, EXPERT_HIDDEN), BF16),
                            pltpu.VMEM((D_MODEL, EXPERT_HIDDEN), BF16),
                            pltpu.VMEM((EXPERT_HIDDEN, D_MODEL), BF16),
                            pltpu.SemaphoreType.DMA((EXPERT_RING,)), pltpu.SemaphoreType.DMA((EXPERT_RING,))]),
        out_shape=jax.ShapeDtypeStruct(xs.shape, jnp.uint32),
        input_output_aliases={2: 0},
        compiler_params=_cparams(("arbitrary",)),
        name="moe_experts",
    )(pad_end, pad_start, xs, w1, w3, w2)


def _combine_kernel(rows, ps_ref, rec_hbm, ys_hbm, loc_ref, x1_ref, h_ref, gate_ref, mod_ref, ws1_ref, ws3_ref,
                    ws2_ref, nf_ref, outp_ref, outs_ref, rsm, ybuf, sem_i, sem):
    i = pl.program_id(0)
    slot = i % 2

    n_chunks = STAGE // CHUNK
    nxt = jnp.minimum(i + 1, pl.num_programs(0) - 1)
    other = 1 - slot

    def fetch(s, chunks):
        _move_chunks(ps_ref, rsm, ybuf.at[s], ys_hbm, sem.at[s], chunks, to_sorted=False, base=s * REC)

    next_record = _record_copy(nxt, rec_hbm, rsm, sem_i.at[other], base=other * REC)
    next_record.start()

    @pl.when(i == 0)
    def _():
        _load_record(i, rec_hbm, rsm, sem_i.at[slot], base=slot * REC)
        fetch(slot, range(n_chunks))

    hb = h_ref[...]
    shared = _dot((_silu(_dot(hb, ws1_ref[...])) * _dot(hb, ws3_ref[...])).astype(BF16), ws2_ref[...])
    staged_y = ybuf.at[slot]
    _wait_all_chunks(staged_y, ys_hbm, sem.at[slot])
    next_record.wait()
    loc, g = loc_ref[...], gate_ref[...]
    routed = jnp.zeros((ROW_TILE, D_MODEL), F32)
    for rb in range(STAGE // MOE_BLOCK):
        gm = jnp.zeros((ROW_TILE, MOE_BLOCK), F32)
        for k, hit in enumerate(_staged_block(loc, rb)):
            gm = jnp.where(hit, g[:, k:k + 1], gm)
        g_hi = gm.astype(BF16)
        g_lo = (gm - g_hi.astype(F32)).astype(BF16)
        yb = _load_packed(staged_y, rb * MOE_BLOCK, MOE_BLOCK)
        routed = routed + _dot(g_hi, yb) + _dot(g_lo, yb)
        per_block = MOE_BLOCK // CHUNK
        fetch(other, range(rb * per_block, (rb + 1) * per_block))
    gate2 = mod_ref[0][:, 5 * D_MODEL:6 * D_MODEL]
    x2 = x1_ref[...] + gate2 * (routed + shared)
    out = _rms(x2, nf_ref[...])

    @pl.when(i == pl.num_programs(0) - 1)
    def _():
        _wait_all_chunks(ybuf.at[other], ys_hbm, sem.at[other])

    @pl.when(i < rows.tiles_ctx)
    def _():
        outp_ref[...] = out

    @pl.when(i >= rows.tiles_ctx)
    def _():
        outs_ref[...] = out


def _combine(rows, pad_start, rec_flat, ys, locs, x1, h2, gates, mod3, ws1_b, ws3_b, ws2_b, normf_g):
    drop = lambda spec: pl.BlockSpec(spec.block_shape, lambda i, ps, f=spec.index_map: f(i))
    any_spec = pl.BlockSpec(memory_space=pl.ANY)
    return pl.pallas_call(
        functools.partial(_combine_kernel, rows),
        grid_spec=pltpu.PrefetchScalarGridSpec(
            num_scalar_prefetch=1,
            grid=(rows.tiles,),
            in_specs=[any_spec, any_spec]
                     + [drop(s) for s in (
                         _row_spec(LANES), _row_spec(D_MODEL), _row_spec(D_MODEL), _row_spec(LANES),
                         _mod_spec(rows),
                         _const_spec((D_MODEL, EXPERT_HIDDEN)), _const_spec((D_MODEL, EXPERT_HIDDEN)),
                         _const_spec((EXPERT_HIDDEN, D_MODEL)), _const_spec((1, D_MODEL)))],
            out_specs=[drop(_row_spec(D_MODEL, rows.ctx_tile)), drop(_row_spec(D_MODEL, rows.lat_tile))],
            scratch_shapes=[pltpu.SMEM((2 * REC,), jnp.int32),
                            pltpu.VMEM((2, STAGE * PACK, LANES), jnp.uint32),
                            pltpu.SemaphoreType.DMA((2,)), pltpu.SemaphoreType.DMA((2,))]),
        out_shape=[jax.ShapeDtypeStruct((rows.tiles_ctx * ROW_TILE, D_MODEL), F32),
                   jax.ShapeDtypeStruct((rows.tiles_lat * ROW_TILE, D_MODEL), F32)],
        compiler_params=_cparams(("arbitrary",)),
        name="moe_combine",
    )(pad_start, rec_flat, ys, locs, x1, h2, gates, mod3, ws1_b, ws3_b, ws2_b, normf_g)


def kernel(x_prompt, x_sample, cache_k, cache_v, c, c_ctx, w_mod, b_mod, norm1_g, w_in, lambda_qk, subln_g,
           w_attn_proj, conv_w, conv_b, conv_ln_g, conv_ln_b, w_conv_out, w_out, norm2_g, w_router,
           router_bias, w1, w3, w2, ws1, ws3, ws2, normf_g):
    batch, seq, _ = x_prompt.shape
    dec_batch, dec_seq, _ = x_sample.shape
    past = cache_k.shape[2]
    l = 0
    rows_p, rows_s = batch * seq, dec_batch * dec_seq
    n_rows = rows_p + rows_s
    rows = Rows(rows_p // ROW_TILE, rows_s // ROW_TILE, seq // ROW_TILE, dec_seq // ROW_TILE)
    xp = x_prompt.reshape(rows_p, D_MODEL)
    xs = x_sample.reshape(rows_s, D_MODEL)
    row = lambda a: a.reshape(1, -1)

    cc = jnp.zeros((MOD_ROWS, D_MODEL), F32).at[0].set(c_ctx).at[1:1 + dec_batch].set(c)
    mod3 = _mod_table(cc, w_mod[l], b_mod[l]).reshape(MOD_ROWS, 1, N_MOD * D_MODEL)

    q, k, v, z, gt, state_k, state_v = _inproj(rows, xp, xs, mod3, row(norm1_g[l]), w_in[l].astype(BF16),
                                               _rope_tables(dec_seq))
    lq, sg = lambda_qk[l], row(subln_g[l])
    on_p = _attention(q, k, v, lq, sg, 0, batch, seq, CTX_HEADS_PER_STEP)
    cache = (cache_k[:, l].reshape(dec_batch, past, ATTN_W), cache_v[:, l].reshape(dec_batch, past, ATTN_W))
    on_s = _attention(q, k, v, lq, sg, rows_p, dec_batch, dec_seq, LAT_HEADS_PER_STEP, cache=cache)
    x1, h2 = _merge(rows, xp, xs, on_p, on_s, z, gt, mod3, conv_w[l], row(conv_b[l]), row(conv_ln_g[l]),
                    row(conv_ln_b[l]), w_conv_out[l].astype(BF16), w_attn_proj[l].astype(BF16),
                    w_out[l].astype(BF16), row(norm2_g[l]))

    gates, locs, rec, counts = _router(rows, h2, w_router[l].astype(BF16), row(router_bias[l]))
    counts = counts[0].astype(jnp.int32)
    padded = (counts + CHUNK + MOE_BLOCK - 1) // MOE_BLOCK * MOE_BLOCK
    pad_end = jnp.cumsum(padded).astype(jnp.int32)
    pad_start = pad_end - padded
    max_rows = n_rows * TOP_K + rows.tiles * N_EXPERTS * (PAIR - 1) + N_EXPERTS * (CHUNK + MOE_BLOCK - 1)
    n_blocks = -(-max_rows // MOE_BLOCK)
    rec_flat = rec.reshape(-1)
    x_sorted = _dispatch(rows, pad_end, pad_start, counts, rec_flat, locs, h2, n_blocks * MOE_BLOCK)
    y_sorted = _experts(pad_end, pad_start, x_sorted, w1[l], w3[l], w2[l])
    y_p, y_s = _combine(rows, pad_start, rec_flat, y_sorted, locs, x1, h2, gates, mod3,
                        ws1[l].astype(BF16), ws3[l].astype(BF16), ws2[l].astype(BF16), row(normf_g))
    return (y_p.reshape(batch, seq, D_MODEL), y_s.reshape(dec_batch, dec_seq, D_MODEL),
            state_k.reshape(batch, 1, seq, N_HEADS, 2, HEAD_DIM),
            state_v.reshape(batch, 1, seq, N_HEADS, 2 * HEAD_DIM))
```

```python
import functools
import math
from typing import NamedTuple

import jax
import jax.numpy as jnp
from jax import lax
from jax.experimental import pallas as pl
from jax.experimental.pallas import tpu as pltpu

D_MODEL = 1024
GRID_W = 64
N_HEADS = 8
HEAD_DIM = 64
ATTN_W = N_HEADS * 2 * HEAD_DIM
CONV_W = 512
CONV_K = 31
N_EXPERTS = 256
TOP_K = 8
N_GROUPS = 8
TOPK_GROUPS = 4
GROUP_SIZE = N_EXPERTS // N_GROUPS
EXPERT_HIDDEN = 256
ROUTE_SCALE = 2.5
ROPE_BASE = 10000.0
EPS = 1e-6
N_MOD = 6
IN_COLS = 3 * ATTN_W + 2 * CONV_W + 2 * D_MODEL
LAM_INIT = 0.8 - 0.6 * math.exp(-0.3 * 0)
LOG2E = math.log2(math.e)

LANES = 128
SUBLANES = 8
VMEM_LIMIT = 56 * 1024 * 1024
HALO = 16
CONV_ROWS = 64
ROW_TILE = 256
CTX_HEADS_PER_STEP = 8
LAT_HEADS_PER_STEP = 2
ATTN_KEY_CHUNK = 512
MOE_BLOCK = 256
EXPERT_AHEAD = 3
EXPERT_RING = EXPERT_AHEAD + 1
MOD_ROWS = 8
MOD_COL_TILES = 4
PACK = D_MODEL // 2 // LANES
CHUNK = 4
PAIR = SUBLANES // PACK
STAGE = ROW_TILE * TOP_K + N_EXPERTS * (CHUNK - 1)
REC_CHUNKS = 768
REC = 2048
assert STAGE // CHUNK <= REC_CHUNKS and REC_CHUNKS % LANES == 0

BF16 = jnp.bfloat16
F32 = jnp.float32


class Rows(NamedTuple):
    tiles_ctx: int
    tiles_lat: int
    tiles_per_ctx: int
    tiles_per_lat: int

    @property
    def tiles(self):
        return self.tiles_ctx + self.tiles_lat

    def ctx_tile(self, i):
        return jnp.minimum(i, self.tiles_ctx - 1)

    def lat_tile(self, i):
        return jnp.maximum(i - self.tiles_ctx, 0)

    def mod_row(self, i):
        return jnp.where(i < self.tiles_ctx, 0, 1 + self.lat_tile(i) // self.tiles_per_lat)

    def seq_tile(self, i):
        is_ctx = i < self.tiles_ctx
        per = jnp.where(is_ctx, self.tiles_per_ctx, self.tiles_per_lat)
        return jnp.where(is_ctx, i % self.tiles_per_ctx, self.lat_tile(i) % self.tiles_per_lat), per


def _cparams(sem):
    return pltpu.CompilerParams(dimension_semantics=sem, vmem_limit_bytes=VMEM_LIMIT)


def _row_spec(width, tile=lambda i: i):
    return pl.BlockSpec((ROW_TILE, width), lambda i: (tile(i), 0))


def _const_spec(shape):
    return pl.BlockSpec(shape, lambda i: (0,) * len(shape))


def _mod_spec(rows):
    return pl.BlockSpec((1, 1, N_MOD * D_MODEL), lambda i: (rows.mod_row(i), 0, 0))


def _silu(x):
    return x * jax.nn.sigmoid(x)


def _dot(a, b):
    return jnp.dot(a, b, preferred_element_type=F32)


def _dot_nt(a, b):
    return lax.dot_general(a, b, (((1,), (1,)), ((), ())), preferred_element_type=F32)


def _rms(x, g):
    return x * lax.rsqrt(jnp.mean(x * x, axis=-1, keepdims=True) + EPS) * g


def _packed_rows(token, n_tokens, align):
    return pl.ds(pl.multiple_of(token * PACK, align * PACK), n_tokens * PACK)


def _store_packed(ref, token0, x):
    half = D_MODEL // 2
    lo = pltpu.bitcast(x[:, :half], jnp.uint32) >> 16
    hi = pltpu.bitcast(x[:, half:], jnp.uint32) & jnp.uint32(0xFFFF0000)
    w = lo | hi
    for s in range(PACK):
        ref[pl.ds(token0 * PACK + s, x.shape[0], stride=PACK), :] = w[:, s * LANES:(s + 1) * LANES]


def _load_packed(ref, token0, n_tokens):
    w = jnp.concatenate([ref[pl.ds(token0 * PACK + s, n_tokens, stride=PACK), :] for s in range(PACK)], axis=1)
    lo = pltpu.bitcast(w << 16, F32)
    hi = pltpu.bitcast(w & jnp.uint32(0xFFFF0000), F32)
    return jnp.concatenate([lo, hi], axis=1).astype(BF16)


def _mod_kernel(c_ref, w_ref, b_ref, o_ref):
    a = _silu(c_ref[...]).astype(BF16)
    o_ref[...] = _dot(a, w_ref[...].astype(BF16)) + b_ref[...]


def _mod_table(cc, w_mod, b_mod):
    n = w_mod.shape[1]
    tn = n // MOD_COL_TILES
    return pl.pallas_call(
        _mod_kernel,
        grid=(MOD_COL_TILES,),
        in_specs=[pl.BlockSpec((MOD_ROWS, D_MODEL), lambda j: (0, 0)),
                  pl.BlockSpec((D_MODEL, tn), lambda j: (0, j)),
                  pl.BlockSpec((1, tn), lambda j: (0, j))],
        out_specs=pl.BlockSpec((MOD_ROWS, tn), lambda j: (0, j)),
        out_shape=jax.ShapeDtypeStruct((MOD_ROWS, n), F32),
        compiler_params=_cparams(("arbitrary",)),
        name="mod_table",
    )(cc, w_mod, b_mod.reshape(1, n))


def _rope(x, cos, sin, lane_lo):
    swapped = jnp.where(lane_lo, pltpu.roll(x, LANES - 16, axis=1), pltpu.roll(x, 16, axis=1))
    return x * cos + swapped * sin


def _inproj_kernel(rows, xp_ref, xs_ref, mod_ref, g_ref, w_ref, cos_ref, sin_ref,
                   q_ref, k_ref, v_ref, z_ref, gt_ref, ks_ref, vs_ref):
    is_ctx = pl.program_id(0) < rows.tiles_ctx
    mod = mod_ref[0]
    shift, scale = mod[:, 0:D_MODEL], mod[:, D_MODEL:2 * D_MODEL]
    x = jnp.where(is_ctx, xp_ref[...], xs_ref[...])
    h = _rms(x, g_ref[...]) * (1.0 + scale) + shift
    hb = h.astype(BF16)
    q = _dot(hb, w_ref[:, 0:ATTN_W]) * (HEAD_DIM ** -0.5 * LOG2E)
    k = _dot(hb, w_ref[:, ATTN_W:2 * ATTN_W])
    v = _dot(hb, w_ref[:, 2 * ATTN_W:3 * ATTN_W])

    cos, sin = cos_ref[...], sin_ref[...]
    lane_lo = (lax.broadcasted_iota(jnp.int32, cos.shape, 1) % 32) < 16
    for j in range(ATTN_W // LANES):
        sl = slice(j * LANES, (j + 1) * LANES)
        q_ref[:, sl] = jnp.where(is_ctx, q[:, sl], _rope(q[:, sl], cos, sin, lane_lo)).astype(BF16)
        k_ref[:, sl] = jnp.where(is_ctx, k[:, sl], _rope(k[:, sl], cos, sin, lane_lo)).astype(BF16)

    v_ref[...] = v.astype(BF16)
    u = _dot(hb, w_ref[:, 3 * ATTN_W:3 * ATTN_W + 2 * CONV_W])
    z_ref[...] = u[:, :CONV_W] * jax.nn.sigmoid(u[:, CONV_W:])
    gt_ref[...] = jax.nn.sigmoid(_dot(hb, w_ref[:, 3 * ATTN_W + 2 * CONV_W:]))

    @pl.when(is_ctx)
    def _():
        ks_ref[...] = k
        vs_ref[...] = v


def _inproj(rows, xp, xs, mod3, norm_g, w_in_b, rope):
    n_rows = rows.tiles * ROW_TILE
    rows_p = rows.tiles_ctx * ROW_TILE
    rope_spec = pl.BlockSpec((ROW_TILE, LANES), lambda i: (rows.lat_tile(i) % rows.tiles_per_lat, 0))
    shapes = lambda w, dt, r=n_rows: jax.ShapeDtypeStruct((r, w), dt)
    return pl.pallas_call(
        functools.partial(_inproj_kernel, rows),
        grid=(rows.tiles,),
        in_specs=[_row_spec(D_MODEL, rows.ctx_tile), _row_spec(D_MODEL, rows.lat_tile), _mod_spec(rows),
                  _const_spec((1, D_MODEL)),
                  pl.BlockSpec((D_MODEL, IN_COLS), lambda i: (0, 0), pipeline_mode=pl.Buffered(1)),
                  rope_spec, rope_spec],
        out_specs=[_row_spec(ATTN_W)] * 3 + [_row_spec(CONV_W), _row_spec(2 * D_MODEL)]
                  + [_row_spec(ATTN_W, rows.ctx_tile)] * 2,
        out_shape=[shapes(ATTN_W, BF16)] * 3 + [shapes(CONV_W, F32), shapes(2 * D_MODEL, F32)]
                  + [shapes(ATTN_W, F32, rows_p)] * 2,
        compiler_params=_cparams(("arbitrary",)),
        name="inproj",
    )(xp, xs, mod3, norm_g, w_in_b, *rope)


def _rope_tables(n_tokens):
    t = jnp.arange(n_tokens, dtype=jnp.int32)
    pos = jnp.stack([t // GRID_W, t % GRID_W], axis=-1).astype(F32)
    half = HEAD_DIM // 2
    inv = ROPE_BASE ** (-jnp.arange(0, half, 2, dtype=F32) / half)
    ang = pos[:, :, None] * inv
    cos, sin = jnp.cos(ang), jnp.sin(ang)
    cos64 = jnp.concatenate([cos[:, 0], cos[:, 0], cos[:, 1], cos[:, 1]], axis=-1)
    sin64 = jnp.concatenate([-sin[:, 0], sin[:, 0], -sin[:, 1], sin[:, 1]], axis=-1)
    return jnp.tile(cos64, (1, LANES // HEAD_DIM)), jnp.tile(sin64, (1, LANES // HEAD_DIM))


def _lane_groups(x):
    return [x[:, j * LANES:(j + 1) * LANES] for j in range(x.shape[1] // LANES)]


def _attn_kernel(has_cache, heads, lq_ref, sg_ref, q_ref, k_ref, v_ref, *rest):
    if has_cache:
        ck_ref, cv_ref, o_ref, *bufs = rest
    else:
        o_ref, *bufs = rest

    def s_ref(u, mp):
        return bufs[(u % 2) * 2 + mp]

    lq = lq_ref[...]
    lam = (jnp.exp(jnp.sum(lq[0:1] * lq[1:2], axis=-1, keepdims=True))
           - jnp.exp(jnp.sum(lq[2:3] * lq[3:4], axis=-1, keepdims=True)) + LAM_INIT)
    tq, seq = q_ref.shape[0], k_ref.shape[0]
    chunks = [(off, min(ATTN_KEY_CHUNK, seq - off), False) for off in range(0, seq, ATTN_KEY_CHUNK)]
    if has_cache:
        chunks.append((seq, ck_ref.shape[0], True))
    first = lax.broadcasted_iota(jnp.int32, (tq, LANES), 1) < HEAD_DIM
    neg = jnp.full((tq, LANES), -jnp.inf, F32)
    heads_state = [dict() for _ in range(heads)]

    def lanes(u):
        return slice(u * LANES, (u + 1) * LANES)

    def load(main_ref, cache_ref, u, chunk):
        off, size, cached = chunk
        if cached:
            return cache_ref[:, lanes(u)].astype(BF16)
        return main_ref[off:off + size, lanes(u)]

    def scores(u, chunk):
        st = heads_state[u]
        if "q" not in st:
            q = q_ref[:, lanes(u)]
            zero = jnp.zeros_like(q)
            st["q"] = (jnp.where(first, q, zero), jnp.where(first, zero, q))
            st["macc"] = [neg, neg]
        kk = load(k_ref, ck_ref if has_cache else None, u, chunk)
        off, size, _ = chunk
        for mp in range(2):
            s = _dot_nt(st["q"][mp], kk)
            s_ref(u, mp)[:, off:off + size] = s
            st["macc"][mp] = functools.reduce(jnp.maximum, _lane_groups(s), st["macc"][mp])

    def exps(u, chunk):
        st = heads_state[u]
        if "m" not in st:
            st["m"] = [jnp.max(a, axis=-1, keepdims=True) for a in st["macc"]]
            st["lacc"] = [jnp.zeros((tq, LANES), F32)] * 2
        off, size, _ = chunk
        for mp in range(2):
            e = jnp.exp2(s_ref(u, mp)[:, off:off + size] - st["m"][mp])
            s_ref(u, mp)[:, off:off + size] = e
            st["lacc"][mp] = functools.reduce(jnp.add, _lane_groups(e), st["lacc"][mp])

    def values(u, chunk):
        st = heads_state[u]
        if "r" not in st:
            l1, l2 = [jnp.sum(a, axis=-1, keepdims=True) for a in st["lacc"]]
            st["r"] = (1.0 / l1, lam / l2)
            st["o"] = jnp.zeros((tq, LANES), F32)
        off, size, _ = chunk
        w = s_ref(u, 0)[:, off:off + size] * st["r"][0] - s_ref(u, 1)[:, off:off + size] * st["r"][1]
        st["o"] = st["o"] + _dot(w.astype(BF16), load(v_ref, cv_ref if has_cache else None, u, chunk))

    stages = (scores, exps, values)
    for phase in range(heads + len(stages) - 1):
        for chunk in chunks:
            for s in reversed(range(len(stages))):
                u = phase - s
                if 0 <= u < heads:
                    stages[s](u, chunk)
        u = phase - (len(stages) - 1)
        if 0 <= u < heads:
            o_ref[:, lanes(u)] = (_rms(heads_state[u]["o"], sg_ref[...]) * (1.0 - LAM_INIT)).astype(BF16)


def _attention(q, k, v, lambda_qk, subln_g, row0, batch, seq_len, heads, cache=None):
    tq = ROW_TILE
    q_tiles = seq_len // tq
    assert row0 % seq_len == 0
    seq0 = row0 // seq_len
    width = heads * LANES
    head_q = pl.BlockSpec((tq, width), lambda b, h, i: (seq0 * q_tiles + b * q_tiles + i, h))
    head_kv = pl.BlockSpec((seq_len, width), lambda b, h, i: (seq0 + b, h))
    in_specs = [pl.BlockSpec((4, HEAD_DIM), lambda b, h, i: (0, 0)),
                pl.BlockSpec((1, 2 * HEAD_DIM), lambda b, h, i: (0, 0)),
                head_q, head_kv, head_kv]
    args = [lambda_qk, subln_g, q, k, v]
    n_keys = seq_len
    if cache is not None:
        past = cache[0].shape[1]
        n_keys += past
        head_cache = pl.BlockSpec((None, past, width), lambda b, h, i: (b, 0, h))
        in_specs += [head_cache, head_cache]
        args += list(cache)
    return pl.pallas_call(
        functools.partial(_attn_kernel, cache is not None, heads),
        grid=(batch, N_HEADS // heads, q_tiles),
        in_specs=in_specs,
        out_specs=pl.BlockSpec((tq, width), lambda b, h, i: (b * q_tiles + i, h)),
        out_shape=jax.ShapeDtypeStruct((batch * seq_len, ATTN_W), BF16),
        scratch_shapes=[pltpu.VMEM((tq, n_keys), F32)] * 4,
        compiler_params=_cparams(("parallel", "parallel", "arbitrary")),
        name="attn_latent" if cache is not None else "attn_ctx",
    )(*args)


def _merge_kernel(rows, xp_ref, xs_ref, onp_ref, ons_ref, z_ref, zp_ref, zn_ref, gt_ref, mod_ref, cw_ref, cb_ref,
                  lg_ref, lb_ref, wco_ref, wap_ref, wout_ref, n2_ref, x1_ref, h2_ref, zext_ref, zsh_ref):
    i = pl.program_id(0)
    is_ctx = i < rows.tiles_ctx
    tm = ROW_TILE
    t, per = rows.seq_tile(i)
    zext_ref[HALO:HALO + tm, :] = z_ref[...]
    zext_ref[0:HALO, :] = jnp.where(t == 0, 0.0, zp_ref[...])
    zext_ref[HALO + tm:, :] = jnp.where(t == per - 1, 0.0, zn_ref[...])
    pad = HALO - CONV_K // 2
    shifted_rows = zsh_ref.shape[1]
    for s in range(SUBLANES):
        zsh_ref[s] = zext_ref[s:s + shifted_rows, :]
    col_blocks = []
    for cb in range(CONV_W // LANES):
        cs = slice(cb * LANES, (cb + 1) * LANES)
        row_chunks = []
        for r0 in range(0, tm, CONV_ROWS):
            acc = jnp.zeros((CONV_ROWS, LANES), F32) + cb_ref[:, cs]
            for j in range(CONV_K):
                phase, base = (j + pad) % SUBLANES, r0 + (j + pad) // SUBLANES * SUBLANES
                acc = acc + zsh_ref[phase, base:base + CONV_ROWS, cs] * cw_ref[j:j + 1, cs]
            row_chunks.append(acc)
        col_blocks.append(jnp.concatenate(row_chunks, axis=0))
    c = jnp.concatenate(col_blocks, axis=1)
    mu = jnp.mean(c, axis=-1, keepdims=True)
    cc = c - mu
    y = cc * lax.rsqrt(jnp.mean(cc * cc, axis=-1, keepdims=True) + EPS) * lg_ref[...] + lb_ref[...]
    conv_out = _dot(_silu(y).astype(BF16), wco_ref[...])
    o_n = jnp.where(is_ctx, onp_ref[...], ons_ref[...])
    a_br = _dot(o_n, wap_ref[...])
    gt = gt_ref[...]
    merged = gt[:, :D_MODEL] * a_br + gt[:, D_MODEL:] * conv_out
    mod = mod_ref[0]
    gate1 = mod[:, 2 * D_MODEL:3 * D_MODEL]
    shift2, scale2 = mod[:, 3 * D_MODEL:4 * D_MODEL], mod[:, 4 * D_MODEL:5 * D_MODEL]
    x = jnp.where(is_ctx, xp_ref[...], xs_ref[...])
    x1 = x + gate1 * _dot(merged.astype(BF16), wout_ref[...])
    x1_ref[...] = x1
    h2_ref[...] = (_rms(x1, n2_ref[...]) * (1.0 + scale2) + shift2).astype(BF16)


def _merge(rows, xp, xs, on_p, on_s, z, gt, mod3, conv_w, conv_b, ln_g, ln_b, wco_b, wap_b, wout_b, norm2_g):
    n_rows = rows.tiles * ROW_TILE
    hb = ROW_TILE // HALO
    n_halo_blocks = n_rows // HALO
    in_specs = [_row_spec(D_MODEL, rows.ctx_tile), _row_spec(D_MODEL, rows.lat_tile),
                _row_spec(ATTN_W, rows.ctx_tile), _row_spec(ATTN_W, rows.lat_tile),
                _row_spec(CONV_W),
                pl.BlockSpec((HALO, CONV_W), lambda i: (jnp.maximum(i * hb - 1, 0), 0)),
                pl.BlockSpec((HALO, CONV_W), lambda i: (jnp.minimum((i + 1) * hb, n_halo_blocks - 1), 0)),
                _row_spec(2 * D_MODEL), _mod_spec(rows),
                _const_spec((CONV_K, CONV_W)), _const_spec((1, CONV_W)), _const_spec((1, CONV_W)),
                _const_spec((1, CONV_W)), _const_spec((CONV_W, D_MODEL)), _const_spec((ATTN_W, D_MODEL)),
                _const_spec((D_MODEL, D_MODEL)), _const_spec((1, D_MODEL))]
    return pl.pallas_call(
        functools.partial(_merge_kernel, rows),
        grid=(rows.tiles,),
        in_specs=in_specs,
        out_specs=[_row_spec(D_MODEL), _row_spec(D_MODEL)],
        out_shape=[jax.ShapeDtypeStruct((n_rows, D_MODEL), F32), jax.ShapeDtypeStruct((n_rows, D_MODEL), BF16)],
        scratch_shapes=[pltpu.VMEM((ROW_TILE + 2 * HALO, CONV_W), F32),
                        pltpu.VMEM((SUBLANES, ROW_TILE + 2 * HALO - SUBLANES, CONV_W), F32)],
        compiler_params=_cparams(("arbitrary",)),
        name="conv_merge",
    )(xp, xs, on_p, on_s, z, z, z, gt, mod3, conv_w, conv_b, ln_g, ln_b, wco_b, wap_b, wout_b, norm2_g)


def _pack_cols(cols, shape):
    lane = lax.broadcasted_iota(jnp.int32, shape, 1)
    out = jnp.zeros(shape, F32)
    for j, col in enumerate(cols):
        out = jnp.where(lane == j, col, out)
    return out


def _router_kernel(h_ref, wr_ref, rb_ref, gate_ref, loc_ref, rec_ref, cnt_ref, run_ref):
    i = pl.program_id(0)

    @pl.when(i == 0)
    def _():
        run_ref[...] = jnp.zeros_like(run_ref)

    tm = ROW_TILE
    neg = jnp.float32(-jnp.inf)
    scores = jax.nn.sigmoid(_dot(h_ref[...], wr_ref[...]))
    biased = scores + rb_ref[...]
    lane_i = lax.broadcasted_iota(jnp.int32, scores.shape, 1)
    lane = lane_i.astype(F32)
    far = jnp.float32(2 * N_EXPERTS)

    def first_argmax(v):
        m = jnp.max(v, axis=-1, keepdims=True)
        return m, jnp.min(jnp.where(v == m, lane, far), axis=-1, keepdims=True)

    in_group, gscore = [], []
    for g in range(N_GROUPS):
        inb = (lane_i >= g * GROUP_SIZE) & (lane_i < (g + 1) * GROUP_SIZE)
        v = jnp.where(inb, biased, neg)
        m1, i1 = first_argmax(v)
        m2 = jnp.max(jnp.where(lane == i1, neg, v), axis=-1, keepdims=True)
        in_group.append(inb)
        gscore.append(m1 + m2)
    allowed = jnp.zeros(scores.shape, F32)
    for g in range(N_GROUPS):
        ahead = jnp.zeros((tm, 1), F32)
        for g2 in range(N_GROUPS):
            if g2 < g:
                ahead = ahead + (gscore[g2] >= gscore[g]).astype(F32)
            elif g2 > g:
                ahead = ahead + (gscore[g2] > gscore[g]).astype(F32)
        keep = (ahead < TOPK_GROUPS).astype(F32)
        allowed = jnp.where(in_group[g], keep, allowed)
    masked = jnp.where(allowed > 0.0, biased, neg)
    picked = jnp.zeros(scores.shape, F32)
    idxs, gates = [], []
    for _ in range(TOP_K):
        _, ik = first_argmax(masked)
        hit = lane == ik
        gates.append(jnp.sum(jnp.where(hit, scores, 0.0), axis=-1, keepdims=True))
        masked = jnp.where(hit, neg, masked)
        picked = jnp.where(hit, 1.0, picked)
        idxs.append(ik)
    gsum = functools.reduce(jnp.add, gates)
    gates = [g / gsum * ROUTE_SCALE for g in gates]
    r_i = lax.broadcasted_iota(jnp.int32, (tm, tm), 0)
    c_i = lax.broadcasted_iota(jnp.int32, (tm, tm), 1)
    before = (c_i < r_i).astype(BF16)
    local_rank = _dot(before, picked.astype(BF16))
    n_tok = jnp.sum(picked, axis=0, keepdims=True)
    n_chunks = jnp.floor((n_tok + (CHUNK - 1)) * (1.0 / CHUNK))
    lower = (lax.broadcasted_iota(jnp.int32, (N_EXPERTS, N_EXPERTS), 0)
             < lax.broadcasted_iota(jnp.int32, (N_EXPERTS, N_EXPERTS), 1)).astype(BF16)
    chunks_before = _dot(jnp.broadcast_to(n_chunks, (SUBLANES, N_EXPERTS)).astype(BF16), lower)[0:1]
    staged_off = chunks_before * CHUNK
    staged = local_rank + staged_off
    locs = [jnp.sum(jnp.where(lane == ik, staged, 0.0), axis=-1, keepdims=True) for ik in idxs]
    e_r = lax.broadcasted_iota(jnp.int32, (N_EXPERTS, N_EXPERTS), 0)
    e_c = lax.broadcasted_iota(jnp.int32, (N_EXPERTS, N_EXPERTS), 1)
    column = lambda v: jnp.sum(jnp.where(e_r == e_c, v, 0.0), axis=1, keepdims=True)
    first = column(chunks_before)
    last = first + column(n_chunks)
    region_off = column(run_ref[...] - staged_off)
    j = lax.broadcasted_iota(jnp.int32, (N_EXPERTS, REC_CHUNKS), 1).astype(F32)
    owns = (first <= j) & (j < last)
    expert = lax.broadcasted_iota(jnp.int32, (N_EXPERTS, REC_CHUNKS), 0).astype(F32)
    chunk_expert = jnp.sum(jnp.where(owns, expert, 0.0), axis=0, keepdims=True)
    chunk_off = jnp.sum(jnp.where(owns, region_off, 0.0), axis=0, keepdims=True) + j[0:1] * CHUNK
    total = jnp.broadcast_to(jnp.sum(n_chunks, axis=1, keepdims=True), (1, REC - 2 * REC_CHUNKS))
    rec_ref[0] = jnp.concatenate([chunk_expert, chunk_off, total], axis=1).astype(jnp.int32)
    run_ref[...] = run_ref[...] + jnp.floor((n_tok + (PAIR - 1)) * (1.0 / PAIR)) * PAIR
    cnt_ref[...] = run_ref[...]
    shape = gate_ref.shape
    gate_ref[...] = _pack_cols(gates, shape)
    loc_ref[...] = _pack_cols(locs, shape).astype(jnp.int32)


def _router(rows, h2, w_router_b, router_bias):
    n_rows = rows.tiles * ROW_TILE
    return pl.pallas_call(
        _router_kernel,
        grid=(rows.tiles,),
        in_specs=[_row_spec(D_MODEL), _const_spec((D_MODEL, N_EXPERTS)), _const_spec((1, N_EXPERTS))],
        out_specs=[_row_spec(LANES), _row_spec(LANES), pl.BlockSpec((1, 1, REC), lambda i: (i, 0, 0)),
                   _const_spec((1, N_EXPERTS))],
        out_shape=[jax.ShapeDtypeStruct((n_rows, LANES), F32), jax.ShapeDtypeStruct((n_rows, LANES), jnp.int32),
                   jax.ShapeDtypeStruct((rows.tiles, 1, REC), jnp.int32),
                   jax.ShapeDtypeStruct((1, N_EXPERTS), F32)],
        scratch_shapes=[pltpu.VMEM((1, N_EXPERTS), F32)],
        compiler_params=_cparams(("arbitrary",)),
        name="router",
    )(h2, w_router_b, router_bias)


def _record_copy(i, rec_hbm, rsm, sem_i, base=0):
    base = base if isinstance(base, int) else pl.multiple_of(base, REC)
    return pltpu.make_async_copy(rec_hbm.at[pl.ds(i * REC, REC)], rsm.at[pl.ds(base, REC)], sem_i)


def _load_record(i, rec_hbm, rsm, sem_i, base=0):
    cp = _record_copy(i, rec_hbm, rsm, sem_i, base)
    cp.start()
    cp.wait()


def _move_chunks(ps_ref, rsm, staged_ref, sorted_hbm, sem, chunks, to_sorted, base=0):
    total = rsm[base + 2 * REC_CHUNKS]
    spare0 = sorted_hbm.shape[0] // PACK - STAGE
    for j in chunks:
        placed_tok = jnp.where(j < total, ps_ref[rsm[base + j]] + rsm[base + REC_CHUNKS + j], spare0 + j * CHUNK)
        staged = staged_ref.at[pl.ds(j * CHUNK * PACK, CHUNK * PACK)]
        placed = sorted_hbm.at[_packed_rows(placed_tok, CHUNK, PAIR)]
        src, dst = (staged, placed) if to_sorted else (placed, staged)
        pltpu.make_async_copy(src, dst, sem).start()


def _wait_all_chunks(staged_ref, sorted_hbm, sem):
    pltpu.make_async_copy(staged_ref, sorted_hbm.at[pl.ds(0, STAGE * PACK)], sem).wait()


def _staged_block(loc, rb):
    col = lax.broadcasted_iota(jnp.int32, (loc.shape[0], MOE_BLOCK), 1) + rb * MOE_BLOCK
    return [loc[:, k:k + 1] == col for k in range(TOP_K)]


def _dispatch_kernel(pe_ref, ps_ref, c2_ref, rec_hbm, loc_ref, h_ref, xs_hbm, rsm, stage, zbuf, sem_i, sem):
    i = pl.program_id(0)
    block_rows = MOE_BLOCK * PACK

    @pl.when(i == 0)
    def _():
        zbuf[...] = jnp.zeros_like(zbuf)
        n_blocks = xs_hbm.shape[0] // block_rows
        n_used = pe_ref[N_EXPERTS - 1] // MOE_BLOCK

        def clear_block(b, carry):
            dst = xs_hbm.at[pl.ds(pl.multiple_of(b * block_rows, block_rows), block_rows)]
            pltpu.make_async_copy(zbuf, dst, sem.at[0]).start()
            return carry

        def clear_padding(e, total):
            first = (ps_ref[e] + c2_ref[e]) // MOE_BLOCK
            last = pe_ref[e] // MOE_BLOCK
            lax.fori_loop(first, last, clear_block, 0)
            return total + last - first

        def clear_wait(e, carry):
            pltpu.make_async_copy(zbuf, xs_hbm.at[pl.ds(0, block_rows)], sem.at[0]).wait()
            return carry

        n_cleared = lax.fori_loop(0, N_EXPERTS, clear_padding, 0)
        lax.fori_loop(n_used, n_blocks, clear_block, 0)
        lax.fori_loop(0, n_cleared + n_blocks - n_used, clear_wait, 0)

    _load_record(i, rec_hbm, rsm, sem_i)
    slot = i % 2
    loc, hb = loc_ref[...], h_ref[...]

    def stage_block(rb):
        onehot = jnp.zeros((ROW_TILE, MOE_BLOCK), F32)
        for hit in _staged_block(loc, rb):
            onehot = jnp.where(hit, 1.0, onehot)
        rows = lax.dot_general(onehot.astype(BF16), hb, (((0,), (0,)), ((), ())), preferred_element_type=F32)
        _store_packed(stage.at[slot], rb * MOE_BLOCK, rows)

    n_blocks, n_chunks = STAGE // MOE_BLOCK, STAGE // CHUNK
    n_early = n_blocks // 2 + 1
    for rb in range(n_early):
        stage_block(rb)

    @pl.when(i >= 1)
    def _():
        _wait_all_chunks(stage.at[1 - slot], xs_hbm, sem.at[1 - slot])

    per_round = -(-n_chunks // (n_blocks - n_early))
    for r, rb in enumerate(range(n_early, n_blocks)):
        stage_block(rb)
        chunks = range(r * per_round, min((r + 1) * per_round, n_chunks))
        assert (chunks[-1] * CHUNK) // MOE_BLOCK <= rb
        _move_chunks(ps_ref, rsm, stage.at[slot], xs_hbm, sem.at[slot], chunks, to_sorted=True)

    @pl.when(i == pl.num_programs(0) - 1)
    def _():
        _wait_all_chunks(stage.at[slot], xs_hbm, sem.at[slot])


def _dispatch(rows, pad_end, pad_start, counts, rec_flat, locs, h2, n_rows_sorted):
    return pl.pallas_call(
        _dispatch_kernel,
        grid_spec=pltpu.PrefetchScalarGridSpec(
            num_scalar_prefetch=3,
            grid=(rows.tiles,),
            in_specs=[pl.BlockSpec(memory_space=pl.ANY),
                      pl.BlockSpec((ROW_TILE, LANES), lambda i, pe, ps, c2: (i, 0)),
                      pl.BlockSpec((ROW_TILE, D_MODEL), lambda i, pe, ps, c2: (i, 0))],
            out_specs=pl.BlockSpec(memory_space=pl.ANY),
            scratch_shapes=[pltpu.SMEM((REC,), jnp.int32),
                            pltpu.VMEM((2, STAGE * PACK, LANES), jnp.uint32),
                            pltpu.VMEM((MOE_BLOCK * PACK, LANES), jnp.uint32),
                            pltpu.SemaphoreType.DMA, pltpu.SemaphoreType.DMA((2,))]),
        out_shape=jax.ShapeDtypeStruct(((n_rows_sorted + STAGE) * PACK, LANES), jnp.uint32),
        compiler_params=_cparams(("arbitrary",)),
        name="moe_dispatch",
    )(pad_end, pad_start, counts, rec_flat, locs, h2)


def _expert_kernel(pe_ref, ps_ref, xs_hbm, w1_ref, w3_ref, w2_ref, ys_hbm, xbuf, ybuf, w1b, w3b, w2b,
                   sem_in, sem_out):
    e = pl.program_id(0)
    first, last = ps_ref[e] // MOE_BLOCK, pe_ref[e] // MOE_BLOCK
    n_used = pe_ref[N_EXPERTS - 1] // MOE_BLOCK
    block_rows = MOE_BLOCK * PACK

    def rows_of(g):
        return pl.ds(pl.multiple_of(g * block_rows, block_rows), block_rows)

    def copy_in(g):
        slot = g % EXPERT_RING
        return pltpu.make_async_copy(xs_hbm.at[rows_of(g)], xbuf.at[slot], sem_in.at[slot])

    def copy_out(g):
        slot = g % EXPERT_RING
        return pltpu.make_async_copy(ybuf.at[slot], ys_hbm.at[rows_of(g)], sem_out.at[slot])

    @pl.when(e == 0)
    def _():
        for g in range(EXPERT_AHEAD):
            @pl.when(g < n_used)
            def _():
                copy_in(g).start()

    w1b[...] = w1_ref[...].astype(BF16)
    w3b[...] = w3_ref[...].astype(BF16)
    w2b[...] = w2_ref[...].astype(BF16)

    def block(g, carry):
        copy_in(g).wait()

        @pl.when(g + EXPERT_AHEAD < n_used)
        def _():
            copy_in(g + EXPERT_AHEAD).start()

        @pl.when(g >= EXPERT_RING)
        def _():
            copy_out(g - EXPERT_RING).wait()

        slot = g % EXPERT_RING
        xb = _load_packed(xbuf.at[slot], 0, MOE_BLOCK)
        hid = (_silu(_dot(xb, w1b[...])) * _dot(xb, w3b[...])).astype(BF16)
        _store_packed(ybuf.at[slot], 0, _dot(hid, w2b[...]).astype(BF16).astype(F32))
        copy_out(g).start()
        return carry

    lax.fori_loop(first, last, block, 0)

    @pl.when(e == N_EXPERTS - 1)
    def _():
        for back in range(EXPERT_RING, 0, -1):
            @pl.when(n_used >= back)
            def _():
                copy_out(n_used - back).wait()


def _experts(pad_end, pad_start, xs, w1, w3, w2):
    expert_spec = lambda shape: pl.BlockSpec((None,) + shape, lambda e, pe, ps: (e, 0, 0))
    any_spec = pl.BlockSpec(memory_space=pl.ANY)
    return pl.pallas_call(
        _expert_kernel,
        grid_spec=pltpu.PrefetchScalarGridSpec(
            num_scalar_prefetch=2,
            grid=(N_EXPERTS,),
            in_specs=[any_spec, expert_spec((D_MODEL, EXPERT_HIDDEN)), expert_spec((D_MODEL, EXPERT_HIDDEN)),
                      expert_spec((EXPERT_HIDDEN, D_MODEL))],
            out_specs=any_spec,
            scratch_shapes=[pltpu.VMEM((EXPERT_RING, MOE_BLOCK * PACK, LANES), jnp.uint32),
                            pltpu.VMEM((EXPERT_RING, MOE_BLOCK * PACK, LANES), jnp.uint32),
                            pltpu.VMEM((D_MODEL, EXPERT_HIDDEN), BF16),
                            pltpu.VMEM((D_MODEL, EXPERT_HIDDEN), BF16),
                            pltpu.VMEM((EXPERT_HIDDEN, D_MODEL), BF16),
                            pltpu.SemaphoreType.DMA((EXPERT_RING,)), pltpu.SemaphoreType.DMA((EXPERT_RING,))]),
        out_shape=jax.ShapeDtypeStruct(xs.shape, jnp.uint32),
        input_output_aliases={2: 0},
        compiler_params=_cparams(("arbitrary",)),
        name="moe_experts",
    )(pad_end, pad_start, xs, w1, w3, w2)


def _combine_kernel(rows, ps_ref, rec_hbm, ys_hbm, loc_ref, x1_ref, h_ref, gate_ref, mod_ref, ws1_ref, ws3_ref,
                    ws2_ref, nf_ref, outp_ref, outs_ref, rsm, ybuf, sem_i, sem):
    i = pl.program_id(0)
    slot = i % 2

    n_chunks = STAGE // CHUNK
    nxt = jnp.minimum(i + 1, pl.num_programs(0) - 1)
    other = 1 - slot

    def fetch(s, chunks):
        _move_chunks(ps_ref, rsm, ybuf.at[s], ys_hbm, sem.at[s], chunks, to_sorted=False, base=s * REC)

    next_record = _record_copy(nxt, rec_hbm, rsm, sem_i.at[other], base=other * REC)
    next_record.start()

    @pl.when(i == 0)
    def _():
        _load_record(i, rec_hbm, rsm, sem_i.at[slot], base=slot * REC)
        fetch(slot, range(n_chunks))

    hb = h_ref[...]
    shared = _dot((_silu(_dot(hb, ws1_ref[...])) * _dot(hb, ws3_ref[...])).astype(BF16), ws2_ref[...])
    staged_y = ybuf.at[slot]
    _wait_all_chunks(staged_y, ys_hbm, sem.at[slot])
    next_record.wait()
    loc, g = loc_ref[...], gate_ref[...]
    routed = jnp.zeros((ROW_TILE, D_MODEL), F32)
    for rb in range(STAGE // MOE_BLOCK):
        gm = jnp.zeros((ROW_TILE, MOE_BLOCK), F32)
        for k, hit in enumerate(_staged_block(loc, rb)):
            gm = jnp.where(hit, g[:, k:k + 1], gm)
        g_hi = gm.astype(BF16)
        g_lo = (gm - g_hi.astype(F32)).astype(BF16)
        yb = _load_packed(staged_y, rb * MOE_BLOCK, MOE_BLOCK)
        routed = routed + _dot(g_hi, yb) + _dot(g_lo, yb)
        per_block = MOE_BLOCK // CHUNK
        fetch(other, range(rb * per_block, (rb + 1) * per_block))
    gate2 = mod_ref[0][:, 5 * D_MODEL:6 * D_MODEL]
    x2 = x1_ref[...] + gate2 * (routed + shared)
    out = _rms(x2, nf_ref[...])

    @pl.when(i == pl.num_programs(0) - 1)
    def _():
        _wait_all_chunks(ybuf.at[other], ys_hbm, sem.at[other])

    @pl.when(i < rows.tiles_ctx)
    def _():
        outp_ref[...] = out

    @pl.when(i >= rows.tiles_ctx)
    def _():
        outs_ref[...] = out


def _combine(rows, pad_start, rec_flat, ys, locs, x1, h2, gates, mod3, ws1_b, ws3_b, ws2_b, normf_g):
    drop = lambda spec: pl.BlockSpec(spec.block_shape, lambda i, ps, f=spec.index_map: f(i))
    any_spec = pl.BlockSpec(memory_space=pl.ANY)
    return pl.pallas_call(
        functools.partial(_combine_kernel, rows),
        grid_spec=pltpu.PrefetchScalarGridSpec(
            num_scalar_prefetch=1,
            grid=(rows.tiles,),
            in_specs=[any_spec, any_spec]
                     + [drop(s) for s in (
                         _row_spec(LANES), _row_spec(D_MODEL), _row_spec(D_MODEL), _row_spec(LANES),
                         _mod_spec(rows),
                         _const_spec((D_MODEL, EXPERT_HIDDEN)), _const_spec((D_MODEL, EXPERT_HIDDEN)),
                         _const_spec((EXPERT_HIDDEN, D_MODEL)), _const_spec((1, D_MODEL)))],
            out_specs=[drop(_row_spec(D_MODEL, rows.ctx_tile)), drop(_row_spec(D_MODEL, rows.lat_tile))],
            scratch_shapes=[pltpu.SMEM((2 * REC,), jnp.int32),
                            pltpu.VMEM((2, STAGE * PACK, LANES), jnp.uint32),
                            pltpu.SemaphoreType.DMA((2,)), pltpu.SemaphoreType.DMA((2,))]),
        out_shape=[jax.ShapeDtypeStruct((rows.tiles_ctx * ROW_TILE, D_MODEL), F32),
                   jax.ShapeDtypeStruct((rows.tiles_lat * ROW_TILE, D_MODEL), F32)],
        compiler_params=_cparams(("arbitrary",)),
        name="moe_combine",
    )(pad_start, rec_flat, ys, locs, x1, h2, gates, mod3, ws1_b, ws3_b, ws2_b, normf_g)


def kernel(x_prompt, x_sample, cache_k, cache_v, c, c_ctx, w_mod, b_mod, norm1_g, w_in, lambda_qk, subln_g,
           w_attn_proj, conv_w, conv_b, conv_ln_g, conv_ln_b, w_conv_out, w_out, norm2_g, w_router,
           router_bias, w1, w3, w2, ws1, ws3, ws2, normf_g):
    batch, seq, _ = x_prompt.shape
    dec_batch, dec_seq, _ = x_sample.shape
    past = cache_k.shape[2]
    l = 0
    rows_p, rows_s = batch * seq, dec_batch * dec_seq
    n_rows = rows_p + rows_s
    rows = Rows(rows_p // ROW_TILE, rows_s // ROW_TILE, seq // ROW_TILE, dec_seq // ROW_TILE)
    xp = x_prompt.reshape(rows_p, D_MODEL)
    xs = x_sample.reshape(rows_s, D_MODEL)
    row = lambda a: a.reshape(1, -1)

    cc = jnp.zeros((MOD_ROWS, D_MODEL), F32).at[0].set(c_ctx).at[1:1 + dec_batch].set(c)
    mod3 = _mod_table(cc, w_mod[l], b_mod[l]).reshape(MOD_ROWS, 1, N_MOD * D_MODEL)

    q, k, v, z, gt, state_k, state_v = _inproj(rows, xp, xs, mod3, row(norm1_g[l]), w_in[l].astype(BF16),
                                               _rope_tables(dec_seq))
    lq, sg = lambda_qk[l], row(subln_g[l])
    on_p = _attention(q, k, v, lq, sg, 0, batch, seq, CTX_HEADS_PER_STEP)
    cache = (cache_k[:, l].reshape(dec_batch, past, ATTN_W), cache_v[:, l].reshape(dec_batch, past, ATTN_W))
    on_s = _attention(q, k, v, lq, sg, rows_p, dec_batch, dec_seq, LAT_HEADS_PER_STEP, cache=cache)
    x1, h2 = _merge(rows, xp, xs, on_p, on_s, z, gt, mod3, conv_w[l], row(conv_b[l]), row(conv_ln_g[l]),
                    row(conv_ln_b[l]), w_conv_out[l].astype(BF16), w_attn_proj[l].astype(BF16),
                    w_out[l].astype(BF16), row(norm2_g[l]))

    gates, locs, rec, counts = _router(rows, h2, w_router[l].astype(BF16), row(router_bias[l]))
    counts = counts[0].astype(jnp.int32)
    padded = (counts + CHUNK + MOE_BLOCK - 1) // MOE_BLOCK * MOE_BLOCK
    pad_end = jnp.cumsum(padded).astype(jnp.int32)
    pad_start = pad_end - padded
    max_rows = n_rows * TOP_K + rows.tiles * N_EXPERTS * (PAIR - 1) + N_EXPERTS * (CHUNK + MOE_BLOCK - 1)
    n_blocks = -(-max_rows // MOE_BLOCK)
    rec_flat = rec.reshape(-1)
    x_sorted = _dispatch(rows, pad_end, pad_start, counts, rec_flat, locs, h2, n_blocks * MOE_BLOCK)
    y_sorted = _experts(pad_end, pad_start, x_sorted, w1[l], w3[l], w2[l])
    y_p, y_s = _combine(rows, pad_start, rec_flat, y_sorted, locs, x1, h2, gates, mod3,
                        ws1[l].astype(BF16), ws3[l].astype(BF16), ws2[l].astype(BF16), row(normf_g))
    return (y_p.reshape(batch, seq, D_MODEL), y_s.reshape(dec_batch, dec_seq, D_MODEL),
            state_k.reshape(batch, 1, seq, N_HEADS, 2, HEAD_DIM),
            state_v.reshape(batch, 1, seq, N_HEADS, 2 * HEAD_DIM))
```

```python
import functools
import math
from typing import NamedTuple

import jax
import jax.numpy as jnp
from jax import lax
from jax.experimental import pallas as pl
from jax.experimental.pallas import tpu as pltpu

D_MODEL = 1024
GRID_W = 64
N_HEADS = 8
HEAD_DIM = 64
ATTN_W = N_HEADS * 2 * HEAD_DIM
CONV_W = 512
CONV_K = 31
N_EXPERTS = 256
TOP_K = 8
N_GROUPS = 8
TOPK_GROUPS = 4
GROUP_SIZE = N_EXPERTS // N_GROUPS
EXPERT_HIDDEN = 256
ROUTE_SCALE = 2.5
ROPE_BASE = 10000.0
EPS = 1e-6
N_MOD = 6
IN_COLS = 3 * ATTN_W + 2 * CONV_W + 2 * D_MODEL
LAM_INIT = 0.8 - 0.6 * math.exp(-0.3 * 0)
LOG2E = math.log2(math.e)

LANES = 128
SUBLANES = 8
VMEM_LIMIT = 56 * 1024 * 1024
HALO = 16
CONV_ROWS = 64
ROW_TILE = 256
CTX_HEADS_PER_STEP = 8
LAT_HEADS_PER_STEP = 2
ATTN_KEY_CHUNK = 512
MOE_BLOCK = 512
STAGE_BLOCK = 256
EXPERT_AHEAD = 3
EXPERT_RING = EXPERT_AHEAD + 1
MOD_ROWS = 8
MOD_COL_TILES = 4
PACK = D_MODEL // 2 // LANES
CHUNK = 4
PAIR = SUBLANES // PACK
STAGE = ROW_TILE * TOP_K + N_EXPERTS * (CHUNK - 1)
SPARE = -(-STAGE // MOE_BLOCK) * MOE_BLOCK
REC_CHUNKS = 768
REC = 2048
assert STAGE // CHUNK <= REC_CHUNKS and REC_CHUNKS % LANES == 0

BF16 = jnp.bfloat16
F32 = jnp.float32


class Rows(NamedTuple):
    tiles_ctx: int
    tiles_lat: int
    tiles_per_ctx: int
    tiles_per_lat: int

    @property
    def tiles(self):
        return self.tiles_ctx + self.tiles_lat

    def ctx_tile(self, i):
        return jnp.minimum(i, self.tiles_ctx - 1)

    def lat_tile(self, i):
        return jnp.maximum(i - self.tiles_ctx, 0)

    def mod_row(self, i):
        return jnp.where(i < self.tiles_ctx, 0, 1 + self.lat_tile(i) // self.tiles_per_lat)

    def seq_tile(self, i):
        is_ctx = i < self.tiles_ctx
        per = jnp.where(is_ctx, self.tiles_per_ctx, self.tiles_per_lat)
        return jnp.where(is_ctx, i % self.tiles_per_ctx, self.lat_tile(i) % self.tiles_per_lat), per


def _cparams(sem):
    return pltpu.CompilerParams(dimension_semantics=sem, vmem_limit_bytes=VMEM_LIMIT)


def _row_spec(width, tile=lambda i: i):
    return pl.BlockSpec((ROW_TILE, width), lambda i: (tile(i), 0))


def _const_spec(shape):
    return pl.BlockSpec(shape, lambda i: (0,) * len(shape))


def _mod_spec(rows):
    return pl.BlockSpec((1, 1, N_MOD * D_MODEL), lambda i: (rows.mod_row(i), 0, 0))


def _silu(x):
    return x * jax.nn.sigmoid(x)


def _dot(a, b):
    return jnp.dot(a, b, preferred_element_type=F32)


def _dot_nt(a, b):
    return lax.dot_general(a, b, (((1,), (1,)), ((), ())), preferred_element_type=F32)


def _rms(x, g):
    return x * lax.rsqrt(jnp.mean(x * x, axis=-1, keepdims=True) + EPS) * g


def _packed_rows(token, n_tokens, align):
    return pl.ds(pl.multiple_of(token * PACK, align * PACK), n_tokens * PACK)


def _store_packed(ref, token0, x):
    half = D_MODEL // 2
    lo = pltpu.bitcast(x[:, :half], jnp.uint32) >> 16
    hi = pltpu.bitcast(x[:, half:], jnp.uint32) & jnp.uint32(0xFFFF0000)
    w = lo | hi
    for s in range(PACK):
        ref[pl.ds(token0 * PACK + s, x.shape[0], stride=PACK), :] = w[:, s * LANES:(s + 1) * LANES]


def _load_packed(ref, token0, n_tokens):
    w = jnp.concatenate([ref[pl.ds(token0 * PACK + s, n_tokens, stride=PACK), :] for s in range(PACK)], axis=1)
    lo = pltpu.bitcast(w << 16, F32)
    hi = pltpu.bitcast(w & jnp.uint32(0xFFFF0000), F32)
    return jnp.concatenate([lo, hi], axis=1).astype(BF16)


def _mod_kernel(c_ref, w_ref, b_ref, o_ref):
    a = _silu(c_ref[...]).astype(BF16)
    o_ref[...] = _dot(a, w_ref[...].astype(BF16)) + b_ref[...]


def _mod_table(cc, w_mod, b_mod):
    n = w_mod.shape[1]
    tn = n // MOD_COL_TILES
    return pl.pallas_call(
        _mod_kernel,
        grid=(MOD_COL_TILES,),
        in_specs=[pl.BlockSpec((MOD_ROWS, D_MODEL), lambda j: (0, 0)),
                  pl.BlockSpec((D_MODEL, tn), lambda j: (0, j)),
                  pl.BlockSpec((1, tn), lambda j: (0, j))],
        out_specs=pl.BlockSpec((MOD_ROWS, tn), lambda j: (0, j)),
        out_shape=jax.ShapeDtypeStruct((MOD_ROWS, n), F32),
        compiler_params=_cparams(("arbitrary",)),
        name="mod_table",
    )(cc, w_mod, b_mod.reshape(1, n))


def _rope(x, cos, sin, lane_lo):
    swapped = jnp.where(lane_lo, pltpu.roll(x, LANES - 16, axis=1), pltpu.roll(x, 16, axis=1))
    return x * cos + swapped * sin


def _inproj_kernel(rows, xp_ref, xs_ref, mod_ref, g_ref, w_ref, cos_ref, sin_ref,
                   q_ref, k_ref, v_ref, z_ref, gt_ref, ks_ref, vs_ref):
    is_ctx = pl.program_id(0) < rows.tiles_ctx
    mod = mod_ref[0]
    shift, scale = mod[:, 0:D_MODEL], mod[:, D_MODEL:2 * D_MODEL]
    x = jnp.where(is_ctx, xp_ref[...], xs_ref[...])
    h = _rms(x, g_ref[...]) * (1.0 + scale) + shift
    hb = h.astype(BF16)
    q = _dot(hb, w_ref[:, 0:ATTN_W]) * (HEAD_DIM ** -0.5 * LOG2E)
    k = _dot(hb, w_ref[:, ATTN_W:2 * ATTN_W])
    v = _dot(hb, w_ref[:, 2 * ATTN_W:3 * ATTN_W])

    cos, sin = cos_ref[...], sin_ref[...]
    lane_lo = (lax.broadcasted_iota(jnp.int32, cos.shape, 1) % 32) < 16
    for j in range(ATTN_W // LANES):
        sl = slice(j * LANES, (j + 1) * LANES)
        q_ref[:, sl] = jnp.where(is_ctx, q[:, sl], _rope(q[:, sl], cos, sin, lane_lo)).astype(BF16)
        k_ref[:, sl] = jnp.where(is_ctx, k[:, sl], _rope(k[:, sl], cos, sin, lane_lo)).astype(BF16)

    v_ref[...] = v.astype(BF16)
    u = _dot(hb, w_ref[:, 3 * ATTN_W:3 * ATTN_W + 2 * CONV_W])
    z_ref[...] = u[:, :CONV_W] * jax.nn.sigmoid(u[:, CONV_W:])
    gt_ref[...] = jax.nn.sigmoid(_dot(hb, w_ref[:, 3 * ATTN_W + 2 * CONV_W:]))

    @pl.when(is_ctx)
    def _():
        ks_ref[...] = k
        vs_ref[...] = v


def _inproj(rows, xp, xs, mod3, norm_g, w_in_b, rope):
    n_rows = rows.tiles * ROW_TILE
    rows_p = rows.tiles_ctx * ROW_TILE
    rope_spec = pl.BlockSpec((ROW_TILE, LANES), lambda i: (rows.lat_tile(i) % rows.tiles_per_lat, 0))
    shapes = lambda w, dt, r=n_rows: jax.ShapeDtypeStruct((r, w), dt)
    return pl.pallas_call(
        functools.partial(_inproj_kernel, rows),
        grid=(rows.tiles,),
        in_specs=[_row_spec(D_MODEL, rows.ctx_tile), _row_spec(D_MODEL, rows.lat_tile), _mod_spec(rows),
                  _const_spec((1, D_MODEL)),
                  pl.BlockSpec((D_MODEL, IN_COLS), lambda i: (0, 0), pipeline_mode=pl.Buffered(1)),
                  rope_spec, rope_spec],
        out_specs=[_row_spec(ATTN_W)] * 3 + [_row_spec(CONV_W), _row_spec(2 * D_MODEL)]
                  + [_row_spec(ATTN_W, rows.ctx_tile)] * 2,
        out_shape=[shapes(ATTN_W, BF16)] * 3 + [shapes(CONV_W, F32), shapes(2 * D_MODEL, F32)]
                  + [shapes(ATTN_W, F32, rows_p)] * 2,
        compiler_params=_cparams(("arbitrary",)),
        name="inproj",
    )(xp, xs, mod3, norm_g, w_in_b, *rope)


def _rope_tables(n_tokens):
    t = jnp.arange(n_tokens, dtype=jnp.int32)
    pos = jnp.stack([t // GRID_W, t % GRID_W], axis=-1).astype(F32)
    half = HEAD_DIM // 2
    inv = ROPE_BASE ** (-jnp.arange(0, half, 2, dtype=F32) / half)
    ang = pos[:, :, None] * inv
    cos, sin = jnp.cos(ang), jnp.sin(ang)
    cos64 = jnp.concatenate([cos[:, 0], cos[:, 0], cos[:, 1], cos[:, 1]], axis=-1)
    sin64 = jnp.concatenate([-sin[:, 0], sin[:, 0], -sin[:, 1], sin[:, 1]], axis=-1)
    return jnp.tile(cos64, (1, LANES // HEAD_DIM)), jnp.tile(sin64, (1, LANES // HEAD_DIM))


def _lane_groups(x):
    return [x[:, j * LANES:(j + 1) * LANES] for j in range(x.shape[1] // LANES)]


def _attn_kernel(has_cache, heads, lq_ref, sg_ref, q_ref, k_ref, v_ref, *rest):
    if has_cache:
        ck_ref, cv_ref, o_ref, *bufs = rest
    else:
        o_ref, *bufs = rest

    def s_ref(u, mp):
        return bufs[(u % 2) * 2 + mp]

    lq = lq_ref[...]
    lam = (jnp.exp(jnp.sum(lq[0:1] * lq[1:2], axis=-1, keepdims=True))
           - jnp.exp(jnp.sum(lq[2:3] * lq[3:4], axis=-1, keepdims=True)) + LAM_INIT)
    tq, seq = q_ref.shape[0], k_ref.shape[0]
    chunks = [(off, min(ATTN_KEY_CHUNK, seq - off), False) for off in range(0, seq, ATTN_KEY_CHUNK)]
    if has_cache:
        chunks.append((seq, ck_ref.shape[0], True))
    first = lax.broadcasted_iota(jnp.int32, (tq, LANES), 1) < HEAD_DIM
    neg = jnp.full((tq, LANES), -jnp.inf, F32)
    heads_state = [dict() for _ in range(heads)]

    def lanes(u):
        return slice(u * LANES, (u + 1) * LANES)

    def load(main_ref, cache_ref, u, chunk):
        off, size, cached = chunk
        if cached:
            return cache_ref[:, lanes(u)].astype(BF16)
        return main_ref[off:off + size, lanes(u)]

    def scores(u, chunk):
        st = heads_state[u]
        if "q" not in st:
            q = q_ref[:, lanes(u)]
            zero = jnp.zeros_like(q)
            st["q"] = (jnp.where(first, q, zero), jnp.where(first, zero, q))
            st["macc"] = [neg, neg]
        kk = load(k_ref, ck_ref if has_cache else None, u, chunk)
        off, size, _ = chunk
        for mp in range(2):
            s = _dot_nt(st["q"][mp], kk)
            s_ref(u, mp)[:, off:off + size] = s
            st["macc"][mp] = functools.reduce(jnp.maximum, _lane_groups(s), st["macc"][mp])

    def exps(u, chunk):
        st = heads_state[u]
        if "m" not in st:
            st["m"] = [jnp.max(a, axis=-1, keepdims=True) for a in st["macc"]]
            st["lacc"] = [jnp.zeros((tq, LANES), F32)] * 2
        off, size, _ = chunk
        for mp in range(2):
            e = jnp.exp2(s_ref(u, mp)[:, off:off + size] - st["m"][mp])
            s_ref(u, mp)[:, off:off + size] = e
            st["lacc"][mp] = functools.reduce(jnp.add, _lane_groups(e), st["lacc"][mp])

    def values(u, chunk):
        st = heads_state[u]
        if "r" not in st:
            l1, l2 = [jnp.sum(a, axis=-1, keepdims=True) for a in st["lacc"]]
            st["r"] = (1.0 / l1, lam / l2)
            st["o"] = jnp.zeros((tq, LANES), F32)
        off, size, _ = chunk
        w = s_ref(u, 0)[:, off:off + size] * st["r"][0] - s_ref(u, 1)[:, off:off + size] * st["r"][1]
        st["o"] = st["o"] + _dot(w.astype(BF16), load(v_ref, cv_ref if has_cache else None, u, chunk))

    stages = (scores, exps, values)
    for phase in range(heads + len(stages) - 1):
        for chunk in chunks:
            for s in reversed(range(len(stages))):
                u = phase - s
                if 0 <= u < heads:
                    stages[s](u, chunk)
        u = phase - (len(stages) - 1)
        if 0 <= u < heads:
            o_ref[:, lanes(u)] = (_rms(heads_state[u]["o"], sg_ref[...]) * (1.0 - LAM_INIT)).astype(BF16)


def _attention(q, k, v, lambda_qk, subln_g, row0, batch, seq_len, heads, cache=None):
    tq = ROW_TILE
    q_tiles = seq_len // tq
    assert row0 % seq_len == 0
    seq0 = row0 // seq_len
    width = heads * LANES
    head_q = pl.BlockSpec((tq, width), lambda b, h, i: (seq0 * q_tiles + b * q_tiles + i, h))
    head_kv = pl.BlockSpec((seq_len, width), lambda b, h, i: (seq0 + b, h))
    in_specs = [pl.BlockSpec((4, HEAD_DIM), lambda b, h, i: (0, 0)),
                pl.BlockSpec((1, 2 * HEAD_DIM), lambda b, h, i: (0, 0)),
                head_q, head_kv, head_kv]
    args = [lambda_qk, subln_g, q, k, v]
    n_keys = seq_len
    if cache is not None:
        past = cache[0].shape[1]
        n_keys += past
        head_cache = pl.BlockSpec((None, past, width), lambda b, h, i: (b, 0, h))
        in_specs += [head_cache, head_cache]
        args += list(cache)
    return pl.pallas_call(
        functools.partial(_attn_kernel, cache is not None, heads),
        grid=(batch, N_HEADS // heads, q_tiles),
        in_specs=in_specs,
        out_specs=pl.BlockSpec((tq, width), lambda b, h, i: (b * q_tiles + i, h)),
        out_shape=jax.ShapeDtypeStruct((batch * seq_len, ATTN_W), BF16),
        scratch_shapes=[pltpu.VMEM((tq, n_keys), F32)] * 4,
        compiler_params=_cparams(("parallel", "parallel", "arbitrary")),
        name="attn_latent" if cache is not None else "attn_ctx",
    )(*args)


def _merge_kernel(rows, xp_ref, xs_ref, onp_ref, ons_ref, z_ref, zp_ref, zn_ref, gt_ref, mod_ref, cw_ref, cb_ref,
                  lg_ref, lb_ref, wco_ref, wap_ref, wout_ref, n2_ref, x1_ref, h2_ref, zext_ref, zsh_ref):
    i = pl.program_id(0)
    is_ctx = i < rows.tiles_ctx
    tm = ROW_TILE
    t, per = rows.seq_tile(i)
    zext_ref[HALO:HALO + tm, :] = z_ref[...]
    zext_ref[0:HALO, :] = jnp.where(t == 0, 0.0, zp_ref[...])
    zext_ref[HALO + tm:, :] = jnp.where(t == per - 1, 0.0, zn_ref[...])
    pad = HALO - CONV_K // 2
    shifted_rows = zsh_ref.shape[1]
    for s in range(SUBLANES):
        zsh_ref[s] = zext_ref[s:s + shifted_rows, :]
    col_blocks = []
    for cb in range(CONV_W // LANES):
        cs = slice(cb * LANES, (cb + 1) * LANES)
        row_chunks = []
        for r0 in range(0, tm, CONV_ROWS):
            acc = jnp.zeros((CONV_ROWS, LANES), F32) + cb_ref[:, cs]
            for j in range(CONV_K):
                phase, base = (j + pad) % SUBLANES, r0 + (j + pad) // SUBLANES * SUBLANES
                acc = acc + zsh_ref[phase, base:base + CONV_ROWS, cs] * cw_ref[j:j + 1, cs]
            row_chunks.append(acc)
        col_blocks.append(jnp.concatenate(row_chunks, axis=0))
    c = jnp.concatenate(col_blocks, axis=1)
    mu = jnp.mean(c, axis=-1, keepdims=True)
    cc = c - mu
    y = cc * lax.rsqrt(jnp.mean(cc * cc, axis=-1, keepdims=True) + EPS) * lg_ref[...] + lb_ref[...]
    conv_out = _dot(_silu(y).astype(BF16), wco_ref[...])
    o_n = jnp.where(is_ctx, onp_ref[...], ons_ref[...])
    a_br = _dot(o_n, wap_ref[...])
    gt = gt_ref[...]
    merged = gt[:, :D_MODEL] * a_br + gt[:, D_MODEL:] * conv_out
    mod = mod_ref[0]
    gate1 = mod[:, 2 * D_MODEL:3 * D_MODEL]
    shift2, scale2 = mod[:, 3 * D_MODEL:4 * D_MODEL], mod[:, 4 * D_MODEL:5 * D_MODEL]
    x = jnp.where(is_ctx, xp_ref[...], xs_ref[...])
    x1 = x + gate1 * _dot(merged.astype(BF16), wout_ref[...])
    x1_ref[...] = x1
    h2_ref[...] = (_rms(x1, n2_ref[...]) * (1.0 + scale2) + shift2).astype(BF16)


def _merge(rows, xp, xs, on_p, on_s, z, gt, mod3, conv_w, conv_b, ln_g, ln_b, wco_b, wap_b, wout_b, norm2_g):
    n_rows = rows.tiles * ROW_TILE
    hb = ROW_TILE // HALO
    n_halo_blocks = n_rows // HALO
    in_specs = [_row_spec(D_MODEL, rows.ctx_tile), _row_spec(D_MODEL, rows.lat_tile),
                _row_spec(ATTN_W, rows.ctx_tile), _row_spec(ATTN_W, rows.lat_tile),
                _row_spec(CONV_W),
                pl.BlockSpec((HALO, CONV_W), lambda i: (jnp.maximum(i * hb - 1, 0), 0)),
                pl.BlockSpec((HALO, CONV_W), lambda i: (jnp.minimum((i + 1) * hb, n_halo_blocks - 1), 0)),
                _row_spec(2 * D_MODEL), _mod_spec(rows),
                _const_spec((CONV_K, CONV_W)), _const_spec((1, CONV_W)), _const_spec((1, CONV_W)),
                _const_spec((1, CONV_W)), _const_spec((CONV_W, D_MODEL)), _const_spec((ATTN_W, D_MODEL)),
                _const_spec((D_MODEL, D_MODEL)), _const_spec((1, D_MODEL))]
    return pl.pallas_call(
        functools.partial(_merge_kernel, rows),
        grid=(rows.tiles,),
        in_specs=in_specs,
        out_specs=[_row_spec(D_MODEL), _row_spec(D_MODEL)],
        out_shape=[jax.ShapeDtypeStruct((n_rows, D_MODEL), F32), jax.ShapeDtypeStruct((n_rows, D_MODEL), BF16)],
        scratch_shapes=[pltpu.VMEM((ROW_TILE + 2 * HALO, CONV_W), F32),
                        pltpu.VMEM((SUBLANES, ROW_TILE + 2 * HALO - SUBLANES, CONV_W), F32)],
        compiler_params=_cparams(("arbitrary",)),
        name="conv_merge",
    )(xp, xs, on_p, on_s, z, z, z, gt, mod3, conv_w, conv_b, ln_g, ln_b, wco_b, wap_b, wout_b, norm2_g)


def _pack_cols(cols, shape):
    lane = lax.broadcasted_iota(jnp.int32, shape, 1)
    out = jnp.zeros(shape, F32)
    for j, col in enumerate(cols):
        out = jnp.where(lane == j, col, out)
    return out


def _router_kernel(h_ref, wr_ref, rb_ref, gate_ref, loc_ref, rec_ref, cnt_ref, run_ref):
    i = pl.program_id(0)

    @pl.when(i == 0)
    def _():
        run_ref[...] = jnp.zeros_like(run_ref)

    tm = ROW_TILE
    neg = jnp.float32(-jnp.inf)
    scores = jax.nn.sigmoid(_dot(h_ref[...], wr_ref[...]))
    biased = scores + rb_ref[...]
    lane_i = lax.broadcasted_iota(jnp.int32, scores.shape, 1)
    lane = lane_i.astype(F32)
    far = jnp.float32(2 * N_EXPERTS)

    def first_argmax(v):
        m = jnp.max(v, axis=-1, keepdims=True)
        return m, jnp.min(jnp.where(v == m, lane, far), axis=-1, keepdims=True)

    in_group, gscore = [], []
    for g in range(N_GROUPS):
        inb = (lane_i >= g * GROUP_SIZE) & (lane_i < (g + 1) * GROUP_SIZE)
        first_lane = g * GROUP_SIZE // LANES * LANES
        part = slice(first_lane, first_lane + LANES)
        lane_p = (lax.broadcasted_iota(jnp.int32, (tm, LANES), 1) + first_lane).astype(F32)
        in_part = (lane_p >= g * GROUP_SIZE) & (lane_p < (g + 1) * GROUP_SIZE)
        v = jnp.where(in_part, biased[:, part], neg)
        m1 = jnp.max(v, axis=-1, keepdims=True)
        i1 = jnp.min(jnp.where(v == m1, lane_p, far), axis=-1, keepdims=True)
        m2 = jnp.max(jnp.where(lane_p == i1, neg, v), axis=-1, keepdims=True)
        in_group.append(inb)
        gscore.append(m1 + m2)
    allowed = jnp.zeros(scores.shape, F32)
    for g in range(N_GROUPS):
        ahead = jnp.zeros((tm, 1), F32)
        for g2 in range(N_GROUPS):
            if g2 < g:
                ahead = ahead + (gscore[g2] >= gscore[g]).astype(F32)
            elif g2 > g:
                ahead = ahead + (gscore[g2] > gscore[g]).astype(F32)
        keep = (ahead < TOPK_GROUPS).astype(F32)
        allowed = jnp.where(in_group[g], keep, allowed)
    masked = jnp.where(allowed > 0.0, biased, neg)
    picked = jnp.zeros(scores.shape, F32)
    idxs, gates = [], []
    for _ in range(TOP_K):
        _, ik = first_argmax(masked)
        hit = lane == ik
        gates.append(jnp.sum(jnp.where(hit, scores, 0.0), axis=-1, keepdims=True))
        masked = jnp.where(hit, neg, masked)
        picked = jnp.where(hit, 1.0, picked)
        idxs.append(ik)
    gsum = functools.reduce(jnp.add, gates)
    gates = [g / gsum * ROUTE_SCALE for g in gates]
    r_i = lax.broadcasted_iota(jnp.int32, (tm, tm), 0)
    c_i = lax.broadcasted_iota(jnp.int32, (tm, tm), 1)
    before = (c_i < r_i).astype(BF16)
    local_rank = _dot(before, picked.astype(BF16))
    n_tok = jnp.sum(picked, axis=0, keepdims=True)
    n_chunks = jnp.floor((n_tok + (CHUNK - 1)) * (1.0 / CHUNK))
    lower = (lax.broadcasted_iota(jnp.int32, (N_EXPERTS, N_EXPERTS), 0)
             < lax.broadcasted_iota(jnp.int32, (N_EXPERTS, N_EXPERTS), 1)).astype(BF16)
    chunks_before = _dot(jnp.broadcast_to(n_chunks, (SUBLANES, N_EXPERTS)).astype(BF16), lower)[0:1]
    staged_off = chunks_before * CHUNK
    staged = local_rank + staged_off
    locs = [jnp.sum(jnp.where(lane == ik, staged, 0.0), axis=-1, keepdims=True) for ik in idxs]
    e_r = lax.broadcasted_iota(jnp.int32, (N_EXPERTS, N_EXPERTS), 0)
    e_c = lax.broadcasted_iota(jnp.int32, (N_EXPERTS, N_EXPERTS), 1)
    column = lambda v: jnp.sum(jnp.where(e_r == e_c, v, 0.0), axis=1, keepdims=True)
    first = column(chunks_before)
    last = first + column(n_chunks)
    region_off = column(run_ref[...] - staged_off)
    j = lax.broadcasted_iota(jnp.int32, (N_EXPERTS, REC_CHUNKS), 1).astype(F32)
    owns = (first <= j) & (j < last)
    expert = lax.broadcasted_iota(jnp.int32, (N_EXPERTS, REC_CHUNKS), 0).astype(F32)
    chunk_expert = jnp.sum(jnp.where(owns, expert, 0.0), axis=0, keepdims=True)
    chunk_off = jnp.sum(jnp.where(owns, region_off, 0.0), axis=0, keepdims=True) + j[0:1] * CHUNK
    total = jnp.broadcast_to(jnp.sum(n_chunks, axis=1, keepdims=True), (1, REC - 2 * REC_CHUNKS))
    rec_ref[0] = jnp.concatenate([chunk_expert, chunk_off, total], axis=1).astype(jnp.int32)
    run_ref[...] = run_ref[...] + jnp.floor((n_tok + (PAIR - 1)) * (1.0 / PAIR)) * PAIR
    cnt_ref[...] = run_ref[...]
    shape = gate_ref.shape
    gate_ref[...] = _pack_cols(gates, shape)
    loc_ref[...] = _pack_cols(locs, shape).astype(jnp.int32)


def _router(rows, h2, w_router_b, router_bias):
    n_rows = rows.tiles * ROW_TILE
    return pl.pallas_call(
        _router_kernel,
        grid=(rows.tiles,),
        in_specs=[_row_spec(D_MODEL), _const_spec((D_MODEL, N_EXPERTS)), _const_spec((1, N_EXPERTS))],
        out_specs=[_row_spec(LANES), _row_spec(LANES), pl.BlockSpec((1, 1, REC), lambda i: (i, 0, 0)),
                   _const_spec((1, N_EXPERTS))],
        out_shape=[jax.ShapeDtypeStruct((n_rows, LANES), F32), jax.ShapeDtypeStruct((n_rows, LANES), jnp.int32),
                   jax.ShapeDtypeStruct((rows.tiles, 1, REC), jnp.int32),
                   jax.ShapeDtypeStruct((1, N_EXPERTS), F32)],
        scratch_shapes=[pltpu.VMEM((1, N_EXPERTS), F32)],
        compiler_params=_cparams(("arbitrary",)),
        name="router",
    )(h2, w_router_b, router_bias)


def _record_copy(i, rec_hbm, rsm, sem_i, base=0):
    base = base if isinstance(base, int) else pl.multiple_of(base, REC)
    return pltpu.make_async_copy(rec_hbm.at[pl.ds(i * REC, REC)], rsm.at[pl.ds(base, REC)], sem_i)


def _load_record(i, rec_hbm, rsm, sem_i, base=0):
    cp = _record_copy(i, rec_hbm, rsm, sem_i, base)
    cp.start()
    cp.wait()


def _move_chunks(ps_ref, rsm, staged_ref, sorted_hbm, sem, chunks, to_sorted, base=0):
    total = rsm[base + 2 * REC_CHUNKS]
    spare0 = sorted_hbm.shape[0] // PACK - SPARE
    for j in chunks:
        placed_tok = jnp.where(j < total, ps_ref[rsm[base + j]] + rsm[base + REC_CHUNKS + j], spare0 + j * CHUNK)
        staged = staged_ref.at[pl.ds(j * CHUNK * PACK, CHUNK * PACK)]
        placed = sorted_hbm.at[_packed_rows(placed_tok, CHUNK, PAIR)]
        src, dst = (staged, placed) if to_sorted else (placed, staged)
        pltpu.make_async_copy(src, dst, sem).start()


def _wait_all_chunks(staged_ref, sorted_hbm, sem):
    pltpu.make_async_copy(staged_ref, sorted_hbm.at[pl.ds(0, STAGE * PACK)], sem).wait()


def _staged_block(loc, rb):
    col = lax.broadcasted_iota(jnp.int32, (loc.shape[0], STAGE_BLOCK), 1) + rb * STAGE_BLOCK
    return [loc[:, k:k + 1] == col for k in range(TOP_K)]


def _dispatch_kernel(pe_ref, ps_ref, c2_ref, rec_hbm, loc_ref, h_ref, xs_hbm, rsm, stage, zbuf, sem_i, sem):
    i = pl.program_id(0)
    block_rows = MOE_BLOCK * PACK

    @pl.when(i == 0)
    def _():
        zbuf[...] = jnp.zeros_like(zbuf)
        n_blocks = xs_hbm.shape[0] // block_rows
        n_used = pe_ref[N_EXPERTS - 1] // MOE_BLOCK

        def clear_block(b, carry):
            dst = xs_hbm.at[pl.ds(pl.multiple_of(b * block_rows, block_rows), block_rows)]
            pltpu.make_async_copy(zbuf, dst, sem.at[0]).start()
            return carry

        def clear_padding(e, total):
            first = (ps_ref[e] + c2_ref[e]) // MOE_BLOCK
            last = pe_ref[e] // MOE_BLOCK
            lax.fori_loop(first, last, clear_block, 0)
            return total + last - first

        def clear_wait(e, carry):
            pltpu.make_async_copy(zbuf, xs_hbm.at[pl.ds(0, block_rows)], sem.at[0]).wait()
            return carry

        n_cleared = lax.fori_loop(0, N_EXPERTS, clear_padding, 0)
        lax.fori_loop(n_used, n_blocks, clear_block, 0)
        lax.fori_loop(0, n_cleared + n_blocks - n_used, clear_wait, 0)

    _load_record(i, rec_hbm, rsm, sem_i)
    slot = i % 2
    loc, hb = loc_ref[...], h_ref[...]

    def stage_block(rb):
        onehot = jnp.zeros((ROW_TILE, STAGE_BLOCK), F32)
        for hit in _staged_block(loc, rb):
            onehot = jnp.where(hit, 1.0, onehot)
        rows = lax.dot_general(onehot.astype(BF16), hb, (((0,), (0,)), ((), ())), preferred_element_type=F32)
        _store_packed(stage.at[slot], rb * STAGE_BLOCK, rows)

    n_blocks, n_chunks = STAGE // STAGE_BLOCK, STAGE // CHUNK
    n_early = n_blocks // 2 + 1
    for rb in range(n_early):
        stage_block(rb)

    @pl.when(i >= 1)
    def _():
        _wait_all_chunks(stage.at[1 - slot], xs_hbm, sem.at[1 - slot])

    per_round = -(-n_chunks // (n_blocks - n_early))
    for r, rb in enumerate(range(n_early, n_blocks)):
        stage_block(rb)
        chunks = range(r * per_round, min((r + 1) * per_round, n_chunks))
        assert (chunks[-1] * CHUNK) // STAGE_BLOCK <= rb
        _move_chunks(ps_ref, rsm, stage.at[slot], xs_hbm, sem.at[slot], chunks, to_sorted=True)

    @pl.when(i == pl.num_programs(0) - 1)
    def _():
        _wait_all_chunks(stage.at[slot], xs_hbm, sem.at[slot])


def _dispatch(rows, pad_end, pad_start, counts, rec_flat, locs, h2, n_rows_sorted):
    return pl.pallas_call(
        _dispatch_kernel,
        grid_spec=pltpu.PrefetchScalarGridSpec(
            num_scalar_prefetch=3,
            grid=(rows.tiles,),
            in_specs=[pl.BlockSpec(memory_space=pl.ANY),
                      pl.BlockSpec((ROW_TILE, LANES), lambda i, pe, ps, c2: (i, 0)),
                      pl.BlockSpec((ROW_TILE, D_MODEL), lambda i, pe, ps, c2: (i, 0))],
            out_specs=pl.BlockSpec(memory_space=pl.ANY),
            scratch_shapes=[pltpu.SMEM((REC,), jnp.int32),
                            pltpu.VMEM((2, STAGE * PACK, LANES), jnp.uint32),
                            pltpu.VMEM((MOE_BLOCK * PACK, LANES), jnp.uint32),
                            pltpu.SemaphoreType.DMA, pltpu.SemaphoreType.DMA((2,))]),
        out_shape=jax.ShapeDtypeStruct(((n_rows_sorted + SPARE) * PACK, LANES), jnp.uint32),
        compiler_params=_cparams(("arbitrary",)),
        name="moe_dispatch",
    )(pad_end, pad_start, counts, rec_flat, locs, h2)


def _expert_kernel(pe_ref, ps_ref, xs_hbm, w1_ref, w3_ref, w2_ref, ys_hbm, xbuf, ybuf, w1b, w3b, w2b,
                   sem_in, sem_out):
    e = pl.program_id(0)
    first, last = ps_ref[e] // MOE_BLOCK, pe_ref[e] // MOE_BLOCK
    n_used = pe_ref[N_EXPERTS - 1] // MOE_BLOCK
    block_rows = MOE_BLOCK * PACK

    def rows_of(g):
        return pl.ds(pl.multiple_of(g * block_rows, block_rows), block_rows)

    def copy_in(g):
        slot = g % EXPERT_RING
        return pltpu.make_async_copy(xs_hbm.at[rows_of(g)], xbuf.at[slot], sem_in.at[slot])

    def copy_out(g):
        slot = g % EXPERT_RING
        return pltpu.make_async_copy(ybuf.at[slot], ys_hbm.at[rows_of(g)], sem_out.at[slot])

    @pl.when(e == 0)
    def _():
        for g in range(EXPERT_AHEAD):
            @pl.when(g < n_used)
            def _():
                copy_in(g).start()

    w1b[...] = w1_ref[...].astype(BF16)
    w3b[...] = w3_ref[...].astype(BF16)
    w2b[...] = w2_ref[...].astype(BF16)

    def block(g, carry):
        copy_in(g).wait()

        @pl.when(g + EXPERT_AHEAD < n_used)
        def _():
            copy_in(g + EXPERT_AHEAD).start()

        @pl.when(g >= EXPERT_RING)
        def _():
            copy_out(g - EXPERT_RING).wait()

        slot = g % EXPERT_RING
        parts = range(0, MOE_BLOCK, STAGE_BLOCK)
        xb = jnp.concatenate([_load_packed(xbuf.at[slot], t0, STAGE_BLOCK) for t0 in parts], axis=0)
        hid = (_silu(_dot(xb, w1b[...])) * _dot(xb, w3b[...])).astype(BF16)
        y = _dot(hid, w2b[...]).astype(BF16).astype(F32)
        for t0 in parts:
            _store_packed(ybuf.at[slot], t0, y[t0:t0 + STAGE_BLOCK])
        copy_out(g).start()
        return carry

    lax.fori_loop(first, last, block, 0)

    @pl.when(e == N_EXPERTS - 1)
    def _():
        for back in range(EXPERT_RING, 0, -1):
            @pl.when(n_used >= back)
            def _():
                copy_out(n_used - back).wait()


def _experts(pad_end, pad_start, xs, w1, w3, w2):
    expert_spec = lambda shape: pl.BlockSpec((None,) + shape, lambda e, pe, ps: (e, 0, 0))
    any_spec = pl.BlockSpec(memory_space=pl.ANY)
    return pl.pallas_call(
        _expert_kernel,
        grid_spec=pltpu.PrefetchScalarGridSpec(
            num_scalar_prefetch=2,
            grid=(N_EXPERTS,),
            in_specs=[any_spec, expert_spec((D_MODEL, EXPERT_HIDDEN)), expert_spec((D_MODEL, EXPERT_HIDDEN)),
                      expert_spec((EXPERT_HIDDEN, D_MODEL))],
            out_specs=any_spec,
            scratch_shapes=[pltpu.VMEM((EXPERT_RING, MOE_BLOCK * PACK, LANES), jnp.uint32),
                            pltpu.VMEM((EXPERT_RING, MOE_BLOCK * PACK, LANES), jnp.uint32),
                            pltpu.VMEM((D_MODEL, EXPERT_HIDDEN), BF16),
                            pltpu.VMEM((D_MODEL, EXPERT_HIDDEN), BF16),
                            pltpu.VMEM((EXPERT_HIDDEN, D_MODEL), BF16),
                            pltpu.SemaphoreType.DMA((EXPERT_RING,)), pltpu.SemaphoreType.DMA((EXPERT_RING,))]),
        out_shape=jax.ShapeDtypeStruct(xs.shape, jnp.uint32),
        input_output_aliases={2: 0},
        compiler_params=_cparams(("arbitrary",)),
        name="moe_experts",
    )(pad_end, pad_start, xs, w1, w3, w2)


def _combine_kernel(rows, ps_ref, rec_hbm, ys_hbm, loc_ref, x1_ref, h_ref, gate_ref, mod_ref, ws1_ref, ws3_ref,
                    ws2_ref, nf_ref, outp_ref, outs_ref, rsm, ybuf, sem_i, sem):
    i = pl.program_id(0)
    slot = i % 2

    n_chunks = STAGE // CHUNK
    nxt = jnp.minimum(i + 1, pl.num_programs(0) - 1)
    other = 1 - slot

    def fetch(s, chunks):
        _move_chunks(ps_ref, rsm, ybuf.at[s], ys_hbm, sem.at[s], chunks, to_sorted=False, base=s * REC)

    next_record = _record_copy(nxt, rec_hbm, rsm, sem_i.at[other], base=other * REC)
    next_record.start()

    @pl.when(i == 0)
    def _():
        _load_record(i, rec_hbm, rsm, sem_i.at[slot], base=slot * REC)
        fetch(slot, range(n_chunks))

    hb = h_ref[...]
    shared = _dot((_silu(_dot(hb, ws1_ref[...])) * _dot(hb, ws3_ref[...])).astype(BF16), ws2_ref[...])
    staged_y = ybuf.at[slot]
    _wait_all_chunks(staged_y, ys_hbm, sem.at[slot])
    next_record.wait()
    loc, g = loc_ref[...], gate_ref[...]
    routed = jnp.zeros((ROW_TILE, D_MODEL), F32)
    for rb in range(STAGE // STAGE_BLOCK):
        gm = jnp.zeros((ROW_TILE, STAGE_BLOCK), F32)
        for k, hit in enumerate(_staged_block(loc, rb)):
            gm = jnp.where(hit, g[:, k:k + 1], gm)
        g_hi = gm.astype(BF16)
        g_lo = (gm - g_hi.astype(F32)).astype(BF16)
        yb = _load_packed(staged_y, rb * STAGE_BLOCK, STAGE_BLOCK)
        routed = routed + _dot(g_hi, yb) + _dot(g_lo, yb)
        per_block = STAGE_BLOCK // CHUNK
        fetch(other, range(rb * per_block, (rb + 1) * per_block))
    gate2 = mod_ref[0][:, 5 * D_MODEL:6 * D_MODEL]
    x2 = x1_ref[...] + gate2 * (routed + shared)
    out = _rms(x2, nf_ref[...])

    @pl.when(i == pl.num_programs(0) - 1)
    def _():
        _wait_all_chunks(ybuf.at[other], ys_hbm, sem.at[other])

    @pl.when(i < rows.tiles_ctx)
    def _():
        outp_ref[...] = out

    @pl.when(i >= rows.tiles_ctx)
    def _():
        outs_ref[...] = out


def _combine(rows, pad_start, rec_flat, ys, locs, x1, h2, gates, mod3, ws1_b, ws3_b, ws2_b, normf_g):
    drop = lambda spec: pl.BlockSpec(spec.block_shape, lambda i, ps, f=spec.index_map: f(i))
    any_spec = pl.BlockSpec(memory_space=pl.ANY)
    return pl.pallas_call(
        functools.partial(_combine_kernel, rows),
        grid_spec=pltpu.PrefetchScalarGridSpec(
            num_scalar_prefetch=1,
            grid=(rows.tiles,),
            in_specs=[any_spec, any_spec]
                     + [drop(s) for s in (
                         _row_spec(LANES), _row_spec(D_MODEL), _row_spec(D_MODEL), _row_spec(LANES),
                         _mod_spec(rows),
                         _const_spec((D_MODEL, EXPERT_HIDDEN)), _const_spec((D_MODEL, EXPERT_HIDDEN)),
                         _const_spec((EXPERT_HIDDEN, D_MODEL)), _const_spec((1, D_MODEL)))],
            out_specs=[drop(_row_spec(D_MODEL, rows.ctx_tile)), drop(_row_spec(D_MODEL, rows.lat_tile))],
            scratch_shapes=[pltpu.SMEM((2 * REC,), jnp.int32),
                            pltpu.VMEM((2, STAGE * PACK, LANES), jnp.uint32),
                            pltpu.SemaphoreType.DMA((2,)), pltpu.SemaphoreType.DMA((2,))]),
        out_shape=[jax.ShapeDtypeStruct((rows.tiles_ctx * ROW_TILE, D_MODEL), F32),
                   jax.ShapeDtypeStruct((rows.tiles_lat * ROW_TILE, D_MODEL), F32)],
        compiler_params=_cparams(("arbitrary",)),
        name="moe_combine",
    )(pad_start, rec_flat, ys, locs, x1, h2, gates, mod3, ws1_b, ws3_b, ws2_b, normf_g)


def kernel(x_prompt, x_sample, cache_k, cache_v, c, c_ctx, w_mod, b_mod, norm1_g, w_in, lambda_qk, subln_g,
           w_attn_proj, conv_w, conv_b, conv_ln_g, conv_ln_b, w_conv_out, w_out, norm2_g, w_router,
           router_bias, w1, w3, w2, ws1, ws3, ws2, normf_g):
    batch, seq, _ = x_prompt.shape
    dec_batch, dec_seq, _ = x_sample.shape
    past = cache_k.shape[2]
    l = 0
    rows_p, rows_s = batch * seq, dec_batch * dec_seq
    n_rows = rows_p + rows_s
    rows = Rows(rows_p // ROW_TILE, rows_s // ROW_TILE, seq // ROW_TILE, dec_seq // ROW_TILE)
    xp = x_prompt.reshape(rows_p, D_MODEL)
    xs = x_sample.reshape(rows_s, D_MODEL)
    row = lambda a: a.reshape(1, -1)

    cc = jnp.zeros((MOD_ROWS, D_MODEL), F32).at[0].set(c_ctx).at[1:1 + dec_batch].set(c)
    mod3 = _mod_table(cc, w_mod[l], b_mod[l]).reshape(MOD_ROWS, 1, N_MOD * D_MODEL)

    q, k, v, z, gt, state_k, state_v = _inproj(rows, xp, xs, mod3, row(norm1_g[l]), w_in[l].astype(BF16),
                                               _rope_tables(dec_seq))
    lq, sg = lambda_qk[l], row(subln_g[l])
    on_p = _attention(q, k, v, lq, sg, 0, batch, seq, CTX_HEADS_PER_STEP)
    cache = (cache_k[:, l].reshape(dec_batch, past, ATTN_W), cache_v[:, l].reshape(dec_batch, past, ATTN_W))
    on_s = _attention(q, k, v, lq, sg, rows_p, dec_batch, dec_seq, LAT_HEADS_PER_STEP, cache=cache)
    x1, h2 = _merge(rows, xp, xs, on_p, on_s, z, gt, mod3, conv_w[l], row(conv_b[l]), row(conv_ln_g[l]),
                    row(conv_ln_b[l]), w_conv_out[l].astype(BF16), w_attn_proj[l].astype(BF16),
                    w_out[l].astype(BF16), row(norm2_g[l]))

    gates, locs, rec, counts = _router(rows, h2, w_router[l].astype(BF16), row(router_bias[l]))
    counts = counts[0].astype(jnp.int32)
    padded = (counts + CHUNK + MOE_BLOCK - 1) // MOE_BLOCK * MOE_BLOCK
    pad_end = jnp.cumsum(padded).astype(jnp.int32)
    pad_start = pad_end - padded
    max_rows = n_rows * TOP_K + rows.tiles * N_EXPERTS * (PAIR - 1) + N_EXPERTS * (CHUNK + MOE_BLOCK - 1)
    n_blocks = -(-max_rows // MOE_BLOCK)
    rec_flat = rec.reshape(-1)
    x_sorted = _dispatch(rows, pad_end, pad_start, counts, rec_flat, locs, h2, n_blocks * MOE_BLOCK)
    y_sorted = _experts(pad_end, pad_start, x_sorted, w1[l], w3[l], w2[l])
    y_p, y_s = _combine(rows, pad_start, rec_flat, y_sorted, locs, x1, h2, gates, mod3,
                        ws1[l].astype(BF16), ws3[l].astype(BF16), ws2[l].astype(BF16), row(normf_g))
    return (y_p.reshape(batch, seq, D_MODEL), y_s.reshape(dec_batch, dec_seq, D_MODEL),
            state_k.reshape(batch, 1, seq, N_HEADS, 2, HEAD_DIM),
            state_v.reshape(batch, 1, seq, N_HEADS, 2 * HEAD_DIM))
```

```python
import functools
import math
from typing import NamedTuple

import jax
import jax.numpy as jnp
from jax import lax
from jax.experimental import pallas as pl
from jax.experimental.pallas import tpu as pltpu

D_MODEL = 1024
GRID_W = 64
N_HEADS = 8
HEAD_DIM = 64
ATTN_W = N_HEADS * 2 * HEAD_DIM
CONV_W = 512
CONV_K = 31
N_EXPERTS = 256
TOP_K = 8
N_GROUPS = 8
TOPK_GROUPS = 4
GROUP_SIZE = N_EXPERTS // N_GROUPS
EXPERT_HIDDEN = 256
ROUTE_SCALE = 2.5
ROPE_BASE = 10000.0
EPS = 1e-6
N_MOD = 6
IN_COLS = 3 * ATTN_W + 2 * CONV_W + 2 * D_MODEL
LAM_INIT = 0.8 - 0.6 * math.exp(-0.3 * 0)
LOG2E = math.log2(math.e)

LANES = 128
SUBLANES = 8
VMEM_LIMIT = 56 * 1024 * 1024
HALO = 16
CONV_ROWS = 64
ROW_TILE = 256
CTX_HEADS_PER_STEP = 8
LAT_HEADS_PER_STEP = 2
ATTN_KEY_CHUNK = 512
MOE_BLOCK = 512
STAGE_BLOCK = 256
CLEAR_ROWS = 64
EXPERT_AHEAD = 3
EXPERT_RING = EXPERT_AHEAD + 1
MOD_ROWS = 8
MOD_COL_TILES = 4
PACK = D_MODEL // 2 // LANES
CHUNK = 4
PAIR = SUBLANES // PACK
STAGE = ROW_TILE * TOP_K + N_EXPERTS * (CHUNK - 1)
SPARE = -(-STAGE // MOE_BLOCK) * MOE_BLOCK
REC_CHUNKS = 768
REC = 2048
assert STAGE // CHUNK <= REC_CHUNKS and REC_CHUNKS % LANES == 0

BF16 = jnp.bfloat16
F32 = jnp.float32


class Rows(NamedTuple):
    tiles_ctx: int
    tiles_lat: int
    tiles_per_ctx: int
    tiles_per_lat: int

    @property
    def tiles(self):
        return self.tiles_ctx + self.tiles_lat

    def ctx_tile(self, i):
        return jnp.minimum(i, self.tiles_ctx - 1)

    def lat_tile(self, i):
        return jnp.maximum(i - self.tiles_ctx, 0)

    def mod_row(self, i):
        return jnp.where(i < self.tiles_ctx, 0, 1 + self.lat_tile(i) // self.tiles_per_lat)

    def seq_tile(self, i):
        is_ctx = i < self.tiles_ctx
        per = jnp.where(is_ctx, self.tiles_per_ctx, self.tiles_per_lat)
        return jnp.where(is_ctx, i % self.tiles_per_ctx, self.lat_tile(i) % self.tiles_per_lat), per


def _cparams(sem):
    return pltpu.CompilerParams(dimension_semantics=sem, vmem_limit_bytes=VMEM_LIMIT)


def _row_spec(width, tile=lambda i: i):
    return pl.BlockSpec((ROW_TILE, width), lambda i: (tile(i), 0))


def _const_spec(shape):
    return pl.BlockSpec(shape, lambda i: (0,) * len(shape))


def _mod_spec(rows):
    return pl.BlockSpec((1, 1, N_MOD * D_MODEL), lambda i: (rows.mod_row(i), 0, 0))


def _silu(x):
    return x * jax.nn.sigmoid(x)


def _dot(a, b):
    return jnp.dot(a, b, preferred_element_type=F32)


def _dot_nt(a, b):
    return lax.dot_general(a, b, (((1,), (1,)), ((), ())), preferred_element_type=F32)


def _rms(x, g):
    return x * lax.rsqrt(jnp.mean(x * x, axis=-1, keepdims=True) + EPS) * g


def _packed_rows(token, n_tokens, align):
    return pl.ds(pl.multiple_of(token * PACK, align * PACK), n_tokens * PACK)


def _store_packed(ref, token0, x):
    half = D_MODEL // 2
    lo = pltpu.bitcast(x[:, :half], jnp.uint32) >> 16
    hi = pltpu.bitcast(x[:, half:], jnp.uint32) & jnp.uint32(0xFFFF0000)
    w = lo | hi
    for s in range(PACK):
        ref[pl.ds(token0 * PACK + s, x.shape[0], stride=PACK), :] = w[:, s * LANES:(s + 1) * LANES]


def _load_packed(ref, token0, n_tokens):
    w = jnp.concatenate([ref[pl.ds(token0 * PACK + s, n_tokens, stride=PACK), :] for s in range(PACK)], axis=1)
    lo = pltpu.bitcast(w << 16, F32)
    hi = pltpu.bitcast(w & jnp.uint32(0xFFFF0000), F32)
    return jnp.concatenate([lo, hi], axis=1).astype(BF16)


def _mod_kernel(c_ref, w_ref, b_ref, o_ref):
    a = _silu(c_ref[...]).astype(BF16)
    o_ref[...] = _dot(a, w_ref[...].astype(BF16)) + b_ref[...]


def _mod_table(cc, w_mod, b_mod):
    n = w_mod.shape[1]
    tn = n // MOD_COL_TILES
    return pl.pallas_call(
        _mod_kernel,
        grid=(MOD_COL_TILES,),
        in_specs=[pl.BlockSpec((MOD_ROWS, D_MODEL), lambda j: (0, 0)),
                  pl.BlockSpec((D_MODEL, tn), lambda j: (0, j)),
                  pl.BlockSpec((1, tn), lambda j: (0, j))],
        out_specs=pl.BlockSpec((MOD_ROWS, tn), lambda j: (0, j)),
        out_shape=jax.ShapeDtypeStruct((MOD_ROWS, n), F32),
        compiler_params=_cparams(("arbitrary",)),
        name="mod_table",
    )(cc, w_mod, b_mod.reshape(1, n))


def _rope(x, cos, sin, lane_lo):
    swapped = jnp.where(lane_lo, pltpu.roll(x, LANES - 16, axis=1), pltpu.roll(x, 16, axis=1))
    return x * cos + swapped * sin


def _inproj_kernel(rows, xp_ref, xs_ref, mod_ref, g_ref, w_ref, cos_ref, sin_ref,
                   q_ref, k_ref, v_ref, z_ref, gt_ref, ks_ref, vs_ref):
    is_ctx = pl.program_id(0) < rows.tiles_ctx
    mod = mod_ref[0]
    shift, scale = mod[:, 0:D_MODEL], mod[:, D_MODEL:2 * D_MODEL]
    x = jnp.where(is_ctx, xp_ref[...], xs_ref[...])
    h = _rms(x, g_ref[...]) * (1.0 + scale) + shift
    hb = h.astype(BF16)
    q = _dot(hb, w_ref[:, 0:ATTN_W]) * (HEAD_DIM ** -0.5 * LOG2E)
    k = _dot(hb, w_ref[:, ATTN_W:2 * ATTN_W])
    v = _dot(hb, w_ref[:, 2 * ATTN_W:3 * ATTN_W])

    cos, sin = cos_ref[...], sin_ref[...]
    lane_lo = (lax.broadcasted_iota(jnp.int32, cos.shape, 1) % 32) < 16
    for j in range(ATTN_W // LANES):
        sl = slice(j * LANES, (j + 1) * LANES)
        q_ref[:, sl] = jnp.where(is_ctx, q[:, sl], _rope(q[:, sl], cos, sin, lane_lo)).astype(BF16)
        k_ref[:, sl] = jnp.where(is_ctx, k[:, sl], _rope(k[:, sl], cos, sin, lane_lo)).astype(BF16)

    v_ref[...] = v.astype(BF16)
    u = _dot(hb, w_ref[:, 3 * ATTN_W:3 * ATTN_W + 2 * CONV_W])
    z_ref[...] = u[:, :CONV_W] * jax.nn.sigmoid(u[:, CONV_W:])
    gt_ref[...] = jax.nn.sigmoid(_dot(hb, w_ref[:, 3 * ATTN_W + 2 * CONV_W:]))

    @pl.when(is_ctx)
    def _():
        ks_ref[...] = k
        vs_ref[...] = v


def _inproj(rows, xp, xs, mod3, norm_g, w_in_b, rope):
    n_rows = rows.tiles * ROW_TILE
    rows_p = rows.tiles_ctx * ROW_TILE
    rope_spec = pl.BlockSpec((ROW_TILE, LANES), lambda i: (rows.lat_tile(i) % rows.tiles_per_lat, 0))
    shapes = lambda w, dt, r=n_rows: jax.ShapeDtypeStruct((r, w), dt)
    return pl.pallas_call(
        functools.partial(_inproj_kernel, rows),
        grid=(rows.tiles,),
        in_specs=[_row_spec(D_MODEL, rows.ctx_tile), _row_spec(D_MODEL, rows.lat_tile), _mod_spec(rows),
                  _const_spec((1, D_MODEL)),
                  pl.BlockSpec((D_MODEL, IN_COLS), lambda i: (0, 0), pipeline_mode=pl.Buffered(1)),
                  rope_spec, rope_spec],
        out_specs=[_row_spec(ATTN_W)] * 3 + [_row_spec(CONV_W), _row_spec(2 * D_MODEL)]
                  + [_row_spec(ATTN_W, rows.ctx_tile)] * 2,
        out_shape=[shapes(ATTN_W, BF16)] * 3 + [shapes(CONV_W, F32), shapes(2 * D_MODEL, F32)]
                  + [shapes(ATTN_W, F32, rows_p)] * 2,
        compiler_params=_cparams(("arbitrary",)),
        name="inproj",
    )(xp, xs, mod3, norm_g, w_in_b, *rope)


def _rope_tables(n_tokens):
    t = jnp.arange(n_tokens, dtype=jnp.int32)
    pos = jnp.stack([t // GRID_W, t % GRID_W], axis=-1).astype(F32)
    half = HEAD_DIM // 2
    inv = ROPE_BASE ** (-jnp.arange(0, half, 2, dtype=F32) / half)
    ang = pos[:, :, None] * inv
    cos, sin = jnp.cos(ang), jnp.sin(ang)
    cos64 = jnp.concatenate([cos[:, 0], cos[:, 0], cos[:, 1], cos[:, 1]], axis=-1)
    sin64 = jnp.concatenate([-sin[:, 0], sin[:, 0], -sin[:, 1], sin[:, 1]], axis=-1)
    return jnp.tile(cos64, (1, LANES // HEAD_DIM)), jnp.tile(sin64, (1, LANES // HEAD_DIM))


def _lane_groups(x):
    return [x[:, j * LANES:(j + 1) * LANES] for j in range(x.shape[1] // LANES)]


def _attn_kernel(has_cache, heads, lq_ref, sg_ref, q_ref, k_ref, v_ref, *rest):
    if has_cache:
        ck_ref, cv_ref, o_ref, *bufs = rest
    else:
        o_ref, *bufs = rest

    def s_ref(u, mp):
        return bufs[(u % 2) * 2 + mp]

    lq = lq_ref[...]
    lam = (jnp.exp(jnp.sum(lq[0:1] * lq[1:2], axis=-1, keepdims=True))
           - jnp.exp(jnp.sum(lq[2:3] * lq[3:4], axis=-1, keepdims=True)) + LAM_INIT)
    tq, seq = q_ref.shape[0], k_ref.shape[0]
    chunks = [(off, min(ATTN_KEY_CHUNK, seq - off), False) for off in range(0, seq, ATTN_KEY_CHUNK)]
    if has_cache:
        chunks.append((seq, ck_ref.shape[0], True))
    first = lax.broadcasted_iota(jnp.int32, (tq, LANES), 1) < HEAD_DIM
    neg = jnp.full((tq, LANES), -jnp.inf, F32)
    heads_state = [dict() for _ in range(heads)]

    def lanes(u):
        return slice(u * LANES, (u + 1) * LANES)

    def load(main_ref, cache_ref, u, chunk):
        off, size, cached = chunk
        if cached:
            return cache_ref[:, lanes(u)].astype(BF16)
        return main_ref[off:off + size, lanes(u)]

    def scores(u, chunk):
        st = heads_state[u]
        if "q" not in st:
            q = q_ref[:, lanes(u)]
            zero = jnp.zeros_like(q)
            st["q"] = (jnp.where(first, q, zero), jnp.where(first, zero, q))
            st["macc"] = [neg, neg]
        kk = load(k_ref, ck_ref if has_cache else None, u, chunk)
        off, size, _ = chunk
        for mp in range(2):
            s = _dot_nt(st["q"][mp], kk)
            s_ref(u, mp)[:, off:off + size] = s
            st["macc"][mp] = functools.reduce(jnp.maximum, _lane_groups(s), st["macc"][mp])

    def exps(u, chunk):
        st = heads_state[u]
        if "m" not in st:
            st["m"] = [jnp.max(a, axis=-1, keepdims=True) for a in st["macc"]]
            st["lacc"] = [jnp.zeros((tq, LANES), F32)] * 2
        off, size, _ = chunk
        for mp in range(2):
            e = jnp.exp2(s_ref(u, mp)[:, off:off + size] - st["m"][mp])
            s_ref(u, mp)[:, off:off + size] = e
            st["lacc"][mp] = functools.reduce(jnp.add, _lane_groups(e), st["lacc"][mp])

    def values(u, chunk):
        st = heads_state[u]
        if "r" not in st:
            l1, l2 = [jnp.sum(a, axis=-1, keepdims=True) for a in st["lacc"]]
            st["r"] = (1.0 / l1, lam / l2)
            st["o"] = jnp.zeros((tq, LANES), F32)
        off, size, _ = chunk
        w = s_ref(u, 0)[:, off:off + size] * st["r"][0] - s_ref(u, 1)[:, off:off + size] * st["r"][1]
        st["o"] = st["o"] + _dot(w.astype(BF16), load(v_ref, cv_ref if has_cache else None, u, chunk))

    stages = (scores, exps, values)
    for phase in range(heads + len(stages) - 1):
        for chunk in chunks:
            for s in reversed(range(len(stages))):
                u = phase - s
                if 0 <= u < heads:
                    stages[s](u, chunk)
        u = phase - (len(stages) - 1)
        if 0 <= u < heads:
            o_ref[:, lanes(u)] = (_rms(heads_state[u]["o"], sg_ref[...]) * (1.0 - LAM_INIT)).astype(BF16)


def _attention(q, k, v, lambda_qk, subln_g, row0, batch, seq_len, heads, cache=None):
    tq = ROW_TILE
    q_tiles = seq_len // tq
    assert row0 % seq_len == 0
    seq0 = row0 // seq_len
    width = heads * LANES
    head_q = pl.BlockSpec((tq, width), lambda b, h, i: (seq0 * q_tiles + b * q_tiles + i, h))
    head_kv = pl.BlockSpec((seq_len, width), lambda b, h, i: (seq0 + b, h))
    in_specs = [pl.BlockSpec((4, HEAD_DIM), lambda b, h, i: (0, 0)),
                pl.BlockSpec((1, 2 * HEAD_DIM), lambda b, h, i: (0, 0)),
                head_q, head_kv, head_kv]
    args = [lambda_qk, subln_g, q, k, v]
    n_keys = seq_len
    if cache is not None:
        past = cache[0].shape[1]
        n_keys += past
        head_cache = pl.BlockSpec((None, past, width), lambda b, h, i: (b, 0, h))
        in_specs += [head_cache, head_cache]
        args += list(cache)
    return pl.pallas_call(
        functools.partial(_attn_kernel, cache is not None, heads),
        grid=(batch, N_HEADS // heads, q_tiles),
        in_specs=in_specs,
        out_specs=pl.BlockSpec((tq, width), lambda b, h, i: (b * q_tiles + i, h)),
        out_shape=jax.ShapeDtypeStruct((batch * seq_len, ATTN_W), BF16),
        scratch_shapes=[pltpu.VMEM((tq, n_keys), F32)] * 4,
        compiler_params=_cparams(("parallel", "parallel", "arbitrary")),
        name="attn_latent" if cache is not None else "attn_ctx",
    )(*args)


def _merge_kernel(rows, xp_ref, xs_ref, onp_ref, ons_ref, z_ref, zp_ref, zn_ref, gt_ref, mod_ref, cw_ref, cb_ref,
                  lg_ref, lb_ref, wco_ref, wap_ref, wout_ref, n2_ref, x1_ref, h2_ref, zext_ref, zsh_ref):
    i = pl.program_id(0)
    is_ctx = i < rows.tiles_ctx
    tm = ROW_TILE
    t, per = rows.seq_tile(i)
    zext_ref[HALO:HALO + tm, :] = z_ref[...]
    zext_ref[0:HALO, :] = jnp.where(t == 0, 0.0, zp_ref[...])
    zext_ref[HALO + tm:, :] = jnp.where(t == per - 1, 0.0, zn_ref[...])
    pad = HALO - CONV_K // 2
    shifted_rows = zsh_ref.shape[1]
    for s in range(SUBLANES):
        zsh_ref[s] = zext_ref[s:s + shifted_rows, :]
    col_blocks = []
    for cb in range(CONV_W // LANES):
        cs = slice(cb * LANES, (cb + 1) * LANES)
        row_chunks = []
        for r0 in range(0, tm, CONV_ROWS):
            acc = jnp.zeros((CONV_ROWS, LANES), F32) + cb_ref[:, cs]
            for j in range(CONV_K):
                phase, base = (j + pad) % SUBLANES, r0 + (j + pad) // SUBLANES * SUBLANES
                acc = acc + zsh_ref[phase, base:base + CONV_ROWS, cs] * cw_ref[j:j + 1, cs]
            row_chunks.append(acc)
        col_blocks.append(jnp.concatenate(row_chunks, axis=0))
    c = jnp.concatenate(col_blocks, axis=1)
    mu = jnp.mean(c, axis=-1, keepdims=True)
    cc = c - mu
    y = cc * lax.rsqrt(jnp.mean(cc * cc, axis=-1, keepdims=True) + EPS) * lg_ref[...] + lb_ref[...]
    conv_out = _dot(_silu(y).astype(BF16), wco_ref[...])
    o_n = jnp.where(is_ctx, onp_ref[...], ons_ref[...])
    a_br = _dot(o_n, wap_ref[...])
    gt = gt_ref[...]
    merged = gt[:, :D_MODEL] * a_br + gt[:, D_MODEL:] * conv_out
    mod = mod_ref[0]
    gate1 = mod[:, 2 * D_MODEL:3 * D_MODEL]
    shift2, scale2 = mod[:, 3 * D_MODEL:4 * D_MODEL], mod[:, 4 * D_MODEL:5 * D_MODEL]
    x = jnp.where(is_ctx, xp_ref[...], xs_ref[...])
    x1 = x + gate1 * _dot(merged.astype(BF16), wout_ref[...])
    x1_ref[...] = x1
    h2_ref[...] = (_rms(x1, n2_ref[...]) * (1.0 + scale2) + shift2).astype(BF16)


def _merge(rows, xp, xs, on_p, on_s, z, gt, mod3, conv_w, conv_b, ln_g, ln_b, wco_b, wap_b, wout_b, norm2_g):
    n_rows = rows.tiles * ROW_TILE
    hb = ROW_TILE // HALO
    n_halo_blocks = n_rows // HALO
    in_specs = [_row_spec(D_MODEL, rows.ctx_tile), _row_spec(D_MODEL, rows.lat_tile),
                _row_spec(ATTN_W, rows.ctx_tile), _row_spec(ATTN_W, rows.lat_tile),
                _row_spec(CONV_W),
                pl.BlockSpec((HALO, CONV_W), lambda i: (jnp.maximum(i * hb - 1, 0), 0)),
                pl.BlockSpec((HALO, CONV_W), lambda i: (jnp.minimum((i + 1) * hb, n_halo_blocks - 1), 0)),
                _row_spec(2 * D_MODEL), _mod_spec(rows),
                _const_spec((CONV_K, CONV_W)), _const_spec((1, CONV_W)), _const_spec((1, CONV_W)),
                _const_spec((1, CONV_W)), _const_spec((CONV_W, D_MODEL)), _const_spec((ATTN_W, D_MODEL)),
                _const_spec((D_MODEL, D_MODEL)), _const_spec((1, D_MODEL))]
    return pl.pallas_call(
        functools.partial(_merge_kernel, rows),
        grid=(rows.tiles,),
        in_specs=in_specs,
        out_specs=[_row_spec(D_MODEL), _row_spec(D_MODEL)],
        out_shape=[jax.ShapeDtypeStruct((n_rows, D_MODEL), F32), jax.ShapeDtypeStruct((n_rows, D_MODEL), BF16)],
        scratch_shapes=[pltpu.VMEM((ROW_TILE + 2 * HALO, CONV_W), F32),
                        pltpu.VMEM((SUBLANES, ROW_TILE + 2 * HALO - SUBLANES, CONV_W), F32)],
        compiler_params=_cparams(("arbitrary",)),
        name="conv_merge",
    )(xp, xs, on_p, on_s, z, z, z, gt, mod3, conv_w, conv_b, ln_g, ln_b, wco_b, wap_b, wout_b, norm2_g)


def _pack_cols(cols, shape):
    lane = lax.broadcasted_iota(jnp.int32, shape, 1)
    out = jnp.zeros(shape, F32)
    for j, col in enumerate(cols):
        out = jnp.where(lane == j, col, out)
    return out


def _router_kernel(h_ref, wr_ref, rb_ref, gate_ref, loc_ref, rec_ref, cnt_ref, run_ref):
    i = pl.program_id(0)

    @pl.when(i == 0)
    def _():
        run_ref[...] = jnp.zeros_like(run_ref)

    tm = ROW_TILE
    neg = jnp.float32(-jnp.inf)
    scores = jax.nn.sigmoid(_dot(h_ref[...], wr_ref[...]))
    biased = scores + rb_ref[...]
    lane_i = lax.broadcasted_iota(jnp.int32, scores.shape, 1)
    lane = lane_i.astype(F32)
    far = jnp.float32(2 * N_EXPERTS)

    def first_argmax(v):
        m = jnp.max(v, axis=-1, keepdims=True)
        return m, jnp.min(jnp.where(v == m, lane, far), axis=-1, keepdims=True)

    in_group, gscore = [], []
    for g in range(N_GROUPS):
        inb = (lane_i >= g * GROUP_SIZE) & (lane_i < (g + 1) * GROUP_SIZE)
        first_lane = g * GROUP_SIZE // LANES * LANES
        part = slice(first_lane, first_lane + LANES)
        lane_p = (lax.broadcasted_iota(jnp.int32, (tm, LANES), 1) + first_lane).astype(F32)
        in_part = (lane_p >= g * GROUP_SIZE) & (lane_p < (g + 1) * GROUP_SIZE)
        v = jnp.where(in_part, biased[:, part], neg)
        m1 = jnp.max(v, axis=-1, keepdims=True)
        i1 = jnp.min(jnp.where(v == m1, lane_p, far), axis=-1, keepdims=True)
        m2 = jnp.max(jnp.where(lane_p == i1, neg, v), axis=-1, keepdims=True)
        in_group.append(inb)
        gscore.append(m1 + m2)
    allowed = jnp.zeros(scores.shape, F32)
    for g in range(N_GROUPS):
        ahead = jnp.zeros((tm, 1), F32)
        for g2 in range(N_GROUPS):
            if g2 < g:
                ahead = ahead + (gscore[g2] >= gscore[g]).astype(F32)
            elif g2 > g:
                ahead = ahead + (gscore[g2] > gscore[g]).astype(F32)
        keep = (ahead < TOPK_GROUPS).astype(F32)
        allowed = jnp.where(in_group[g], keep, allowed)
    masked = jnp.where(allowed > 0.0, biased, neg)
    picked = jnp.zeros(scores.shape, F32)
    idxs, gates = [], []
    for _ in range(TOP_K):
        _, ik = first_argmax(masked)
        hit = lane == ik
        gates.append(jnp.sum(jnp.where(hit, scores, 0.0), axis=-1, keepdims=True))
        masked = jnp.where(hit, neg, masked)
        picked = jnp.where(hit, 1.0, picked)
        idxs.append(ik)
    gsum = functools.reduce(jnp.add, gates)
    gates = [g / gsum * ROUTE_SCALE for g in gates]
    r_i = lax.broadcasted_iota(jnp.int32, (tm, tm), 0)
    c_i = lax.broadcasted_iota(jnp.int32, (tm, tm), 1)
    before = (c_i < r_i).astype(BF16)
    local_rank = _dot(before, picked.astype(BF16))
    n_tok = jnp.sum(picked, axis=0, keepdims=True)
    n_chunks = jnp.floor((n_tok + (CHUNK - 1)) * (1.0 / CHUNK))
    lower = (lax.broadcasted_iota(jnp.int32, (N_EXPERTS, N_EXPERTS), 0)
             < lax.broadcasted_iota(jnp.int32, (N_EXPERTS, N_EXPERTS), 1)).astype(BF16)
    chunks_before = _dot(jnp.broadcast_to(n_chunks, (SUBLANES, N_EXPERTS)).astype(BF16), lower)[0:1]
    staged_off = chunks_before * CHUNK
    staged = local_rank + staged_off
    locs = [jnp.sum(jnp.where(lane == ik, staged, 0.0), axis=-1, keepdims=True) for ik in idxs]
    e_r = lax.broadcasted_iota(jnp.int32, (N_EXPERTS, N_EXPERTS), 0)
    e_c = lax.broadcasted_iota(jnp.int32, (N_EXPERTS, N_EXPERTS), 1)
    column = lambda v: jnp.sum(jnp.where(e_r == e_c, v, 0.0), axis=1, keepdims=True)
    first = column(chunks_before)
    last = first + column(n_chunks)
    region_off = column(run_ref[...] - staged_off)
    j = lax.broadcasted_iota(jnp.int32, (N_EXPERTS, REC_CHUNKS), 1).astype(F32)
    owns = (first <= j) & (j < last)
    expert = lax.broadcasted_iota(jnp.int32, (N_EXPERTS, REC_CHUNKS), 0).astype(F32)
    chunk_expert = jnp.sum(jnp.where(owns, expert, 0.0), axis=0, keepdims=True)
    chunk_off = jnp.sum(jnp.where(owns, region_off, 0.0), axis=0, keepdims=True) + j[0:1] * CHUNK
    total = jnp.broadcast_to(jnp.sum(n_chunks, axis=1, keepdims=True), (1, REC - 2 * REC_CHUNKS))
    rec_ref[0] = jnp.concatenate([chunk_expert, chunk_off, total], axis=1).astype(jnp.int32)
    run_ref[...] = run_ref[...] + jnp.floor((n_tok + (PAIR - 1)) * (1.0 / PAIR)) * PAIR
    cnt_ref[...] = run_ref[...]
    shape = gate_ref.shape
    gate_ref[...] = _pack_cols(gates, shape)
    loc_ref[...] = _pack_cols(locs, shape).astype(jnp.int32)


def _router(rows, h2, w_router_b, router_bias):
    n_rows = rows.tiles * ROW_TILE
    return pl.pallas_call(
        _router_kernel,
        grid=(rows.tiles,),
        in_specs=[_row_spec(D_MODEL), _const_spec((D_MODEL, N_EXPERTS)), _const_spec((1, N_EXPERTS))],
        out_specs=[_row_spec(LANES), _row_spec(LANES), pl.BlockSpec((1, 1, REC), lambda i: (i, 0, 0)),
                   _const_spec((1, N_EXPERTS))],
        out_shape=[jax.ShapeDtypeStruct((n_rows, LANES), F32), jax.ShapeDtypeStruct((n_rows, LANES), jnp.int32),
                   jax.ShapeDtypeStruct((rows.tiles, 1, REC), jnp.int32),
                   jax.ShapeDtypeStruct((1, N_EXPERTS), F32)],
        scratch_shapes=[pltpu.VMEM((1, N_EXPERTS), F32)],
        compiler_params=_cparams(("arbitrary",)),
        name="router",
    )(h2, w_router_b, router_bias)


def _record_copy(i, rec_hbm, rsm, sem_i, base=0):
    base = base if isinstance(base, int) else pl.multiple_of(base, REC)
    return pltpu.make_async_copy(rec_hbm.at[pl.ds(i * REC, REC)], rsm.at[pl.ds(base, REC)], sem_i)


def _load_record(i, rec_hbm, rsm, sem_i, base=0):
    cp = _record_copy(i, rec_hbm, rsm, sem_i, base)
    cp.start()
    cp.wait()


def _move_chunks(ps_ref, rsm, staged_ref, sorted_hbm, sem, chunks, to_sorted, base=0):
    total = rsm[base + 2 * REC_CHUNKS]
    spare0 = sorted_hbm.shape[0] // PACK - SPARE
    for j in chunks:
        placed_tok = jnp.where(j < total, ps_ref[rsm[base + j]] + rsm[base + REC_CHUNKS + j], spare0 + j * CHUNK)
        staged = staged_ref.at[pl.ds(j * CHUNK * PACK, CHUNK * PACK)]
        placed = sorted_hbm.at[_packed_rows(placed_tok, CHUNK, PAIR)]
        src, dst = (staged, placed) if to_sorted else (placed, staged)
        pltpu.make_async_copy(src, dst, sem).start()


def _wait_all_chunks(staged_ref, sorted_hbm, sem):
    pltpu.make_async_copy(staged_ref, sorted_hbm.at[pl.ds(0, STAGE * PACK)], sem).wait()


def _staged_block(loc, rb):
    col = lax.broadcasted_iota(jnp.int32, (loc.shape[0], STAGE_BLOCK), 1) + rb * STAGE_BLOCK
    return [loc[:, k:k + 1] == col for k in range(TOP_K)]


def _dispatch_kernel(pe_ref, ps_ref, c2_ref, rec_hbm, loc_ref, h_ref, xs_hbm, rsm, stage, zbuf, sem_i, sem):
    i = pl.program_id(0)
    block_rows = MOE_BLOCK * PACK

    @pl.when(i == 0)
    def _():
        zbuf[...] = jnp.zeros_like(zbuf)
        n_blocks = xs_hbm.shape[0] // block_rows
        n_used = pe_ref[N_EXPERTS - 1] // MOE_BLOCK
        unit_rows = CLEAR_ROWS * PACK
        zunit = zbuf.at[pl.ds(0, unit_rows)]

        def clear_unit(u, carry):
            dst = xs_hbm.at[pl.ds(pl.multiple_of(u * unit_rows, unit_rows), unit_rows)]
            pltpu.make_async_copy(zunit, dst, sem.at[0]).start()
            return carry

        def clear_block(b, carry):
            dst = xs_hbm.at[pl.ds(pl.multiple_of(b * block_rows, block_rows), block_rows)]
            pltpu.make_async_copy(zbuf, dst, sem_i).start()
            return carry

        def clear_padding(e, total):
            first = (ps_ref[e] + c2_ref[e]) // CLEAR_ROWS
            last = pe_ref[e] // CLEAR_ROWS
            lax.fori_loop(first, last, clear_unit, 0)
            return total + last - first

        def unit_wait(e, carry):
            pltpu.make_async_copy(zunit, xs_hbm.at[pl.ds(0, unit_rows)], sem.at[0]).wait()
            return carry

        def block_wait(e, carry):
            pltpu.make_async_copy(zbuf, xs_hbm.at[pl.ds(0, block_rows)], sem_i).wait()
            return carry

        n_units = lax.fori_loop(0, N_EXPERTS, clear_padding, 0)
        lax.fori_loop(n_used, n_blocks, clear_block, 0)
        lax.fori_loop(0, n_units, unit_wait, 0)
        lax.fori_loop(0, n_blocks - n_used, block_wait, 0)

    _load_record(i, rec_hbm, rsm, sem_i)
    slot = i % 2
    loc, hb = loc_ref[...], h_ref[...]

    def stage_block(rb):
        onehot = jnp.zeros((ROW_TILE, STAGE_BLOCK), F32)
        for hit in _staged_block(loc, rb):
            onehot = jnp.where(hit, 1.0, onehot)
        rows = lax.dot_general(onehot.astype(BF16), hb, (((0,), (0,)), ((), ())), preferred_element_type=F32)
        _store_packed(stage.at[slot], rb * STAGE_BLOCK, rows)

    n_blocks, n_chunks = STAGE // STAGE_BLOCK, STAGE // CHUNK
    n_early = n_blocks // 2 + 1
    for rb in range(n_early):
        stage_block(rb)

    @pl.when(i >= 1)
    def _():
        _wait_all_chunks(stage.at[1 - slot], xs_hbm, sem.at[1 - slot])

    per_round = -(-n_chunks // (n_blocks - n_early))
    for r, rb in enumerate(range(n_early, n_blocks)):
        stage_block(rb)
        chunks = range(r * per_round, min((r + 1) * per_round, n_chunks))
        assert (chunks[-1] * CHUNK) // STAGE_BLOCK <= rb
        _move_chunks(ps_ref, rsm, stage.at[slot], xs_hbm, sem.at[slot], chunks, to_sorted=True)

    @pl.when(i == pl.num_programs(0) - 1)
    def _():
        _wait_all_chunks(stage.at[slot], xs_hbm, sem.at[slot])


def _dispatch(rows, pad_end, pad_start, counts, rec_flat, locs, h2, n_rows_sorted):
    return pl.pallas_call(
        _dispatch_kernel,
        grid_spec=pltpu.PrefetchScalarGridSpec(
            num_scalar_prefetch=3,
            grid=(rows.tiles,),
            in_specs=[pl.BlockSpec(memory_space=pl.ANY),
                      pl.BlockSpec((ROW_TILE, LANES), lambda i, pe, ps, c2: (i, 0)),
                      pl.BlockSpec((ROW_TILE, D_MODEL), lambda i, pe, ps, c2: (i, 0))],
            out_specs=pl.BlockSpec(memory_space=pl.ANY),
            scratch_shapes=[pltpu.SMEM((REC,), jnp.int32),
                            pltpu.VMEM((2, STAGE * PACK, LANES), jnp.uint32),
                            pltpu.VMEM((MOE_BLOCK * PACK, LANES), jnp.uint32),
                            pltpu.SemaphoreType.DMA, pltpu.SemaphoreType.DMA((2,))]),
        out_shape=jax.ShapeDtypeStruct(((n_rows_sorted + SPARE) * PACK, LANES), jnp.uint32),
        compiler_params=_cparams(("arbitrary",)),
        name="moe_dispatch",
    )(pad_end, pad_start, counts, rec_flat, locs, h2)


def _expert_kernel(pe_ref, ps_ref, xs_hbm, w1_ref, w3_ref, w2_ref, ys_hbm, xbuf, ybuf, w1b, w3b, w2b,
                   sem_in, sem_out):
    e = pl.program_id(0)
    first, last = ps_ref[e] // MOE_BLOCK, pe_ref[e] // MOE_BLOCK
    n_used = pe_ref[N_EXPERTS - 1] // MOE_BLOCK
    block_rows = MOE_BLOCK * PACK

    def rows_of(g):
        return pl.ds(pl.multiple_of(g * block_rows, block_rows), block_rows)

    def copy_in(g):
        slot = g % EXPERT_RING
        return pltpu.make_async_copy(xs_hbm.at[rows_of(g)], xbuf.at[slot], sem_in.at[slot])

    def copy_out(g):
        slot = g % EXPERT_RING
        return pltpu.make_async_copy(ybuf.at[slot], ys_hbm.at[rows_of(g)], sem_out.at[slot])

    @pl.when(e == 0)
    def _():
        for g in range(EXPERT_AHEAD):
            @pl.when(g < n_used)
            def _():
                copy_in(g).start()

    w1b[...] = w1_ref[...].astype(BF16)
    w3b[...] = w3_ref[...].astype(BF16)
    w2b[...] = w2_ref[...].astype(BF16)

    def block(g, carry):
        copy_in(g).wait()

        @pl.when(g + EXPERT_AHEAD < n_used)
        def _():
            copy_in(g + EXPERT_AHEAD).start()

        @pl.when(g >= EXPERT_RING)
        def _():
            copy_out(g - EXPERT_RING).wait()

        slot = g % EXPERT_RING
        parts = range(0, MOE_BLOCK, STAGE_BLOCK)
        xb = jnp.concatenate([_load_packed(xbuf.at[slot], t0, STAGE_BLOCK) for t0 in parts], axis=0)
        hid = (_silu(_dot(xb, w1b[...])) * _dot(xb, w3b[...])).astype(BF16)
        y = _dot(hid, w2b[...]).astype(BF16).astype(F32)
        for t0 in parts:
            _store_packed(ybuf.at[slot], t0, y[t0:t0 + STAGE_BLOCK])
        copy_out(g).start()
        return carry

    lax.fori_loop(first, last, block, 0)

    @pl.when(e == N_EXPERTS - 1)
    def _():
        for back in range(EXPERT_RING, 0, -1):
            @pl.when(n_used >= back)
            def _():
                copy_out(n_used - back).wait()


def _experts(pad_end, pad_start, xs, w1, w3, w2):
    expert_spec = lambda shape: pl.BlockSpec((None,) + shape, lambda e, pe, ps: (e, 0, 0))
    any_spec = pl.BlockSpec(memory_space=pl.ANY)
    return pl.pallas_call(
        _expert_kernel,
        grid_spec=pltpu.PrefetchScalarGridSpec(
            num_scalar_prefetch=2,
            grid=(N_EXPERTS,),
            in_specs=[any_spec, expert_spec((D_MODEL, EXPERT_HIDDEN)), expert_spec((D_MODEL, EXPERT_HIDDEN)),
                      expert_spec((EXPERT_HIDDEN, D_MODEL))],
            out_specs=any_spec,
            scratch_shapes=[pltpu.VMEM((EXPERT_RING, MOE_BLOCK * PACK, LANES), jnp.uint32),
                            pltpu.VMEM((EXPERT_RING, MOE_BLOCK * PACK, LANES), jnp.uint32),
                            pltpu.VMEM((D_MODEL, EXPERT_HIDDEN), BF16),
                            pltpu.VMEM((D_MODEL, EXPERT_HIDDEN), BF16),
                            pltpu.VMEM((EXPERT_HIDDEN, D_MODEL), BF16),
                            pltpu.SemaphoreType.DMA((EXPERT_RING,)), pltpu.SemaphoreType.DMA((EXPERT_RING,))]),
        out_shape=jax.ShapeDtypeStruct(xs.shape, jnp.uint32),
        input_output_aliases={2: 0},
        compiler_params=_cparams(("arbitrary",)),
        name="moe_experts",
    )(pad_end, pad_start, xs, w1, w3, w2)


def _combine_kernel(rows, ps_ref, rec_hbm, ys_hbm, loc_ref, x1_ref, h_ref, gate_ref, mod_ref, ws1_ref, ws3_ref,
                    ws2_ref, nf_ref, outp_ref, outs_ref, rsm, ybuf, sem_i, sem):
    i = pl.program_id(0)
    slot = i % 2

    n_chunks = STAGE // CHUNK
    nxt = jnp.minimum(i + 1, pl.num_programs(0) - 1)
    other = 1 - slot

    def fetch(s, chunks):
        _move_chunks(ps_ref, rsm, ybuf.at[s], ys_hbm, sem.at[s], chunks, to_sorted=False, base=s * REC)

    next_record = _record_copy(nxt, rec_hbm, rsm, sem_i.at[other], base=other * REC)
    next_record.start()

    @pl.when(i == 0)
    def _():
        _load_record(i, rec_hbm, rsm, sem_i.at[slot], base=slot * REC)
        fetch(slot, range(n_chunks))

    hb = h_ref[...]
    shared = _dot((_silu(_dot(hb, ws1_ref[...])) * _dot(hb, ws3_ref[...])).astype(BF16), ws2_ref[...])
    staged_y = ybuf.at[slot]
    _wait_all_chunks(staged_y, ys_hbm, sem.at[slot])
    next_record.wait()
    loc, g = loc_ref[...], gate_ref[...]
    routed = jnp.zeros((ROW_TILE, D_MODEL), F32)
    for rb in range(STAGE // STAGE_BLOCK):
        gm = jnp.zeros((ROW_TILE, STAGE_BLOCK), F32)
        for k, hit in enumerate(_staged_block(loc, rb)):
            gm = jnp.where(hit, g[:, k:k + 1], gm)
        g_hi = gm.astype(BF16)
        g_lo = (gm - g_hi.astype(F32)).astype(BF16)
        yb = _load_packed(staged_y, rb * STAGE_BLOCK, STAGE_BLOCK)
        routed = routed + _dot(g_hi, yb) + _dot(g_lo, yb)
        per_block = STAGE_BLOCK // CHUNK
        fetch(other, range(rb * per_block, (rb + 1) * per_block))
    gate2 = mod_ref[0][:, 5 * D_MODEL:6 * D_MODEL]
    x2 = x1_ref[...] + gate2 * (routed + shared)
    out = _rms(x2, nf_ref[...])

    @pl.when(i == pl.num_programs(0) - 1)
    def _():
        _wait_all_chunks(ybuf.at[other], ys_hbm, sem.at[other])

    @pl.when(i < rows.tiles_ctx)
    def _():
        outp_ref[...] = out

    @pl.when(i >= rows.tiles_ctx)
    def _():
        outs_ref[...] = out


def _combine(rows, pad_start, rec_flat, ys, locs, x1, h2, gates, mod3, ws1_b, ws3_b, ws2_b, normf_g):
    drop = lambda spec: pl.BlockSpec(spec.block_shape, lambda i, ps, f=spec.index_map: f(i))
    any_spec = pl.BlockSpec(memory_space=pl.ANY)
    return pl.pallas_call(
        functools.partial(_combine_kernel, rows),
        grid_spec=pltpu.PrefetchScalarGridSpec(
            num_scalar_prefetch=1,
            grid=(rows.tiles,),
            in_specs=[any_spec, any_spec]
                     + [drop(s) for s in (
                         _row_spec(LANES), _row_spec(D_MODEL), _row_spec(D_MODEL), _row_spec(LANES),
                         _mod_spec(rows),
                         _const_spec((D_MODEL, EXPERT_HIDDEN)), _const_spec((D_MODEL, EXPERT_HIDDEN)),
                         _const_spec((EXPERT_HIDDEN, D_MODEL)), _const_spec((1, D_MODEL)))],
            out_specs=[drop(_row_spec(D_MODEL, rows.ctx_tile)), drop(_row_spec(D_MODEL, rows.lat_tile))],
            scratch_shapes=[pltpu.SMEM((2 * REC,), jnp.int32),
                            pltpu.VMEM((2, STAGE * PACK, LANES), jnp.uint32),
                            pltpu.SemaphoreType.DMA((2,)), pltpu.SemaphoreType.DMA((2,))]),
        out_shape=[jax.ShapeDtypeStruct((rows.tiles_ctx * ROW_TILE, D_MODEL), F32),
                   jax.ShapeDtypeStruct((rows.tiles_lat * ROW_TILE, D_MODEL), F32)],
        compiler_params=_cparams(("arbitrary",)),
        name="moe_combine",
    )(pad_start, rec_flat, ys, locs, x1, h2, gates, mod3, ws1_b, ws3_b, ws2_b, normf_g)


def kernel(x_prompt, x_sample, cache_k, cache_v, c, c_ctx, w_mod, b_mod, norm1_g, w_in, lambda_qk, subln_g,
           w_attn_proj, conv_w, conv_b, conv_ln_g, conv_ln_b, w_conv_out, w_out, norm2_g, w_router,
           router_bias, w1, w3, w2, ws1, ws3, ws2, normf_g):
    batch, seq, _ = x_prompt.shape
    dec_batch, dec_seq, _ = x_sample.shape
    past = cache_k.shape[2]
    l = 0
    rows_p, rows_s = batch * seq, dec_batch * dec_seq
    n_rows = rows_p + rows_s
    rows = Rows(rows_p // ROW_TILE, rows_s // ROW_TILE, seq // ROW_TILE, dec_seq // ROW_TILE)
    xp = x_prompt.reshape(rows_p, D_MODEL)
    xs = x_sample.reshape(rows_s, D_MODEL)
    row = lambda a: a.reshape(1, -1)

    cc = jnp.zeros((MOD_ROWS, D_MODEL), F32).at[0].set(c_ctx).at[1:1 + dec_batch].set(c)
    mod3 = _mod_table(cc, w_mod[l], b_mod[l]).reshape(MOD_ROWS, 1, N_MOD * D_MODEL)

    q, k, v, z, gt, state_k, state_v = _inproj(rows, xp, xs, mod3, row(norm1_g[l]), w_in[l].astype(BF16),
                                               _rope_tables(dec_seq))
    lq, sg = lambda_qk[l], row(subln_g[l])
    on_p = _attention(q, k, v, lq, sg, 0, batch, seq, CTX_HEADS_PER_STEP)
    cache = (cache_k[:, l].reshape(dec_batch, past, ATTN_W), cache_v[:, l].reshape(dec_batch, past, ATTN_W))
    on_s = _attention(q, k, v, lq, sg, rows_p, dec_batch, dec_seq, LAT_HEADS_PER_STEP, cache=cache)
    x1, h2 = _merge(rows, xp, xs, on_p, on_s, z, gt, mod3, conv_w[l], row(conv_b[l]), row(conv_ln_g[l]),
                    row(conv_ln_b[l]), w_conv_out[l].astype(BF16), w_attn_proj[l].astype(BF16),
                    w_out[l].astype(BF16), row(norm2_g[l]))

    gates, locs, rec, counts = _router(rows, h2, w_router[l].astype(BF16), row(router_bias[l]))
    counts = counts[0].astype(jnp.int32)
    padded = (counts + CHUNK + MOE_BLOCK - 1) // MOE_BLOCK * MOE_BLOCK
    pad_end = jnp.cumsum(padded).astype(jnp.int32)
    pad_start = pad_end - padded
    max_rows = n_rows * TOP_K + rows.tiles * N_EXPERTS * (PAIR - 1) + N_EXPERTS * (CHUNK + MOE_BLOCK - 1)
    n_blocks = -(-max_rows // MOE_BLOCK)
    rec_flat = rec.reshape(-1)
    x_sorted = _dispatch(rows, pad_end, pad_start, counts, rec_flat, locs, h2, n_blocks * MOE_BLOCK)
    y_sorted = _experts(pad_end, pad_start, x_sorted, w1[l], w3[l], w2[l])
    y_p, y_s = _combine(rows, pad_start, rec_flat, y_sorted, locs, x1, h2, gates, mod3,
                        ws1[l].astype(BF16), ws3[l].astype(BF16), ws2[l].astype(BF16), row(normf_g))
    return (y_p.reshape(batch, seq, D_MODEL), y_s.reshape(dec_batch, dec_seq, D_MODEL),
            state_k.reshape(batch, 1, seq, N_HEADS, 2, HEAD_DIM),
            state_v.reshape(batch, 1, seq, N_HEADS, 2 * HEAD_DIM))
```

```python
import functools
import math
from typing import NamedTuple

import jax
import jax.numpy as jnp
from jax import lax
from jax.experimental import pallas as pl
from jax.experimental.pallas import tpu as pltpu

D_MODEL = 1024
GRID_W = 64
N_HEADS = 8
HEAD_DIM = 64
ATTN_W = N_HEADS * 2 * HEAD_DIM
CONV_W = 512
CONV_K = 31
N_EXPERTS = 256
TOP_K = 8
N_GROUPS = 8
TOPK_GROUPS = 4
GROUP_SIZE = N_EXPERTS // N_GROUPS
EXPERT_HIDDEN = 256
ROUTE_SCALE = 2.5
ROPE_BASE = 10000.0
EPS = 1e-6
N_MOD = 6
IN_COLS = 3 * ATTN_W + 2 * CONV_W + 2 * D_MODEL
LAM_INIT = 0.8 - 0.6 * math.exp(-0.3 * 0)
LOG2E = math.log2(math.e)

LANES = 128
SUBLANES = 8
VMEM_LIMIT = 56 * 1024 * 1024
HALO = 16
CONV_ROWS = 64
ROW_TILE = 256
CTX_HEADS_PER_STEP = 8
LAT_HEADS_PER_STEP = 2
ATTN_KEY_CHUNK = 512
MOE_BLOCK = 512
STAGE_BLOCK = 256
CLEAR_ROWS = 64
EXPERT_AHEAD = 3
EXPERT_RING = EXPERT_AHEAD + 1
MOD_ROWS = 8
MOD_COL_TILES = 4
PACK = D_MODEL // 2 // LANES
CHUNK = 4
PAIR = SUBLANES // PACK
STAGE = ROW_TILE * TOP_K + N_EXPERTS * (CHUNK - 1)
SPARE = -(-STAGE // MOE_BLOCK) * MOE_BLOCK
MAIN_CHUNKS = 576
TAIL_GROUP = 64
assert (STAGE // CHUNK - MAIN_CHUNKS) % TAIL_GROUP == 0 and MAIN_CHUNKS * CHUNK >= ROW_TILE * TOP_K
REC_CHUNKS = 768
REC = 2048
assert STAGE // CHUNK <= REC_CHUNKS and REC_CHUNKS % LANES == 0

BF16 = jnp.bfloat16
F32 = jnp.float32


class Rows(NamedTuple):
    tiles_ctx: int
    tiles_lat: int
    tiles_per_ctx: int
    tiles_per_lat: int

    @property
    def tiles(self):
        return self.tiles_ctx + self.tiles_lat

    def ctx_tile(self, i):
        return jnp.minimum(i, self.tiles_ctx - 1)

    def lat_tile(self, i):
        return jnp.maximum(i - self.tiles_ctx, 0)

    def mod_row(self, i):
        return jnp.where(i < self.tiles_ctx, 0, 1 + self.lat_tile(i) // self.tiles_per_lat)

    def seq_tile(self, i):
        is_ctx = i < self.tiles_ctx
        per = jnp.where(is_ctx, self.tiles_per_ctx, self.tiles_per_lat)
        return jnp.where(is_ctx, i % self.tiles_per_ctx, self.lat_tile(i) % self.tiles_per_lat), per


def _cparams(sem):
    return pltpu.CompilerParams(dimension_semantics=sem, vmem_limit_bytes=VMEM_LIMIT)


def _row_spec(width, tile=lambda i: i):
    return pl.BlockSpec((ROW_TILE, width), lambda i: (tile(i), 0))


def _const_spec(shape):
    return pl.BlockSpec(shape, lambda i: (0,) * len(shape))


def _mod_spec(rows):
    return pl.BlockSpec((1, 1, N_MOD * D_MODEL), lambda i: (rows.mod_row(i), 0, 0))


def _silu(x):
    return x * jax.nn.sigmoid(x)


def _dot(a, b):
    return jnp.dot(a, b, preferred_element_type=F32)


def _dot_nt(a, b):
    return lax.dot_general(a, b, (((1,), (1,)), ((), ())), preferred_element_type=F32)


def _rms(x, g):
    return x * lax.rsqrt(jnp.mean(x * x, axis=-1, keepdims=True) + EPS) * g


def _packed_rows(token, n_tokens, align):
    return pl.ds(pl.multiple_of(token * PACK, align * PACK), n_tokens * PACK)


def _store_packed(ref, token0, x):
    half = D_MODEL // 2
    lo = pltpu.bitcast(x[:, :half], jnp.uint32) >> 16
    hi = pltpu.bitcast(x[:, half:], jnp.uint32) & jnp.uint32(0xFFFF0000)
    w = lo | hi
    for s in range(PACK):
        ref[pl.ds(token0 * PACK + s, x.shape[0], stride=PACK), :] = w[:, s * LANES:(s + 1) * LANES]


def _load_packed(ref, token0, n_tokens):
    w = jnp.concatenate([ref[pl.ds(token0 * PACK + s, n_tokens, stride=PACK), :] for s in range(PACK)], axis=1)
    lo = pltpu.bitcast(w << 16, F32)
    hi = pltpu.bitcast(w & jnp.uint32(0xFFFF0000), F32)
    return jnp.concatenate([lo, hi], axis=1).astype(BF16)


def _mod_kernel(c_ref, w_ref, b_ref, o_ref):
    a = _silu(c_ref[...]).astype(BF16)
    o_ref[...] = _dot(a, w_ref[...].astype(BF16)) + b_ref[...]


def _mod_table(cc, w_mod, b_mod):
    n = w_mod.shape[1]
    tn = n // MOD_COL_TILES
    return pl.pallas_call(
        _mod_kernel,
        grid=(MOD_COL_TILES,),
        in_specs=[pl.BlockSpec((MOD_ROWS, D_MODEL), lambda j: (0, 0)),
                  pl.BlockSpec((D_MODEL, tn), lambda j: (0, j)),
                  pl.BlockSpec((1, tn), lambda j: (0, j))],
        out_specs=pl.BlockSpec((MOD_ROWS, tn), lambda j: (0, j)),
        out_shape=jax.ShapeDtypeStruct((MOD_ROWS, n), F32),
        compiler_params=_cparams(("arbitrary",)),
        name="mod_table",
    )(cc, w_mod, b_mod.reshape(1, n))


def _rope(x, cos, sin, lane_lo):
    swapped = jnp.where(lane_lo, pltpu.roll(x, LANES - 16, axis=1), pltpu.roll(x, 16, axis=1))
    return x * cos + swapped * sin


def _inproj_kernel(rows, xp_ref, xs_ref, mod_ref, g_ref, w_ref, cos_ref, sin_ref,
                   q_ref, k_ref, v_ref, z_ref, gt_ref, ks_ref, vs_ref):
    is_ctx = pl.program_id(0) < rows.tiles_ctx
    mod = mod_ref[0]
    shift, scale = mod[:, 0:D_MODEL], mod[:, D_MODEL:2 * D_MODEL]
    x = jnp.where(is_ctx, xp_ref[...], xs_ref[...])
    h = _rms(x, g_ref[...]) * (1.0 + scale) + shift
    hb = h.astype(BF16)
    q = _dot(hb, w_ref[:, 0:ATTN_W]) * (HEAD_DIM ** -0.5 * LOG2E)
    k = _dot(hb, w_ref[:, ATTN_W:2 * ATTN_W])
    v = _dot(hb, w_ref[:, 2 * ATTN_W:3 * ATTN_W])

    cos, sin = cos_ref[...], sin_ref[...]
    lane_lo = (lax.broadcasted_iota(jnp.int32, cos.shape, 1) % 32) < 16
    for j in range(ATTN_W // LANES):
        sl = slice(j * LANES, (j + 1) * LANES)
        q_ref[:, sl] = jnp.where(is_ctx, q[:, sl], _rope(q[:, sl], cos, sin, lane_lo)).astype(BF16)
        k_ref[:, sl] = jnp.where(is_ctx, k[:, sl], _rope(k[:, sl], cos, sin, lane_lo)).astype(BF16)

    v_ref[...] = v.astype(BF16)
    u = _dot(hb, w_ref[:, 3 * ATTN_W:3 * ATTN_W + 2 * CONV_W])
    z_ref[...] = u[:, :CONV_W] * jax.nn.sigmoid(u[:, CONV_W:])
    gt_ref[...] = jax.nn.sigmoid(_dot(hb, w_ref[:, 3 * ATTN_W + 2 * CONV_W:]))

    @pl.when(is_ctx)
    def _():
        ks_ref[...] = k
        vs_ref[...] = v


def _inproj(rows, xp, xs, mod3, norm_g, w_in_b, rope):
    n_rows = rows.tiles * ROW_TILE
    rows_p = rows.tiles_ctx * ROW_TILE
    rope_spec = pl.BlockSpec((ROW_TILE, LANES), lambda i: (rows.lat_tile(i) % rows.tiles_per_lat, 0))
    shapes = lambda w, dt, r=n_rows: jax.ShapeDtypeStruct((r, w), dt)
    return pl.pallas_call(
        functools.partial(_inproj_kernel, rows),
        grid=(rows.tiles,),
        in_specs=[_row_spec(D_MODEL, rows.ctx_tile), _row_spec(D_MODEL, rows.lat_tile), _mod_spec(rows),
                  _const_spec((1, D_MODEL)),
                  pl.BlockSpec((D_MODEL, IN_COLS), lambda i: (0, 0), pipeline_mode=pl.Buffered(1)),
                  rope_spec, rope_spec],
        out_specs=[_row_spec(ATTN_W)] * 3 + [_row_spec(CONV_W), _row_spec(2 * D_MODEL)]
                  + [_row_spec(ATTN_W, rows.ctx_tile)] * 2,
        out_shape=[shapes(ATTN_W, BF16)] * 3 + [shapes(CONV_W, F32), shapes(2 * D_MODEL, F32)]
                  + [shapes(ATTN_W, F32, rows_p)] * 2,
        compiler_params=_cparams(("arbitrary",)),
        name="inproj",
    )(xp, xs, mod3, norm_g, w_in_b, *rope)


def _rope_tables(n_tokens):
    t = jnp.arange(n_tokens, dtype=jnp.int32)
    pos = jnp.stack([t // GRID_W, t % GRID_W], axis=-1).astype(F32)
    half = HEAD_DIM // 2
    inv = ROPE_BASE ** (-jnp.arange(0, half, 2, dtype=F32) / half)
    ang = pos[:, :, None] * inv
    cos, sin = jnp.cos(ang), jnp.sin(ang)
    cos64 = jnp.concatenate([cos[:, 0], cos[:, 0], cos[:, 1], cos[:, 1]], axis=-1)
    sin64 = jnp.concatenate([-sin[:, 0], sin[:, 0], -sin[:, 1], sin[:, 1]], axis=-1)
    return jnp.tile(cos64, (1, LANES // HEAD_DIM)), jnp.tile(sin64, (1, LANES // HEAD_DIM))


def _lane_groups(x):
    return [x[:, j * LANES:(j + 1) * LANES] for j in range(x.shape[1] // LANES)]


def _attn_kernel(has_cache, heads, lq_ref, sg_ref, q_ref, k_ref, v_ref, *rest):
    if has_cache:
        ck_ref, cv_ref, o_ref, *bufs = rest
    else:
        o_ref, *bufs = rest

    def s_ref(u, mp):
        return bufs[(u % 2) * 2 + mp]

    lq = lq_ref[...]
    lam = (jnp.exp(jnp.sum(lq[0:1] * lq[1:2], axis=-1, keepdims=True))
           - jnp.exp(jnp.sum(lq[2:3] * lq[3:4], axis=-1, keepdims=True)) + LAM_INIT)
    tq, seq = q_ref.shape[0], k_ref.shape[0]
    chunks = [(off, min(ATTN_KEY_CHUNK, seq - off), False) for off in range(0, seq, ATTN_KEY_CHUNK)]
    if has_cache:
        chunks.append((seq, ck_ref.shape[0], True))
    first = lax.broadcasted_iota(jnp.int32, (tq, LANES), 1) < HEAD_DIM
    neg = jnp.full((tq, LANES), -jnp.inf, F32)
    heads_state = [dict() for _ in range(heads)]

    def lanes(u):
        return slice(u * LANES, (u + 1) * LANES)

    def load(main_ref, cache_ref, u, chunk):
        off, size, cached = chunk
        if cached:
            return cache_ref[:, lanes(u)].astype(BF16)
        return main_ref[off:off + size, lanes(u)]

    def scores(u, chunk):
        st = heads_state[u]
        if "q" not in st:
            q = q_ref[:, lanes(u)]
            zero = jnp.zeros_like(q)
            st["q"] = (jnp.where(first, q, zero), jnp.where(first, zero, q))
            st["macc"] = [neg, neg]
        kk = load(k_ref, ck_ref if has_cache else None, u, chunk)
        off, size, _ = chunk
        for mp in range(2):
            s = _dot_nt(st["q"][mp], kk)
            s_ref(u, mp)[:, off:off + size] = s
            st["macc"][mp] = functools.reduce(jnp.maximum, _lane_groups(s), st["macc"][mp])

    def exps(u, chunk):
        st = heads_state[u]
        if "m" not in st:
            st["m"] = [jnp.max(a, axis=-1, keepdims=True) for a in st["macc"]]
            st["lacc"] = [jnp.zeros((tq, LANES), F32)] * 2
        off, size, _ = chunk
        for mp in range(2):
            e = jnp.exp2(s_ref(u, mp)[:, off:off + size] - st["m"][mp])
            s_ref(u, mp)[:, off:off + size] = e
            st["lacc"][mp] = functools.reduce(jnp.add, _lane_groups(e), st["lacc"][mp])

    def values(u, chunk):
        st = heads_state[u]
        if "r" not in st:
            l1, l2 = [jnp.sum(a, axis=-1, keepdims=True) for a in st["lacc"]]
            st["r"] = (1.0 / l1, lam / l2)
            st["o"] = jnp.zeros((tq, LANES), F32)
        off, size, _ = chunk
        w = s_ref(u, 0)[:, off:off + size] * st["r"][0] - s_ref(u, 1)[:, off:off + size] * st["r"][1]
        st["o"] = st["o"] + _dot(w.astype(BF16), load(v_ref, cv_ref if has_cache else None, u, chunk))

    stages = (scores, exps, values)
    for phase in range(heads + len(stages) - 1):
        for chunk in chunks:
            for s in reversed(range(len(stages))):
                u = phase - s
                if 0 <= u < heads:
                    stages[s](u, chunk)
        u = phase - (len(stages) - 1)
        if 0 <= u < heads:
            o_ref[:, lanes(u)] = (_rms(heads_state[u]["o"], sg_ref[...]) * (1.0 - LAM_INIT)).astype(BF16)


def _attention(q, k, v, lambda_qk, subln_g, row0, batch, seq_len, heads, cache=None):
    tq = ROW_TILE
    q_tiles = seq_len // tq
    assert row0 % seq_len == 0
    seq0 = row0 // seq_len
    width = heads * LANES
    head_q = pl.BlockSpec((tq, width), lambda b, h, i: (seq0 * q_tiles + b * q_tiles + i, h))
    head_kv = pl.BlockSpec((seq_len, width), lambda b, h, i: (seq0 + b, h))
    in_specs = [pl.BlockSpec((4, HEAD_DIM), lambda b, h, i: (0, 0)),
                pl.BlockSpec((1, 2 * HEAD_DIM), lambda b, h, i: (0, 0)),
                head_q, head_kv, head_kv]
    args = [lambda_qk, subln_g, q, k, v]
    n_keys = seq_len
    if cache is not None:
        past = cache[0].shape[1]
        n_keys += past
        head_cache = pl.BlockSpec((None, past, width), lambda b, h, i: (b, 0, h))
        in_specs += [head_cache, head_cache]
        args += list(cache)
    return pl.pallas_call(
        functools.partial(_attn_kernel, cache is not None, heads),
        grid=(batch, N_HEADS // heads, q_tiles),
        in_specs=in_specs,
        out_specs=pl.BlockSpec((tq, width), lambda b, h, i: (b * q_tiles + i, h)),
        out_shape=jax.ShapeDtypeStruct((batch * seq_len, ATTN_W), BF16),
        scratch_shapes=[pltpu.VMEM((tq, n_keys), F32)] * 4,
        compiler_params=_cparams(("parallel", "parallel", "arbitrary")),
        name="attn_latent" if cache is not None else "attn_ctx",
    )(*args)


def _merge_kernel(rows, xp_ref, xs_ref, onp_ref, ons_ref, z_ref, zp_ref, zn_ref, gt_ref, mod_ref, cw_ref, cb_ref,
                  lg_ref, lb_ref, wco_ref, wap_ref, wout_ref, n2_ref, x1_ref, h2_ref, zext_ref, zsh_ref):
    i = pl.program_id(0)
    is_ctx = i < rows.tiles_ctx
    tm = ROW_TILE
    t, per = rows.seq_tile(i)
    zext_ref[HALO:HALO + tm, :] = z_ref[...]
    zext_ref[0:HALO, :] = jnp.where(t == 0, 0.0, zp_ref[...])
    zext_ref[HALO + tm:, :] = jnp.where(t == per - 1, 0.0, zn_ref[...])
    pad = HALO - CONV_K // 2
    shifted_rows = zsh_ref.shape[1]
    for s in range(SUBLANES):
        zsh_ref[s] = zext_ref[s:s + shifted_rows, :]
    col_blocks = []
    for cb in range(CONV_W // LANES):
        cs = slice(cb * LANES, (cb + 1) * LANES)
        row_chunks = []
        for r0 in range(0, tm, CONV_ROWS):
            acc = jnp.zeros((CONV_ROWS, LANES), F32) + cb_ref[:, cs]
            for j in range(CONV_K):
                phase, base = (j + pad) % SUBLANES, r0 + (j + pad) // SUBLANES * SUBLANES
                acc = acc + zsh_ref[phase, base:base + CONV_ROWS, cs] * cw_ref[j:j + 1, cs]
            row_chunks.append(acc)
        col_blocks.append(jnp.concatenate(row_chunks, axis=0))
    c = jnp.concatenate(col_blocks, axis=1)
    mu = jnp.mean(c, axis=-1, keepdims=True)
    cc = c - mu
    y = cc * lax.rsqrt(jnp.mean(cc * cc, axis=-1, keepdims=True) + EPS) * lg_ref[...] + lb_ref[...]
    conv_out = _dot(_silu(y).astype(BF16), wco_ref[...])
    o_n = jnp.where(is_ctx, onp_ref[...], ons_ref[...])
    a_br = _dot(o_n, wap_ref[...])
    gt = gt_ref[...]
    merged = gt[:, :D_MODEL] * a_br + gt[:, D_MODEL:] * conv_out
    mod = mod_ref[0]
    gate1 = mod[:, 2 * D_MODEL:3 * D_MODEL]
    shift2, scale2 = mod[:, 3 * D_MODEL:4 * D_MODEL], mod[:, 4 * D_MODEL:5 * D_MODEL]
    x = jnp.where(is_ctx, xp_ref[...], xs_ref[...])
    x1 = x + gate1 * _dot(merged.astype(BF16), wout_ref[...])
    x1_ref[...] = x1
    h2_ref[...] = (_rms(x1, n2_ref[...]) * (1.0 + scale2) + shift2).astype(BF16)


def _merge(rows, xp, xs, on_p, on_s, z, gt, mod3, conv_w, conv_b, ln_g, ln_b, wco_b, wap_b, wout_b, norm2_g):
    n_rows = rows.tiles * ROW_TILE
    hb = ROW_TILE // HALO
    n_halo_blocks = n_rows // HALO
    in_specs = [_row_spec(D_MODEL, rows.ctx_tile), _row_spec(D_MODEL, rows.lat_tile),
                _row_spec(ATTN_W, rows.ctx_tile), _row_spec(ATTN_W, rows.lat_tile),
                _row_spec(CONV_W),
                pl.BlockSpec((HALO, CONV_W), lambda i: (jnp.maximum(i * hb - 1, 0), 0)),
                pl.BlockSpec((HALO, CONV_W), lambda i: (jnp.minimum((i + 1) * hb, n_halo_blocks - 1), 0)),
                _row_spec(2 * D_MODEL), _mod_spec(rows),
                _const_spec((CONV_K, CONV_W)), _const_spec((1, CONV_W)), _const_spec((1, CONV_W)),
                _const_spec((1, CONV_W)), _const_spec((CONV_W, D_MODEL)), _const_spec((ATTN_W, D_MODEL)),
                _const_spec((D_MODEL, D_MODEL)), _const_spec((1, D_MODEL))]
    return pl.pallas_call(
        functools.partial(_merge_kernel, rows),
        grid=(rows.tiles,),
        in_specs=in_specs,
        out_specs=[_row_spec(D_MODEL), _row_spec(D_MODEL)],
        out_shape=[jax.ShapeDtypeStruct((n_rows, D_MODEL), F32), jax.ShapeDtypeStruct((n_rows, D_MODEL), BF16)],
        scratch_shapes=[pltpu.VMEM((ROW_TILE + 2 * HALO, CONV_W), F32),
                        pltpu.VMEM((SUBLANES, ROW_TILE + 2 * HALO - SUBLANES, CONV_W), F32)],
        compiler_params=_cparams(("arbitrary",)),
        name="conv_merge",
    )(xp, xs, on_p, on_s, z, z, z, gt, mod3, conv_w, conv_b, ln_g, ln_b, wco_b, wap_b, wout_b, norm2_g)


def _pack_cols(cols, shape):
    lane = lax.broadcasted_iota(jnp.int32, shape, 1)
    out = jnp.zeros(shape, F32)
    for j, col in enumerate(cols):
        out = jnp.where(lane == j, col, out)
    return out


def _router_kernel(h_ref, wr_ref, rb_ref, gate_ref, loc_ref, rec_ref, cnt_ref, run_ref):
    i = pl.program_id(0)

    @pl.when(i == 0)
    def _():
        run_ref[...] = jnp.zeros_like(run_ref)

    tm = ROW_TILE
    neg = jnp.float32(-jnp.inf)
    scores = jax.nn.sigmoid(_dot(h_ref[...], wr_ref[...]))
    biased = scores + rb_ref[...]
    lane_i = lax.broadcasted_iota(jnp.int32, scores.shape, 1)
    lane = lane_i.astype(F32)
    far = jnp.float32(2 * N_EXPERTS)

    def first_argmax(v):
        m = jnp.max(v, axis=-1, keepdims=True)
        return m, jnp.min(jnp.where(v == m, lane, far), axis=-1, keepdims=True)

    in_group, gscore = [], []
    for g in range(N_GROUPS):
        inb = (lane_i >= g * GROUP_SIZE) & (lane_i < (g + 1) * GROUP_SIZE)
        first_lane = g * GROUP_SIZE // LANES * LANES
        part = slice(first_lane, first_lane + LANES)
        lane_p = (lax.broadcasted_iota(jnp.int32, (tm, LANES), 1) + first_lane).astype(F32)
        in_part = (lane_p >= g * GROUP_SIZE) & (lane_p < (g + 1) * GROUP_SIZE)
        v = jnp.where(in_part, biased[:, part], neg)
        m1 = jnp.max(v, axis=-1, keepdims=True)
        i1 = jnp.min(jnp.where(v == m1, lane_p, far), axis=-1, keepdims=True)
        m2 = jnp.max(jnp.where(lane_p == i1, neg, v), axis=-1, keepdims=True)
        in_group.append(inb)
        gscore.append(m1 + m2)
    allowed = jnp.zeros(scores.shape, F32)
    for g in range(N_GROUPS):
        ahead = jnp.zeros((tm, 1), F32)
        for g2 in range(N_GROUPS):
            if g2 < g:
                ahead = ahead + (gscore[g2] >= gscore[g]).astype(F32)
            elif g2 > g:
                ahead = ahead + (gscore[g2] > gscore[g]).astype(F32)
        keep = (ahead < TOPK_GROUPS).astype(F32)
        allowed = jnp.where(in_group[g], keep, allowed)
    masked = jnp.where(allowed > 0.0, biased, neg)
    picked = jnp.zeros(scores.shape, F32)
    idxs, gates = [], []
    for _ in range(TOP_K):
        _, ik = first_argmax(masked)
        hit = lane == ik
        gates.append(jnp.sum(jnp.where(hit, scores, 0.0), axis=-1, keepdims=True))
        masked = jnp.where(hit, neg, masked)
        picked = jnp.where(hit, 1.0, picked)
        idxs.append(ik)
    gsum = functools.reduce(jnp.add, gates)
    gates = [g / gsum * ROUTE_SCALE for g in gates]
    r_i = lax.broadcasted_iota(jnp.int32, (tm, tm), 0)
    c_i = lax.broadcasted_iota(jnp.int32, (tm, tm), 1)
    before = (c_i < r_i).astype(BF16)
    local_rank = _dot(before, picked.astype(BF16))
    n_tok = jnp.sum(picked, axis=0, keepdims=True)
    n_chunks = jnp.floor((n_tok + (CHUNK - 1)) * (1.0 / CHUNK))
    lower = (lax.broadcasted_iota(jnp.int32, (N_EXPERTS, N_EXPERTS), 0)
             < lax.broadcasted_iota(jnp.int32, (N_EXPERTS, N_EXPERTS), 1)).astype(BF16)
    chunks_before = _dot(jnp.broadcast_to(n_chunks, (SUBLANES, N_EXPERTS)).astype(BF16), lower)[0:1]
    staged_off = chunks_before * CHUNK
    staged = local_rank + staged_off
    locs = [jnp.sum(jnp.where(lane == ik, staged, 0.0), axis=-1, keepdims=True) for ik in idxs]
    e_r = lax.broadcasted_iota(jnp.int32, (N_EXPERTS, N_EXPERTS), 0)
    e_c = lax.broadcasted_iota(jnp.int32, (N_EXPERTS, N_EXPERTS), 1)
    column = lambda v: jnp.sum(jnp.where(e_r == e_c, v, 0.0), axis=1, keepdims=True)
    first = column(chunks_before)
    last = first + column(n_chunks)
    region_off = column(run_ref[...] - staged_off)
    j = lax.broadcasted_iota(jnp.int32, (N_EXPERTS, REC_CHUNKS), 1).astype(F32)
    owns = (first <= j) & (j < last)
    expert = lax.broadcasted_iota(jnp.int32, (N_EXPERTS, REC_CHUNKS), 0).astype(F32)
    chunk_expert = jnp.sum(jnp.where(owns, expert, 0.0), axis=0, keepdims=True)
    chunk_off = jnp.sum(jnp.where(owns, region_off, 0.0), axis=0, keepdims=True) + j[0:1] * CHUNK
    total = jnp.broadcast_to(jnp.sum(n_chunks, axis=1, keepdims=True), (1, REC - 2 * REC_CHUNKS))
    rec_ref[0] = jnp.concatenate([chunk_expert, chunk_off, total], axis=1).astype(jnp.int32)
    run_ref[...] = run_ref[...] + jnp.floor((n_tok + (PAIR - 1)) * (1.0 / PAIR)) * PAIR
    cnt_ref[...] = run_ref[...]
    shape = gate_ref.shape
    gate_ref[...] = _pack_cols(gates, shape)
    loc_ref[...] = _pack_cols(locs, shape).astype(jnp.int32)


def _router(rows, h2, w_router_b, router_bias):
    n_rows = rows.tiles * ROW_TILE
    return pl.pallas_call(
        _router_kernel,
        grid=(rows.tiles,),
        in_specs=[_row_spec(D_MODEL), _const_spec((D_MODEL, N_EXPERTS)), _const_spec((1, N_EXPERTS))],
        out_specs=[_row_spec(LANES), _row_spec(LANES), pl.BlockSpec((1, 1, REC), lambda i: (i, 0, 0)),
                   _const_spec((1, N_EXPERTS))],
        out_shape=[jax.ShapeDtypeStruct((n_rows, LANES), F32), jax.ShapeDtypeStruct((n_rows, LANES), jnp.int32),
                   jax.ShapeDtypeStruct((rows.tiles, 1, REC), jnp.int32),
                   jax.ShapeDtypeStruct((1, N_EXPERTS), F32)],
        scratch_shapes=[pltpu.VMEM((1, N_EXPERTS), F32)],
        compiler_params=_cparams(("arbitrary",)),
        name="router",
    )(h2, w_router_b, router_bias)


def _record_copy(i, rec_hbm, rsm, sem_i, base=0):
    base = base if isinstance(base, int) else pl.multiple_of(base, REC)
    return pltpu.make_async_copy(rec_hbm.at[pl.ds(i * REC, REC)], rsm.at[pl.ds(base, REC)], sem_i)


def _load_record(i, rec_hbm, rsm, sem_i, base=0):
    cp = _record_copy(i, rec_hbm, rsm, sem_i, base)
    cp.start()
    cp.wait()


def _move_chunks(ps_ref, rsm, staged_ref, sorted_hbm, sem, chunks, to_sorted, base=0):
    total = rsm[base + 2 * REC_CHUNKS]
    spare0 = sorted_hbm.shape[0] // PACK - SPARE
    for j in chunks:
        placed_tok = jnp.where(j < total, ps_ref[rsm[base + j]] + rsm[base + REC_CHUNKS + j], spare0 + j * CHUNK)
        staged = staged_ref.at[pl.ds(j * CHUNK * PACK, CHUNK * PACK)]
        placed = sorted_hbm.at[_packed_rows(placed_tok, CHUNK, PAIR)]
        src, dst = (staged, placed) if to_sorted else (placed, staged)
        pltpu.make_async_copy(src, dst, sem).start()


def _wait_chunks(staged_ref, sorted_hbm, sem, n_chunks):
    n_rows = n_chunks * CHUNK * PACK
    pltpu.make_async_copy(staged_ref.at[pl.ds(0, n_rows)], sorted_hbm.at[pl.ds(0, n_rows)], sem).wait()


def _wait_tile_chunks(staged_ref, sorted_hbm, sem, total):
    _wait_chunks(staged_ref, sorted_hbm, sem, MAIN_CHUNKS)
    for start in range(MAIN_CHUNKS, STAGE // CHUNK, TAIL_GROUP):
        @pl.when(total > start)
        def _():
            _wait_chunks(staged_ref, sorted_hbm, sem, TAIL_GROUP)


def _move_tail_chunks(total, move):
    for start in range(MAIN_CHUNKS, STAGE // CHUNK, TAIL_GROUP):
        @pl.when(total > start)
        def _():
            move(range(start, start + TAIL_GROUP))


def _staged_block(loc, rb):
    col = lax.broadcasted_iota(jnp.int32, (loc.shape[0], STAGE_BLOCK), 1) + rb * STAGE_BLOCK
    return [loc[:, k:k + 1] == col for k in range(TOP_K)]


def _dispatch_kernel(pe_ref, ps_ref, c2_ref, rec_hbm, loc_ref, h_ref, xs_hbm, rsm, tot, stage, zbuf, sem_i, sem):
    i = pl.program_id(0)
    block_rows = MOE_BLOCK * PACK

    @pl.when(i == 0)
    def _():
        zbuf[...] = jnp.zeros_like(zbuf)
        n_blocks = xs_hbm.shape[0] // block_rows
        n_used = pe_ref[N_EXPERTS - 1] // MOE_BLOCK
        unit_rows = CLEAR_ROWS * PACK
        zunit = zbuf.at[pl.ds(0, unit_rows)]

        def clear_unit(u, carry):
            dst = xs_hbm.at[pl.ds(pl.multiple_of(u * unit_rows, unit_rows), unit_rows)]
            pltpu.make_async_copy(zunit, dst, sem.at[0]).start()
            return carry

        def clear_block(b, carry):
            dst = xs_hbm.at[pl.ds(pl.multiple_of(b * block_rows, block_rows), block_rows)]
            pltpu.make_async_copy(zbuf, dst, sem_i).start()
            return carry

        def clear_padding(e, total):
            first = (ps_ref[e] + c2_ref[e]) // CLEAR_ROWS
            last = pe_ref[e] // CLEAR_ROWS
            lax.fori_loop(first, last, clear_unit, 0)
            return total + last - first

        def unit_wait(e, carry):
            pltpu.make_async_copy(zunit, xs_hbm.at[pl.ds(0, unit_rows)], sem.at[0]).wait()
            return carry

        def block_wait(e, carry):
            pltpu.make_async_copy(zbuf, xs_hbm.at[pl.ds(0, block_rows)], sem_i).wait()
            return carry

        n_units = lax.fori_loop(0, N_EXPERTS, clear_padding, 0)
        lax.fori_loop(n_used, n_blocks, clear_block, 0)
        lax.fori_loop(0, n_units, unit_wait, 0)
        lax.fori_loop(0, n_blocks - n_used, block_wait, 0)

    _load_record(i, rec_hbm, rsm, sem_i)
    slot = i % 2
    loc, hb = loc_ref[...], h_ref[...]

    def stage_block(rb):
        onehot = jnp.zeros((ROW_TILE, STAGE_BLOCK), F32)
        for hit in _staged_block(loc, rb):
            onehot = jnp.where(hit, 1.0, onehot)
        rows = lax.dot_general(onehot.astype(BF16), hb, (((0,), (0,)), ((), ())), preferred_element_type=F32)
        _store_packed(stage.at[slot], rb * STAGE_BLOCK, rows)

    n_blocks, n_chunks = STAGE // STAGE_BLOCK, STAGE // CHUNK
    n_early = n_blocks // 2 + 1
    for rb in range(n_early):
        stage_block(rb)

    @pl.when(i >= 1)
    def _():
        _wait_tile_chunks(stage.at[1 - slot], xs_hbm, sem.at[1 - slot], tot[1 - slot])

    total = rsm[2 * REC_CHUNKS]
    tot[slot] = total
    per_round = -(-MAIN_CHUNKS // (n_blocks - n_early))
    for r, rb in enumerate(range(n_early, n_blocks)):
        stage_block(rb)
        chunks = range(r * per_round, min((r + 1) * per_round, MAIN_CHUNKS))
        assert (chunks[-1] * CHUNK) // STAGE_BLOCK <= rb
        _move_chunks(ps_ref, rsm, stage.at[slot], xs_hbm, sem.at[slot], chunks, to_sorted=True)

    _move_tail_chunks(total, lambda chunks: _move_chunks(ps_ref, rsm, stage.at[slot], xs_hbm, sem.at[slot], chunks,
                                                         to_sorted=True))

    @pl.when(i == pl.num_programs(0) - 1)
    def _():
        _wait_tile_chunks(stage.at[slot], xs_hbm, sem.at[slot], total)


def _dispatch(rows, pad_end, pad_start, counts, rec_flat, locs, h2, n_rows_sorted):
    return pl.pallas_call(
        _dispatch_kernel,
        grid_spec=pltpu.PrefetchScalarGridSpec(
            num_scalar_prefetch=3,
            grid=(rows.tiles,),
            in_specs=[pl.BlockSpec(memory_space=pl.ANY),
                      pl.BlockSpec((ROW_TILE, LANES), lambda i, pe, ps, c2: (i, 0)),
                      pl.BlockSpec((ROW_TILE, D_MODEL), lambda i, pe, ps, c2: (i, 0))],
            out_specs=pl.BlockSpec(memory_space=pl.ANY),
            scratch_shapes=[pltpu.SMEM((REC,), jnp.int32), pltpu.SMEM((2,), jnp.int32),
                            pltpu.VMEM((2, STAGE * PACK, LANES), jnp.uint32),
                            pltpu.VMEM((MOE_BLOCK * PACK, LANES), jnp.uint32),
                            pltpu.SemaphoreType.DMA, pltpu.SemaphoreType.DMA((2,))]),
        out_shape=jax.ShapeDtypeStruct(((n_rows_sorted + SPARE) * PACK, LANES), jnp.uint32),
        compiler_params=_cparams(("arbitrary",)),
        name="moe_dispatch",
    )(pad_end, pad_start, counts, rec_flat, locs, h2)


def _expert_kernel(pe_ref, ps_ref, xs_hbm, w1_ref, w3_ref, w2_ref, ys_hbm, xbuf, ybuf, w1b, w3b, w2b,
                   sem_in, sem_out):
    e = pl.program_id(0)
    first, last = ps_ref[e] // MOE_BLOCK, pe_ref[e] // MOE_BLOCK
    n_used = pe_ref[N_EXPERTS - 1] // MOE_BLOCK
    block_rows = MOE_BLOCK * PACK

    def rows_of(g):
        return pl.ds(pl.multiple_of(g * block_rows, block_rows), block_rows)

    def copy_in(g):
        slot = g % EXPERT_RING
        return pltpu.make_async_copy(xs_hbm.at[rows_of(g)], xbuf.at[slot], sem_in.at[slot])

    def copy_out(g):
        slot = g % EXPERT_RING
        return pltpu.make_async_copy(ybuf.at[slot], ys_hbm.at[rows_of(g)], sem_out.at[slot])

    @pl.when(e == 0)
    def _():
        for g in range(EXPERT_AHEAD):
            @pl.when(g < n_used)
            def _():
                copy_in(g).start()

    w1b[...] = w1_ref[...].astype(BF16)
    w3b[...] = w3_ref[...].astype(BF16)
    w2b[...] = w2_ref[...].astype(BF16)

    def block(g, carry):
        copy_in(g).wait()

        @pl.when(g + EXPERT_AHEAD < n_used)
        def _():
            copy_in(g + EXPERT_AHEAD).start()

        @pl.when(g >= EXPERT_RING)
        def _():
            copy_out(g - EXPERT_RING).wait()

        slot = g % EXPERT_RING
        parts = range(0, MOE_BLOCK, STAGE_BLOCK)
        xb = jnp.concatenate([_load_packed(xbuf.at[slot], t0, STAGE_BLOCK) for t0 in parts], axis=0)
        hid = (_silu(_dot(xb, w1b[...])) * _dot(xb, w3b[...])).astype(BF16)
        y = _dot(hid, w2b[...]).astype(BF16).astype(F32)
        for t0 in parts:
            _store_packed(ybuf.at[slot], t0, y[t0:t0 + STAGE_BLOCK])
        copy_out(g).start()
        return carry

    lax.fori_loop(first, last, block, 0)

    @pl.when(e == N_EXPERTS - 1)
    def _():
        for back in range(EXPERT_RING, 0, -1):
            @pl.when(n_used >= back)
            def _():
                copy_out(n_used - back).wait()


def _experts(pad_end, pad_start, xs, w1, w3, w2):
    expert_spec = lambda shape: pl.BlockSpec((None,) + shape, lambda e, pe, ps: (e, 0, 0))
    any_spec = pl.BlockSpec(memory_space=pl.ANY)
    return pl.pallas_call(
        _expert_kernel,
        grid_spec=pltpu.PrefetchScalarGridSpec(
            num_scalar_prefetch=2,
            grid=(N_EXPERTS,),
            in_specs=[any_spec, expert_spec((D_MODEL, EXPERT_HIDDEN)), expert_spec((D_MODEL, EXPERT_HIDDEN)),
                      expert_spec((EXPERT_HIDDEN, D_MODEL))],
            out_specs=any_spec,
            scratch_shapes=[pltpu.VMEM((EXPERT_RING, MOE_BLOCK * PACK, LANES), jnp.uint32),
                            pltpu.VMEM((EXPERT_RING, MOE_BLOCK * PACK, LANES), jnp.uint32),
                            pltpu.VMEM((D_MODEL, EXPERT_HIDDEN), BF16),
                            pltpu.VMEM((D_MODEL, EXPERT_HIDDEN), BF16),
                            pltpu.VMEM((EXPERT_HIDDEN, D_MODEL), BF16),
                            pltpu.SemaphoreType.DMA((EXPERT_RING,)), pltpu.SemaphoreType.DMA((EXPERT_RING,))]),
        out_shape=jax.ShapeDtypeStruct(xs.shape, jnp.uint32),
        input_output_aliases={2: 0},
        compiler_params=_cparams(("arbitrary",)),
        name="moe_experts",
    )(pad_end, pad_start, xs, w1, w3, w2)


def _combine_kernel(rows, ps_ref, rec_hbm, ys_hbm, loc_ref, x1_ref, h_ref, gate_ref, mod_ref, ws1_ref, ws3_ref,
                    ws2_ref, nf_ref, outp_ref, outs_ref, rsm, ybuf, sem_i, sem):
    i = pl.program_id(0)
    slot = i % 2

    n_chunks = STAGE // CHUNK
    nxt = jnp.minimum(i + 1, pl.num_programs(0) - 1)
    other = 1 - slot

    def fetch(s, chunks):
        _move_chunks(ps_ref, rsm, ybuf.at[s], ys_hbm, sem.at[s], chunks, to_sorted=False, base=s * REC)

    def fetch_rest(s):
        _move_tail_chunks(rsm[s * REC + 2 * REC_CHUNKS], lambda chunks: fetch(s, chunks))

    next_record = _record_copy(nxt, rec_hbm, rsm, sem_i.at[other], base=other * REC)
    next_record.start()

    @pl.when(i == 0)
    def _():
        ybuf[...] = jnp.zeros_like(ybuf)
        _load_record(i, rec_hbm, rsm, sem_i.at[slot], base=slot * REC)
        fetch(slot, range(MAIN_CHUNKS))
        fetch_rest(slot)

    hb = h_ref[...]
    shared = _dot((_silu(_dot(hb, ws1_ref[...])) * _dot(hb, ws3_ref[...])).astype(BF16), ws2_ref[...])
    staged_y = ybuf.at[slot]
    _wait_tile_chunks(staged_y, ys_hbm, sem.at[slot], rsm[slot * REC + 2 * REC_CHUNKS])
    next_record.wait()
    loc, g = loc_ref[...], gate_ref[...]
    routed = jnp.zeros((ROW_TILE, D_MODEL), F32)
    for rb in range(STAGE // STAGE_BLOCK):
        gm = jnp.zeros((ROW_TILE, STAGE_BLOCK), F32)
        for k, hit in enumerate(_staged_block(loc, rb)):
            gm = jnp.where(hit, g[:, k:k + 1], gm)
        g_hi = gm.astype(BF16)
        g_lo = (gm - g_hi.astype(F32)).astype(BF16)
        yb = _load_packed(staged_y, rb * STAGE_BLOCK, STAGE_BLOCK)
        routed = routed + _dot(g_hi, yb) + _dot(g_lo, yb)
        per_block = STAGE_BLOCK // CHUNK
        fetch(other, range(min(rb * per_block, MAIN_CHUNKS), min((rb + 1) * per_block, MAIN_CHUNKS)))
    fetch_rest(other)
    gate2 = mod_ref[0][:, 5 * D_MODEL:6 * D_MODEL]
    x2 = x1_ref[...] + gate2 * (routed + shared)
    out = _rms(x2, nf_ref[...])

    @pl.when(i == pl.num_programs(0) - 1)
    def _():
        _wait_tile_chunks(ybuf.at[other], ys_hbm, sem.at[other], rsm[other * REC + 2 * REC_CHUNKS])

    @pl.when(i < rows.tiles_ctx)
    def _():
        outp_ref[...] = out

    @pl.when(i >= rows.tiles_ctx)
    def _():
        outs_ref[...] = out


def _combine(rows, pad_start, rec_flat, ys, locs, x1, h2, gates, mod3, ws1_b, ws3_b, ws2_b, normf_g):
    drop = lambda spec: pl.BlockSpec(spec.block_shape, lambda i, ps, f=spec.index_map: f(i))
    any_spec = pl.BlockSpec(memory_space=pl.ANY)
    return pl.pallas_call(
        functools.partial(_combine_kernel, rows),
        grid_spec=pltpu.PrefetchScalarGridSpec(
            num_scalar_prefetch=1,
            grid=(rows.tiles,),
            in_specs=[any_spec, any_spec]
                     + [drop(s) for s in (
                         _row_spec(LANES), _row_spec(D_MODEL), _row_spec(D_MODEL), _row_spec(LANES),
                         _mod_spec(rows),
                         _const_spec((D_MODEL, EXPERT_HIDDEN)), _const_spec((D_MODEL, EXPERT_HIDDEN)),
                         _const_spec((EXPERT_HIDDEN, D_MODEL)), _const_spec((1, D_MODEL)))],
            out_specs=[drop(_row_spec(D_MODEL, rows.ctx_tile)), drop(_row_spec(D_MODEL, rows.lat_tile))],
            scratch_shapes=[pltpu.SMEM((2 * REC,), jnp.int32),
                            pltpu.VMEM((2, STAGE * PACK, LANES), jnp.uint32),
                            pltpu.SemaphoreType.DMA((2,)), pltpu.SemaphoreType.DMA((2,))]),
        out_shape=[jax.ShapeDtypeStruct((rows.tiles_ctx * ROW_TILE, D_MODEL), F32),
                   jax.ShapeDtypeStruct((rows.tiles_lat * ROW_TILE, D_MODEL), F32)],
        compiler_params=_cparams(("arbitrary",)),
        name="moe_combine",
    )(pad_start, rec_flat, ys, locs, x1, h2, gates, mod3, ws1_b, ws3_b, ws2_b, normf_g)


def kernel(x_prompt, x_sample, cache_k, cache_v, c, c_ctx, w_mod, b_mod, norm1_g, w_in, lambda_qk, subln_g,
           w_attn_proj, conv_w, conv_b, conv_ln_g, conv_ln_b, w_conv_out, w_out, norm2_g, w_router,
           router_bias, w1, w3, w2, ws1, ws3, ws2, normf_g):
    batch, seq, _ = x_prompt.shape
    dec_batch, dec_seq, _ = x_sample.shape
    past = cache_k.shape[2]
    l = 0
    rows_p, rows_s = batch * seq, dec_batch * dec_seq
    n_rows = rows_p + rows_s
    rows = Rows(rows_p // ROW_TILE, rows_s // ROW_TILE, seq // ROW_TILE, dec_seq // ROW_TILE)
    xp = x_prompt.reshape(rows_p, D_MODEL)
    xs = x_sample.reshape(rows_s, D_MODEL)
    row = lambda a: a.reshape(1, -1)

    cc = jnp.zeros((MOD_ROWS, D_MODEL), F32).at[0].set(c_ctx).at[1:1 + dec_batch].set(c)
    mod3 = _mod_table(cc, w_mod[l], b_mod[l]).reshape(MOD_ROWS, 1, N_MOD * D_MODEL)

    q, k, v, z, gt, state_k, state_v = _inproj(rows, xp, xs, mod3, row(norm1_g[l]), w_in[l].astype(BF16),
                                               _rope_tables(dec_seq))
    lq, sg = lambda_qk[l], row(subln_g[l])
    on_p = _attention(q, k, v, lq, sg, 0, batch, seq, CTX_HEADS_PER_STEP)
    cache = (cache_k[:, l].reshape(dec_batch, past, ATTN_W), cache_v[:, l].reshape(dec_batch, past, ATTN_W))
    on_s = _attention(q, k, v, lq, sg, rows_p, dec_batch, dec_seq, LAT_HEADS_PER_STEP, cache=cache)
    x1, h2 = _merge(rows, xp, xs, on_p, on_s, z, gt, mod3, conv_w[l], row(conv_b[l]), row(conv_ln_g[l]),
                    row(conv_ln_b[l]), w_conv_out[l].astype(BF16), w_attn_proj[l].astype(BF16),
                    w_out[l].astype(BF16), row(norm2_g[l]))

    gates, locs, rec, counts = _router(rows, h2, w_router[l].astype(BF16), row(router_bias[l]))
    counts = counts[0].astype(jnp.int32)
    padded = (counts + CHUNK + MOE_BLOCK - 1) // MOE_BLOCK * MOE_BLOCK
    pad_end = jnp.cumsum(padded).astype(jnp.int32)
    pad_start = pad_end - padded
    max_rows = n_rows * TOP_K + rows.tiles * N_EXPERTS * (PAIR - 1) + N_EXPERTS * (CHUNK + MOE_BLOCK - 1)
    n_blocks = -(-max_rows // MOE_BLOCK)
    rec_flat = rec.reshape(-1)
    x_sorted = _dispatch(rows, pad_end, pad_start, counts, rec_flat, locs, h2, n_blocks * MOE_BLOCK)
    y_sorted = _experts(pad_end, pad_start, x_sorted, w1[l], w3[l], w2[l])
    y_p, y_s = _combine(rows, pad_start, rec_flat, y_sorted, locs, x1, h2, gates, mod3,
                        ws1[l].astype(BF16), ws3[l].astype(BF16), ws2[l].astype(BF16), row(normf_g))
    return (y_p.reshape(batch, seq, D_MODEL), y_s.reshape(dec_batch, dec_seq, D_MODEL),
            state_k.reshape(batch, 1, seq, N_HEADS, 2, HEAD_DIM),
            state_v.reshape(batch, 1, seq, N_HEADS, 2 * HEAD_DIM))
```

```python
import functools
import math
from typing import NamedTuple

import jax
import jax.numpy as jnp
from jax import lax
from jax.experimental import pallas as pl
from jax.experimental.pallas import tpu as pltpu

D_MODEL = 1024
GRID_W = 64
N_HEADS = 8
HEAD_DIM = 64
ATTN_W = N_HEADS * 2 * HEAD_DIM
CONV_W = 512
CONV_K = 31
N_EXPERTS = 256
TOP_K = 8
N_GROUPS = 8
TOPK_GROUPS = 4
GROUP_SIZE = N_EXPERTS // N_GROUPS
EXPERT_HIDDEN = 256
ROUTE_SCALE = 2.5
ROPE_BASE = 10000.0
EPS = 1e-6
N_MOD = 6
IN_COLS = 3 * ATTN_W + 2 * CONV_W + 2 * D_MODEL
LAM_INIT = 0.8 - 0.6 * math.exp(-0.3 * 0)
LOG2E = math.log2(math.e)

LANES = 128
SUBLANES = 8
VMEM_LIMIT = 56 * 1024 * 1024
HALO = 16
CONV_ROWS = 64
ROW_TILE = 256
CTX_HEADS_PER_STEP = 8
LAT_HEADS_PER_STEP = 2
ATTN_KEY_CHUNK = 512
MOE_BLOCK = 512
STAGE_BLOCK = 256
CLEAR_ROWS = 64
EXPERT_AHEAD = 3
EXPERT_RING = EXPERT_AHEAD + 1
MOD_ROWS = 8
MOD_COL_TILES = 4
PACK = D_MODEL // 2 // LANES
CHUNK = 4
PAIR = SUBLANES // PACK
STAGE = ROW_TILE * TOP_K + N_EXPERTS * (CHUNK - 1)
SPARE = -(-STAGE // MOE_BLOCK) * MOE_BLOCK
REC_CHUNKS = 768
REC = 2048
assert STAGE // CHUNK <= REC_CHUNKS and REC_CHUNKS % LANES == 0

BF16 = jnp.bfloat16
F32 = jnp.float32


class Rows(NamedTuple):
    tiles_ctx: int
    tiles_lat: int
    tiles_per_ctx: int
    tiles_per_lat: int

    @property
    def tiles(self):
        return self.tiles_ctx + self.tiles_lat

    def ctx_tile(self, i):
        return jnp.minimum(i, self.tiles_ctx - 1)

    def lat_tile(self, i):
        return jnp.maximum(i - self.tiles_ctx, 0)

    def mod_row(self, i):
        return jnp.where(i < self.tiles_ctx, 0, 1 + self.lat_tile(i) // self.tiles_per_lat)

    def seq_tile(self, i):
        is_ctx = i < self.tiles_ctx
        per = jnp.where(is_ctx, self.tiles_per_ctx, self.tiles_per_lat)
        return jnp.where(is_ctx, i % self.tiles_per_ctx, self.lat_tile(i) % self.tiles_per_lat), per


def _cparams(sem):
    return pltpu.CompilerParams(dimension_semantics=sem, vmem_limit_bytes=VMEM_LIMIT)


def _row_spec(width, tile=lambda i: i):
    return pl.BlockSpec((ROW_TILE, width), lambda i: (tile(i), 0))


def _const_spec(shape):
    return pl.BlockSpec(shape, lambda i: (0,) * len(shape))


def _mod_spec(rows):
    return pl.BlockSpec((1, 1, N_MOD * D_MODEL), lambda i: (rows.mod_row(i), 0, 0))


def _silu(x):
    return x * jax.nn.sigmoid(x)


def _dot(a, b):
    return jnp.dot(a, b, preferred_element_type=F32)


def _dot_nt(a, b):
    return lax.dot_general(a, b, (((1,), (1,)), ((), ())), preferred_element_type=F32)


def _rms(x, g):
    return x * lax.rsqrt(jnp.mean(x * x, axis=-1, keepdims=True) + EPS) * g


def _packed_rows(token, n_tokens, align):
    return pl.ds(pl.multiple_of(token * PACK, align * PACK), n_tokens * PACK)


def _store_packed(ref, token0, x):
    half = D_MODEL // 2
    lo = pltpu.bitcast(x[:, :half], jnp.uint32) >> 16
    hi = pltpu.bitcast(x[:, half:], jnp.uint32) & jnp.uint32(0xFFFF0000)
    w = lo | hi
    for s in range(PACK):
        ref[pl.ds(token0 * PACK + s, x.shape[0], stride=PACK), :] = w[:, s * LANES:(s + 1) * LANES]


def _load_packed(ref, token0, n_tokens):
    w = jnp.concatenate([ref[pl.ds(token0 * PACK + s, n_tokens, stride=PACK), :] for s in range(PACK)], axis=1)
    lo = pltpu.bitcast(w << 16, F32)
    hi = pltpu.bitcast(w & jnp.uint32(0xFFFF0000), F32)
    return jnp.concatenate([lo, hi], axis=1).astype(BF16)


def _mod_kernel(c_ref, w_ref, b_ref, o_ref):
    a = _silu(c_ref[...]).astype(BF16)
    o_ref[...] = _dot(a, w_ref[...].astype(BF16)) + b_ref[...]


def _mod_table(cc, w_mod, b_mod):
    n = w_mod.shape[1]
    tn = n // MOD_COL_TILES
    return pl.pallas_call(
        _mod_kernel,
        grid=(MOD_COL_TILES,),
        in_specs=[pl.BlockSpec((MOD_ROWS, D_MODEL), lambda j: (0, 0)),
                  pl.BlockSpec((D_MODEL, tn), lambda j: (0, j)),
                  pl.BlockSpec((1, tn), lambda j: (0, j))],
        out_specs=pl.BlockSpec((MOD_ROWS, tn), lambda j: (0, j)),
        out_shape=jax.ShapeDtypeStruct((MOD_ROWS, n), F32),
        compiler_params=_cparams(("arbitrary",)),
        name="mod_table",
    )(cc, w_mod, b_mod.reshape(1, n))


def _rope(x, cos, sin, lane_lo):
    swapped = jnp.where(lane_lo, pltpu.roll(x, LANES - 16, axis=1), pltpu.roll(x, 16, axis=1))
    return x * cos + swapped * sin


def _inproj_kernel(rows, xp_ref, xs_ref, mod_ref, g_ref, w_ref, cos_ref, sin_ref,
                   q_ref, k_ref, v_ref, z_ref, gt_ref, ks_ref, vs_ref):
    is_ctx = pl.program_id(0) < rows.tiles_ctx
    mod = mod_ref[0]
    shift, scale = mod[:, 0:D_MODEL], mod[:, D_MODEL:2 * D_MODEL]
    x = jnp.where(is_ctx, xp_ref[...], xs_ref[...])
    h = _rms(x, g_ref[...]) * (1.0 + scale) + shift
    hb = h.astype(BF16)
    q = _dot(hb, w_ref[:, 0:ATTN_W]) * (HEAD_DIM ** -0.5 * LOG2E)
    k = _dot(hb, w_ref[:, ATTN_W:2 * ATTN_W])
    v = _dot(hb, w_ref[:, 2 * ATTN_W:3 * ATTN_W])

    cos, sin = cos_ref[...], sin_ref[...]
    lane_lo = (lax.broadcasted_iota(jnp.int32, cos.shape, 1) % 32) < 16
    for j in range(ATTN_W // LANES):
        sl = slice(j * LANES, (j + 1) * LANES)
        q_ref[:, sl] = jnp.where(is_ctx, q[:, sl], _rope(q[:, sl], cos, sin, lane_lo)).astype(BF16)
        k_ref[:, sl] = jnp.where(is_ctx, k[:, sl], _rope(k[:, sl], cos, sin, lane_lo)).astype(BF16)

    v_ref[...] = v.astype(BF16)
    u = _dot(hb, w_ref[:, 3 * ATTN_W:3 * ATTN_W + 2 * CONV_W])
    z_ref[...] = u[:, :CONV_W] * jax.nn.sigmoid(u[:, CONV_W:])
    gt_ref[...] = jax.nn.sigmoid(_dot(hb, w_ref[:, 3 * ATTN_W + 2 * CONV_W:]))

    @pl.when(is_ctx)
    def _():
        ks_ref[...] = k
        vs_ref[...] = v


def _inproj(rows, xp, xs, mod3, norm_g, w_in_b, rope):
    n_rows = rows.tiles * ROW_TILE
    rows_p = rows.tiles_ctx * ROW_TILE
    rope_spec = pl.BlockSpec((ROW_TILE, LANES), lambda i: (rows.lat_tile(i) % rows.tiles_per_lat, 0))
    shapes = lambda w, dt, r=n_rows: jax.ShapeDtypeStruct((r, w), dt)
    return pl.pallas_call(
        functools.partial(_inproj_kernel, rows),
        grid=(rows.tiles,),
        in_specs=[_row_spec(D_MODEL, rows.ctx_tile), _row_spec(D_MODEL, rows.lat_tile), _mod_spec(rows),
                  _const_spec((1, D_MODEL)),
                  pl.BlockSpec((D_MODEL, IN_COLS), lambda i: (0, 0), pipeline_mode=pl.Buffered(1)),
                  rope_spec, rope_spec],
        out_specs=[_row_spec(ATTN_W)] * 3 + [_row_spec(CONV_W), _row_spec(2 * D_MODEL)]
                  + [_row_spec(ATTN_W, rows.ctx_tile)] * 2,
        out_shape=[shapes(ATTN_W, BF16)] * 3 + [shapes(CONV_W, F32), shapes(2 * D_MODEL, F32)]
                  + [shapes(ATTN_W, F32, rows_p)] * 2,
        compiler_params=_cparams(("arbitrary",)),
        name="inproj",
    )(xp, xs, mod3, norm_g, w_in_b, *rope)


def _rope_tables(n_tokens):
    t = jnp.arange(n_tokens, dtype=jnp.int32)
    pos = jnp.stack([t // GRID_W, t % GRID_W], axis=-1).astype(F32)
    half = HEAD_DIM // 2
    inv = ROPE_BASE ** (-jnp.arange(0, half, 2, dtype=F32) / half)
    ang = pos[:, :, None] * inv
    cos, sin = jnp.cos(ang), jnp.sin(ang)
    cos64 = jnp.concatenate([cos[:, 0], cos[:, 0], cos[:, 1], cos[:, 1]], axis=-1)
    sin64 = jnp.concatenate([-sin[:, 0], sin[:, 0], -sin[:, 1], sin[:, 1]], axis=-1)
    return jnp.tile(cos64, (1, LANES // HEAD_DIM)), jnp.tile(sin64, (1, LANES // HEAD_DIM))


def _lane_groups(x):
    return [x[:, j * LANES:(j + 1) * LANES] for j in range(x.shape[1] // LANES)]


def _attn_kernel(has_cache, heads, lq_ref, sg_ref, q_ref, k_ref, v_ref, *rest):
    if has_cache:
        ck_ref, cv_ref, o_ref, *bufs = rest
    else:
        o_ref, *bufs = rest

    def s_ref(u, mp):
        return bufs[(u % 2) * 2 + mp]

    lq = lq_ref[...]
    lam = (jnp.exp(jnp.sum(lq[0:1] * lq[1:2], axis=-1, keepdims=True))
           - jnp.exp(jnp.sum(lq[2:3] * lq[3:4], axis=-1, keepdims=True)) + LAM_INIT)
    tq, seq = q_ref.shape[0], k_ref.shape[0]
    chunks = [(off, min(ATTN_KEY_CHUNK, seq - off), False) for off in range(0, seq, ATTN_KEY_CHUNK)]
    if has_cache:
        chunks.append((seq, ck_ref.shape[0], True))
    first = lax.broadcasted_iota(jnp.int32, (tq, LANES), 1) < HEAD_DIM
    neg = jnp.full((tq, LANES), -jnp.inf, F32)
    heads_state = [dict() for _ in range(heads)]

    def lanes(u):
        return slice(u * LANES, (u + 1) * LANES)

    def load(main_ref, cache_ref, u, chunk):
        off, size, cached = chunk
        if cached:
            return cache_ref[:, lanes(u)].astype(BF16)
        return main_ref[off:off + size, lanes(u)]

    def scores(u, chunk):
        st = heads_state[u]
        if "q" not in st:
            q = q_ref[:, lanes(u)]
            zero = jnp.zeros_like(q)
            st["q"] = (jnp.where(first, q, zero), jnp.where(first, zero, q))
            st["macc"] = [neg, neg]
        kk = load(k_ref, ck_ref if has_cache else None, u, chunk)
        off, size, _ = chunk
        for mp in range(2):
            s = _dot_nt(st["q"][mp], kk)
            s_ref(u, mp)[:, off:off + size] = s
            st["macc"][mp] = functools.reduce(jnp.maximum, _lane_groups(s), st["macc"][mp])

    def exps(u, chunk):
        st = heads_state[u]
        if "m" not in st:
            st["m"] = [jnp.max(a, axis=-1, keepdims=True) for a in st["macc"]]
            st["lacc"] = [jnp.zeros((tq, LANES), F32)] * 2
        off, size, _ = chunk
        for mp in range(2):
            e = jnp.exp2(s_ref(u, mp)[:, off:off + size] - st["m"][mp])
            s_ref(u, mp)[:, off:off + size] = e
            st["lacc"][mp] = functools.reduce(jnp.add, _lane_groups(e), st["lacc"][mp])

    def values(u, chunk):
        st = heads_state[u]
        if "r" not in st:
            l1, l2 = [jnp.sum(a, axis=-1, keepdims=True) for a in st["lacc"]]
            st["r"] = (1.0 / l1, lam / l2)
            st["o"] = jnp.zeros((tq, LANES), F32)
        off, size, _ = chunk
        w = s_ref(u, 0)[:, off:off + size] * st["r"][0] - s_ref(u, 1)[:, off:off + size] * st["r"][1]
        st["o"] = st["o"] + _dot(w.astype(BF16), load(v_ref, cv_ref if has_cache else None, u, chunk))

    stages = (scores, exps, values)
    for phase in range(heads + len(stages) - 1):
        for chunk in chunks:
            for s in reversed(range(len(stages))):
                u = phase - s
                if 0 <= u < heads:
                    stages[s](u, chunk)
        u = phase - (len(stages) - 1)
        if 0 <= u < heads:
            o_ref[:, lanes(u)] = (_rms(heads_state[u]["o"], sg_ref[...]) * (1.0 - LAM_INIT)).astype(BF16)


def _attention(q, k, v, lambda_qk, subln_g, row0, batch, seq_len, heads, cache=None):
    tq = ROW_TILE
    q_tiles = seq_len // tq
    assert row0 % seq_len == 0
    seq0 = row0 // seq_len
    width = heads * LANES
    head_q = pl.BlockSpec((tq, width), lambda b, h, i: (seq0 * q_tiles + b * q_tiles + i, h))
    head_kv = pl.BlockSpec((seq_len, width), lambda b, h, i: (seq0 + b, h))
    in_specs = [pl.BlockSpec((4, HEAD_DIM), lambda b, h, i: (0, 0)),
                pl.BlockSpec((1, 2 * HEAD_DIM), lambda b, h, i: (0, 0)),
                head_q, head_kv, head_kv]
    args = [lambda_qk, subln_g, q, k, v]
    n_keys = seq_len
    if cache is not None:
        past = cache[0].shape[1]
        n_keys += past
        head_cache = pl.BlockSpec((None, past, width), lambda b, h, i: (b, 0, h))
        in_specs += [head_cache, head_cache]
        args += list(cache)
    return pl.pallas_call(
        functools.partial(_attn_kernel, cache is not None, heads),
        grid=(batch, N_HEADS // heads, q_tiles),
        in_specs=in_specs,
        out_specs=pl.BlockSpec((tq, width), lambda b, h, i: (b * q_tiles + i, h)),
        out_shape=jax.ShapeDtypeStruct((batch * seq_len, ATTN_W), BF16),
        scratch_shapes=[pltpu.VMEM((tq, n_keys), F32)] * 4,
        compiler_params=_cparams(("parallel", "parallel", "arbitrary")),
        name="attn_latent" if cache is not None else "attn_ctx",
    )(*args)


def _merge_kernel(rows, xp_ref, xs_ref, onp_ref, ons_ref, z_ref, zp_ref, zn_ref, gt_ref, mod_ref, cw_ref, cb_ref,
                  lg_ref, lb_ref, wco_ref, wap_ref, wout_ref, n2_ref, x1_ref, h2_ref, zext_ref, zsh_ref):
    i = pl.program_id(0)
    is_ctx = i < rows.tiles_ctx
    tm = ROW_TILE
    t, per = rows.seq_tile(i)
    zext_ref[HALO:HALO + tm, :] = z_ref[...]
    zext_ref[0:HALO, :] = jnp.where(t == 0, 0.0, zp_ref[...])
    zext_ref[HALO + tm:, :] = jnp.where(t == per - 1, 0.0, zn_ref[...])
    pad = HALO - CONV_K // 2
    shifted_rows = zsh_ref.shape[1]
    for s in range(SUBLANES):
        zsh_ref[s] = zext_ref[s:s + shifted_rows, :]
    col_blocks = []
    for cb in range(CONV_W // LANES):
        cs = slice(cb * LANES, (cb + 1) * LANES)
        row_chunks = []
        for r0 in range(0, tm, CONV_ROWS):
            acc = jnp.zeros((CONV_ROWS, LANES), F32) + cb_ref[:, cs]
            for j in range(CONV_K):
                phase, base = (j + pad) % SUBLANES, r0 + (j + pad) // SUBLANES * SUBLANES
                acc = acc + zsh_ref[phase, base:base + CONV_ROWS, cs] * cw_ref[j:j + 1, cs]
            row_chunks.append(acc)
        col_blocks.append(jnp.concatenate(row_chunks, axis=0))
    c = jnp.concatenate(col_blocks, axis=1)
    mu = jnp.mean(c, axis=-1, keepdims=True)
    cc = c - mu
    y = cc * lax.rsqrt(jnp.mean(cc * cc, axis=-1, keepdims=True) + EPS) * lg_ref[...] + lb_ref[...]
    conv_out = _dot(_silu(y).astype(BF16), wco_ref[...])
    o_n = jnp.where(is_ctx, onp_ref[...], ons_ref[...])
    a_br = _dot(o_n, wap_ref[...])
    gt = gt_ref[...]
    merged = gt[:, :D_MODEL] * a_br + gt[:, D_MODEL:] * conv_out
    mod = mod_ref[0]
    gate1 = mod[:, 2 * D_MODEL:3 * D_MODEL]
    shift2, scale2 = mod[:, 3 * D_MODEL:4 * D_MODEL], mod[:, 4 * D_MODEL:5 * D_MODEL]
    x = jnp.where(is_ctx, xp_ref[...], xs_ref[...])
    x1 = x + gate1 * _dot(merged.astype(BF16), wout_ref[...])
    x1_ref[...] = x1
    h2_ref[...] = (_rms(x1, n2_ref[...]) * (1.0 + scale2) + shift2).astype(BF16)


def _merge(rows, xp, xs, on_p, on_s, z, gt, mod3, conv_w, conv_b, ln_g, ln_b, wco_b, wap_b, wout_b, norm2_g):
    n_rows = rows.tiles * ROW_TILE
    hb = ROW_TILE // HALO
    n_halo_blocks = n_rows // HALO
    in_specs = [_row_spec(D_MODEL, rows.ctx_tile), _row_spec(D_MODEL, rows.lat_tile),
                _row_spec(ATTN_W, rows.ctx_tile), _row_spec(ATTN_W, rows.lat_tile),
                _row_spec(CONV_W),
                pl.BlockSpec((HALO, CONV_W), lambda i: (jnp.maximum(i * hb - 1, 0), 0)),
                pl.BlockSpec((HALO, CONV_W), lambda i: (jnp.minimum((i + 1) * hb, n_halo_blocks - 1), 0)),
                _row_spec(2 * D_MODEL), _mod_spec(rows),
                _const_spec((CONV_K, CONV_W)), _const_spec((1, CONV_W)), _const_spec((1, CONV_W)),
                _const_spec((1, CONV_W)), _const_spec((CONV_W, D_MODEL)), _const_spec((ATTN_W, D_MODEL)),
                _const_spec((D_MODEL, D_MODEL)), _const_spec((1, D_MODEL))]
    return pl.pallas_call(
        functools.partial(_merge_kernel, rows),
        grid=(rows.tiles,),
        in_specs=in_specs,
        out_specs=[_row_spec(D_MODEL), _row_spec(D_MODEL)],
        out_shape=[jax.ShapeDtypeStruct((n_rows, D_MODEL), F32), jax.ShapeDtypeStruct((n_rows, D_MODEL), BF16)],
        scratch_shapes=[pltpu.VMEM((ROW_TILE + 2 * HALO, CONV_W), F32),
                        pltpu.VMEM((SUBLANES, ROW_TILE + 2 * HALO - SUBLANES, CONV_W), F32)],
        compiler_params=_cparams(("arbitrary",)),
        name="conv_merge",
    )(xp, xs, on_p, on_s, z, z, z, gt, mod3, conv_w, conv_b, ln_g, ln_b, wco_b, wap_b, wout_b, norm2_g)


def _pack_cols(cols, shape):
    lane = lax.broadcasted_iota(jnp.int32, shape, 1)
    out = jnp.zeros(shape, F32)
    for j, col in enumerate(cols):
        out = jnp.where(lane == j, col, out)
    return out


def _router_kernel(h_ref, wr_ref, rb_ref, gate_ref, loc_ref, rec_ref, cnt_ref, run_ref):
    i = pl.program_id(0)

    @pl.when(i == 0)
    def _():
        run_ref[...] = jnp.zeros_like(run_ref)

    tm = ROW_TILE
    neg = jnp.float32(-jnp.inf)
    scores = jax.nn.sigmoid(_dot(h_ref[...], wr_ref[...]))
    biased = scores + rb_ref[...]
    lane_i = lax.broadcasted_iota(jnp.int32, scores.shape, 1)
    lane = lane_i.astype(F32)
    far = jnp.float32(2 * N_EXPERTS)

    def first_argmax(v):
        m = jnp.max(v, axis=-1, keepdims=True)
        return m, jnp.min(jnp.where(v == m, lane, far), axis=-1, keepdims=True)

    in_group, gscore = [], []
    for g in range(N_GROUPS):
        inb = (lane_i >= g * GROUP_SIZE) & (lane_i < (g + 1) * GROUP_SIZE)
        first_lane = g * GROUP_SIZE // LANES * LANES
        part = slice(first_lane, first_lane + LANES)
        lane_p = (lax.broadcasted_iota(jnp.int32, (tm, LANES), 1) + first_lane).astype(F32)
        in_part = (lane_p >= g * GROUP_SIZE) & (lane_p < (g + 1) * GROUP_SIZE)
        v = jnp.where(in_part, biased[:, part], neg)
        m1 = jnp.max(v, axis=-1, keepdims=True)
        i1 = jnp.min(jnp.where(v == m1, lane_p, far), axis=-1, keepdims=True)
        m2 = jnp.max(jnp.where(lane_p == i1, neg, v), axis=-1, keepdims=True)
        in_group.append(inb)
        gscore.append(m1 + m2)
    allowed = jnp.zeros(scores.shape, F32)
    for g in range(N_GROUPS):
        ahead = jnp.zeros((tm, 1), F32)
        for g2 in range(N_GROUPS):
            if g2 < g:
                ahead = ahead + (gscore[g2] >= gscore[g]).astype(F32)
            elif g2 > g:
                ahead = ahead + (gscore[g2] > gscore[g]).astype(F32)
        keep = (ahead < TOPK_GROUPS).astype(F32)
        allowed = jnp.where(in_group[g], keep, allowed)
    masked = jnp.where(allowed > 0.0, biased, neg)
    picked = jnp.zeros(scores.shape, F32)
    idxs, gates = [], []
    for _ in range(TOP_K):
        _, ik = first_argmax(masked)
        hit = lane == ik
        gates.append(jnp.sum(jnp.where(hit, scores, 0.0), axis=-1, keepdims=True))
        masked = jnp.where(hit, neg, masked)
        picked = jnp.where(hit, 1.0, picked)
        idxs.append(ik)
    gsum = functools.reduce(jnp.add, gates)
    gates = [g / gsum * ROUTE_SCALE for g in gates]
    r_i = lax.broadcasted_iota(jnp.int32, (tm, tm), 0)
    c_i = lax.broadcasted_iota(jnp.int32, (tm, tm), 1)
    before = (c_i < r_i).astype(BF16)
    local_rank = _dot(before, picked.astype(BF16))
    n_tok = jnp.sum(picked, axis=0, keepdims=True)
    n_chunks = jnp.floor((n_tok + (CHUNK - 1)) * (1.0 / CHUNK))
    lower = (lax.broadcasted_iota(jnp.int32, (N_EXPERTS, N_EXPERTS), 0)
             < lax.broadcasted_iota(jnp.int32, (N_EXPERTS, N_EXPERTS), 1)).astype(BF16)
    chunks_before = _dot(jnp.broadcast_to(n_chunks, (SUBLANES, N_EXPERTS)).astype(BF16), lower)[0:1]
    staged_off = chunks_before * CHUNK
    staged = local_rank + staged_off
    locs = [jnp.sum(jnp.where(lane == ik, staged, 0.0), axis=-1, keepdims=True) for ik in idxs]
    e_r = lax.broadcasted_iota(jnp.int32, (N_EXPERTS, N_EXPERTS), 0)
    e_c = lax.broadcasted_iota(jnp.int32, (N_EXPERTS, N_EXPERTS), 1)
    column = lambda v: jnp.sum(jnp.where(e_r == e_c, v, 0.0), axis=1, keepdims=True)
    first = column(chunks_before)
    last = first + column(n_chunks)
    region_off = column(run_ref[...] - staged_off)
    j = lax.broadcasted_iota(jnp.int32, (N_EXPERTS, REC_CHUNKS), 1).astype(F32)
    owns = (first <= j) & (j < last)
    expert = lax.broadcasted_iota(jnp.int32, (N_EXPERTS, REC_CHUNKS), 0).astype(F32)
    chunk_expert = jnp.sum(jnp.where(owns, expert, 0.0), axis=0, keepdims=True)
    chunk_off = jnp.sum(jnp.where(owns, region_off, 0.0), axis=0, keepdims=True) + j[0:1] * CHUNK
    total = jnp.broadcast_to(jnp.sum(n_chunks, axis=1, keepdims=True), (1, REC - 2 * REC_CHUNKS))
    rec_ref[0] = jnp.concatenate([chunk_expert, chunk_off, total], axis=1).astype(jnp.int32)
    run_ref[...] = run_ref[...] + jnp.floor((n_tok + (PAIR - 1)) * (1.0 / PAIR)) * PAIR
    cnt_ref[...] = run_ref[...]
    shape = gate_ref.shape
    gate_ref[...] = _pack_cols(gates, shape)
    loc_ref[...] = _pack_cols(locs, shape).astype(jnp.int32)


def _router(rows, h2, w_router_b, router_bias):
    n_rows = rows.tiles * ROW_TILE
    return pl.pallas_call(
        _router_kernel,
        grid=(rows.tiles,),
        in_specs=[_row_spec(D_MODEL), _const_spec((D_MODEL, N_EXPERTS)), _const_spec((1, N_EXPERTS))],
        out_specs=[_row_spec(LANES), _row_spec(LANES), pl.BlockSpec((1, 1, REC), lambda i: (i, 0, 0)),
                   _const_spec((1, N_EXPERTS))],
        out_shape=[jax.ShapeDtypeStruct((n_rows, LANES), F32), jax.ShapeDtypeStruct((n_rows, LANES), jnp.int32),
                   jax.ShapeDtypeStruct((rows.tiles, 1, REC), jnp.int32),
                   jax.ShapeDtypeStruct((1, N_EXPERTS), F32)],
        scratch_shapes=[pltpu.VMEM((1, N_EXPERTS), F32)],
        compiler_params=_cparams(("arbitrary",)),
        name="router",
    )(h2, w_router_b, router_bias)


def _record_copy(i, rec_hbm, rsm, sem_i, base=0):
    base = base if isinstance(base, int) else pl.multiple_of(base, REC)
    return pltpu.make_async_copy(rec_hbm.at[pl.ds(i * REC, REC)], rsm.at[pl.ds(base, REC)], sem_i)


def _load_record(i, rec_hbm, rsm, sem_i, base=0):
    cp = _record_copy(i, rec_hbm, rsm, sem_i, base)
    cp.start()
    cp.wait()


def _move_chunks(ps_ref, rsm, staged_ref, sorted_hbm, sem, chunks, to_sorted, base=0):
    total = rsm[base + 2 * REC_CHUNKS]
    spare0 = sorted_hbm.shape[0] // PACK - SPARE
    for j in chunks:
        placed_tok = jnp.where(j < total, ps_ref[rsm[base + j]] + rsm[base + REC_CHUNKS + j], spare0 + j * CHUNK)
        staged = staged_ref.at[pl.ds(j * CHUNK * PACK, CHUNK * PACK)]
        placed = sorted_hbm.at[_packed_rows(placed_tok, CHUNK, PAIR)]
        src, dst = (staged, placed) if to_sorted else (placed, staged)
        pltpu.make_async_copy(src, dst, sem).start(priority=j % 2)


def _wait_all_chunks(staged_ref, sorted_hbm, sem):
    pltpu.make_async_copy(staged_ref, sorted_hbm.at[pl.ds(0, STAGE * PACK)], sem).wait()


def _staged_block(loc, rb):
    col = lax.broadcasted_iota(jnp.int32, (loc.shape[0], STAGE_BLOCK), 1) + rb * STAGE_BLOCK
    return [loc[:, k:k + 1] == col for k in range(TOP_K)]


def _dispatch_kernel(pe_ref, ps_ref, c2_ref, rec_hbm, loc_ref, h_ref, xs_hbm, rsm, stage, zbuf, sem_i, sem):
    i = pl.program_id(0)
    block_rows = MOE_BLOCK * PACK

    @pl.when(i == 0)
    def _():
        zbuf[...] = jnp.zeros_like(zbuf)
        n_blocks = xs_hbm.shape[0] // block_rows
        n_used = pe_ref[N_EXPERTS - 1] // MOE_BLOCK
        unit_rows = CLEAR_ROWS * PACK
        zunit = zbuf.at[pl.ds(0, unit_rows)]

        def clear_unit(u, carry):
            dst = xs_hbm.at[pl.ds(pl.multiple_of(u * unit_rows, unit_rows), unit_rows)]
            pltpu.make_async_copy(zunit, dst, sem.at[0]).start()
            return carry

        def clear_block(b, carry):
            dst = xs_hbm.at[pl.ds(pl.multiple_of(b * block_rows, block_rows), block_rows)]
            pltpu.make_async_copy(zbuf, dst, sem_i).start()
            return carry

        def clear_padding(e, total):
            first = (ps_ref[e] + c2_ref[e]) // CLEAR_ROWS
            last = pe_ref[e] // CLEAR_ROWS
            lax.fori_loop(first, last, clear_unit, 0)
            return total + last - first

        def unit_wait(e, carry):
            pltpu.make_async_copy(zunit, xs_hbm.at[pl.ds(0, unit_rows)], sem.at[0]).wait()
            return carry

        def block_wait(e, carry):
            pltpu.make_async_copy(zbuf, xs_hbm.at[pl.ds(0, block_rows)], sem_i).wait()
            return carry

        n_units = lax.fori_loop(0, N_EXPERTS, clear_padding, 0)
        lax.fori_loop(n_used, n_blocks, clear_block, 0)
        lax.fori_loop(0, n_units, unit_wait, 0)
        lax.fori_loop(0, n_blocks - n_used, block_wait, 0)

    _load_record(i, rec_hbm, rsm, sem_i)
    slot = i % 2
    loc, hb = loc_ref[...], h_ref[...]

    def stage_block(rb):
        onehot = jnp.zeros((ROW_TILE, STAGE_BLOCK), F32)
        for hit in _staged_block(loc, rb):
            onehot = jnp.where(hit, 1.0, onehot)
        rows = lax.dot_general(onehot.astype(BF16), hb, (((0,), (0,)), ((), ())), preferred_element_type=F32)
        _store_packed(stage.at[slot], rb * STAGE_BLOCK, rows)

    n_blocks, n_chunks = STAGE // STAGE_BLOCK, STAGE // CHUNK
    n_early = n_blocks // 2 + 1
    for rb in range(n_early):
        stage_block(rb)

    @pl.when(i >= 1)
    def _():
        _wait_all_chunks(stage.at[1 - slot], xs_hbm, sem.at[1 - slot])

    per_round = -(-n_chunks // (n_blocks - n_early))
    for r, rb in enumerate(range(n_early, n_blocks)):
        stage_block(rb)
        chunks = range(r * per_round, min((r + 1) * per_round, n_chunks))
        assert (chunks[-1] * CHUNK) // STAGE_BLOCK <= rb
        _move_chunks(ps_ref, rsm, stage.at[slot], xs_hbm, sem.at[slot], chunks, to_sorted=True)

    @pl.when(i == pl.num_programs(0) - 1)
    def _():
        _wait_all_chunks(stage.at[slot], xs_hbm, sem.at[slot])


def _dispatch(rows, pad_end, pad_start, counts, rec_flat, locs, h2, n_rows_sorted):
    return pl.pallas_call(
        _dispatch_kernel,
        grid_spec=pltpu.PrefetchScalarGridSpec(
            num_scalar_prefetch=3,
            grid=(rows.tiles,),
            in_specs=[pl.BlockSpec(memory_space=pl.ANY),
                      pl.BlockSpec((ROW_TILE, LANES), lambda i, pe, ps, c2: (i, 0)),
                      pl.BlockSpec((ROW_TILE, D_MODEL), lambda i, pe, ps, c2: (i, 0))],
            out_specs=pl.BlockSpec(memory_space=pl.ANY),
            scratch_shapes=[pltpu.SMEM((REC,), jnp.int32),
                            pltpu.VMEM((2, STAGE * PACK, LANES), jnp.uint32),
                            pltpu.VMEM((MOE_BLOCK * PACK, LANES), jnp.uint32),
                            pltpu.SemaphoreType.DMA, pltpu.SemaphoreType.DMA((2,))]),
        out_shape=jax.ShapeDtypeStruct(((n_rows_sorted + SPARE) * PACK, LANES), jnp.uint32),
        compiler_params=_cparams(("arbitrary",)),
        name="moe_dispatch",
    )(pad_end, pad_start, counts, rec_flat, locs, h2)


def _expert_kernel(pe_ref, ps_ref, xs_hbm, w1_ref, w3_ref, w2_ref, ys_hbm, xbuf, ybuf, w1b, w3b, w2b,
                   sem_in, sem_out):
    e = pl.program_id(0)
    first, last = ps_ref[e] // MOE_BLOCK, pe_ref[e] // MOE_BLOCK
    n_used = pe_ref[N_EXPERTS - 1] // MOE_BLOCK
    block_rows = MOE_BLOCK * PACK

    def rows_of(g):
        return pl.ds(pl.multiple_of(g * block_rows, block_rows), block_rows)

    def copy_in(g):
        slot = g % EXPERT_RING
        return pltpu.make_async_copy(xs_hbm.at[rows_of(g)], xbuf.at[slot], sem_in.at[slot])

    def copy_out(g):
        slot = g % EXPERT_RING
        return pltpu.make_async_copy(ybuf.at[slot], ys_hbm.at[rows_of(g)], sem_out.at[slot])

    @pl.when(e == 0)
    def _():
        for g in range(EXPERT_AHEAD):
            @pl.when(g < n_used)
            def _():
                copy_in(g).start()

    w1b[...] = w1_ref[...].astype(BF16)
    w3b[...] = w3_ref[...].astype(BF16)
    w2b[...] = w2_ref[...].astype(BF16)

    def block(g, carry):
        copy_in(g).wait()

        @pl.when(g + EXPERT_AHEAD < n_used)
        def _():
            copy_in(g + EXPERT_AHEAD).start()

        @pl.when(g >= EXPERT_RING)
        def _():
            copy_out(g - EXPERT_RING).wait()

        slot = g % EXPERT_RING
        parts = range(0, MOE_BLOCK, STAGE_BLOCK)
        xb = jnp.concatenate([_load_packed(xbuf.at[slot], t0, STAGE_BLOCK) for t0 in parts], axis=0)
        hid = (_silu(_dot(xb, w1b[...])) * _dot(xb, w3b[...])).astype(BF16)
        y = _dot(hid, w2b[...]).astype(BF16).astype(F32)
        for t0 in parts:
            _store_packed(ybuf.at[slot], t0, y[t0:t0 + STAGE_BLOCK])
        copy_out(g).start()
        return carry

    lax.fori_loop(first, last, block, 0)

    @pl.when(e == N_EXPERTS - 1)
    def _():
        for back in range(EXPERT_RING, 0, -1):
            @pl.when(n_used >= back)
            def _():
                copy_out(n_used - back).wait()


def _experts(pad_end, pad_start, xs, w1, w3, w2):
    expert_spec = lambda shape: pl.BlockSpec((None,) + shape, lambda e, pe, ps: (e, 0, 0))
    any_spec = pl.BlockSpec(memory_space=pl.ANY)
    return pl.pallas_call(
        _expert_kernel,
        grid_spec=pltpu.PrefetchScalarGridSpec(
            num_scalar_prefetch=2,
            grid=(N_EXPERTS,),
            in_specs=[any_spec, expert_spec((D_MODEL, EXPERT_HIDDEN)), expert_spec((D_MODEL, EXPERT_HIDDEN)),
                      expert_spec((EXPERT_HIDDEN, D_MODEL))],
            out_specs=any_spec,
            scratch_shapes=[pltpu.VMEM((EXPERT_RING, MOE_BLOCK * PACK, LANES), jnp.uint32),
                            pltpu.VMEM((EXPERT_RING, MOE_BLOCK * PACK, LANES), jnp.uint32),
                            pltpu.VMEM((D_MODEL, EXPERT_HIDDEN), BF16),
                            pltpu.VMEM((D_MODEL, EXPERT_HIDDEN), BF16),
                            pltpu.VMEM((EXPERT_HIDDEN, D_MODEL), BF16),
                            pltpu.SemaphoreType.DMA((EXPERT_RING,)), pltpu.SemaphoreType.DMA((EXPERT_RING,))]),
        out_shape=jax.ShapeDtypeStruct(xs.shape, jnp.uint32),
        input_output_aliases={2: 0},
        compiler_params=_cparams(("arbitrary",)),
        name="moe_experts",
    )(pad_end, pad_start, xs, w1, w3, w2)


def _combine_kernel(rows, ps_ref, rec_hbm, ys_hbm, loc_ref, x1_ref, h_ref, gate_ref, mod_ref, ws1_ref, ws3_ref,
                    ws2_ref, nf_ref, outp_ref, outs_ref, rsm, ybuf, sem_i, sem):
    i = pl.program_id(0)
    slot = i % 2

    n_chunks = STAGE // CHUNK
    nxt = jnp.minimum(i + 1, pl.num_programs(0) - 1)
    other = 1 - slot

    def fetch(s, chunks):
        _move_chunks(ps_ref, rsm, ybuf.at[s], ys_hbm, sem.at[s], chunks, to_sorted=False, base=s * REC)

    next_record = _record_copy(nxt, rec_hbm, rsm, sem_i.at[other], base=other * REC)
    next_record.start()

    @pl.when(i == 0)
    def _():
        _load_record(i, rec_hbm, rsm, sem_i.at[slot], base=slot * REC)
        fetch(slot, range(n_chunks))

    hb = h_ref[...]
    shared = _dot((_silu(_dot(hb, ws1_ref[...])) * _dot(hb, ws3_ref[...])).astype(BF16), ws2_ref[...])
    staged_y = ybuf.at[slot]
    _wait_all_chunks(staged_y, ys_hbm, sem.at[slot])
    next_record.wait()
    loc, g = loc_ref[...], gate_ref[...]
    routed = jnp.zeros((ROW_TILE, D_MODEL), F32)
    for rb in range(STAGE // STAGE_BLOCK):
        gm = jnp.zeros((ROW_TILE, STAGE_BLOCK), F32)
        for k, hit in enumerate(_staged_block(loc, rb)):
            gm = jnp.where(hit, g[:, k:k + 1], gm)
        g_hi = gm.astype(BF16)
        g_lo = (gm - g_hi.astype(F32)).astype(BF16)
        yb = _load_packed(staged_y, rb * STAGE_BLOCK, STAGE_BLOCK)
        routed = routed + _dot(g_hi, yb) + _dot(g_lo, yb)
        per_block = STAGE_BLOCK // CHUNK
        fetch(other, range(rb * per_block, (rb + 1) * per_block))
    gate2 = mod_ref[0][:, 5 * D_MODEL:6 * D_MODEL]
    x2 = x1_ref[...] + gate2 * (routed + shared)
    out = _rms(x2, nf_ref[...])

    @pl.when(i == pl.num_programs(0) - 1)
    def _():
        _wait_all_chunks(ybuf.at[other], ys_hbm, sem.at[other])

    @pl.when(i < rows.tiles_ctx)
    def _():
        outp_ref[...] = out

    @pl.when(i >= rows.tiles_ctx)
    def _():
        outs_ref[...] = out


def _combine(rows, pad_start, rec_flat, ys, locs, x1, h2, gates, mod3, ws1_b, ws3_b, ws2_b, normf_g):
    drop = lambda spec: pl.BlockSpec(spec.block_shape, lambda i, ps, f=spec.index_map: f(i))
    any_spec = pl.BlockSpec(memory_space=pl.ANY)
    return pl.pallas_call(
        functools.partial(_combine_kernel, rows),
        grid_spec=pltpu.PrefetchScalarGridSpec(
            num_scalar_prefetch=1,
            grid=(rows.tiles,),
            in_specs=[any_spec, any_spec]
                     + [drop(s) for s in (
                         _row_spec(LANES), _row_spec(D_MODEL), _row_spec(D_MODEL), _row_spec(LANES),
                         _mod_spec(rows),
                         _const_spec((D_MODEL, EXPERT_HIDDEN)), _const_spec((D_MODEL, EXPERT_HIDDEN)),
                         _const_spec((EXPERT_HIDDEN, D_MODEL)), _const_spec((1, D_MODEL)))],
            out_specs=[drop(_row_spec(D_MODEL, rows.ctx_tile)), drop(_row_spec(D_MODEL, rows.lat_tile))],
            scratch_shapes=[pltpu.SMEM((2 * REC,), jnp.int32),
                            pltpu.VMEM((2, STAGE * PACK, LANES), jnp.uint32),
                            pltpu.SemaphoreType.DMA((2,)), pltpu.SemaphoreType.DMA((2,))]),
        out_shape=[jax.ShapeDtypeStruct((rows.tiles_ctx * ROW_TILE, D_MODEL), F32),
                   jax.ShapeDtypeStruct((rows.tiles_lat * ROW_TILE, D_MODEL), F32)],
        compiler_params=_cparams(("arbitrary",)),
        name="moe_combine",
    )(pad_start, rec_flat, ys, locs, x1, h2, gates, mod3, ws1_b, ws3_b, ws2_b, normf_g)


def kernel(x_prompt, x_sample, cache_k, cache_v, c, c_ctx, w_mod, b_mod, norm1_g, w_in, lambda_qk, subln_g,
           w_attn_proj, conv_w, conv_b, conv_ln_g, conv_ln_b, w_conv_out, w_out, norm2_g, w_router,
           router_bias, w1, w3, w2, ws1, ws3, ws2, normf_g):
    batch, seq, _ = x_prompt.shape
    dec_batch, dec_seq, _ = x_sample.shape
    past = cache_k.shape[2]
    l = 0
    rows_p, rows_s = batch * seq, dec_batch * dec_seq
    n_rows = rows_p + rows_s
    rows = Rows(rows_p // ROW_TILE, rows_s // ROW_TILE, seq // ROW_TILE, dec_seq // ROW_TILE)
    xp = x_prompt.reshape(rows_p, D_MODEL)
    xs = x_sample.reshape(rows_s, D_MODEL)
    row = lambda a: a.reshape(1, -1)

    cc = jnp.zeros((MOD_ROWS, D_MODEL), F32).at[0].set(c_ctx).at[1:1 + dec_batch].set(c)
    mod3 = _mod_table(cc, w_mod[l], b_mod[l]).reshape(MOD_ROWS, 1, N_MOD * D_MODEL)

    q, k, v, z, gt, state_k, state_v = _inproj(rows, xp, xs, mod3, row(norm1_g[l]), w_in[l].astype(BF16),
                                               _rope_tables(dec_seq))
    lq, sg = lambda_qk[l], row(subln_g[l])
    on_p = _attention(q, k, v, lq, sg, 0, batch, seq, CTX_HEADS_PER_STEP)
    cache = (cache_k[:, l].reshape(dec_batch, past, ATTN_W), cache_v[:, l].reshape(dec_batch, past, ATTN_W))
    on_s = _attention(q, k, v, lq, sg, rows_p, dec_batch, dec_seq, LAT_HEADS_PER_STEP, cache=cache)
    x1, h2 = _merge(rows, xp, xs, on_p, on_s, z, gt, mod3, conv_w[l], row(conv_b[l]), row(conv_ln_g[l]),
                    row(conv_ln_b[l]), w_conv_out[l].astype(BF16), w_attn_proj[l].astype(BF16),
                    w_out[l].astype(BF16), row(norm2_g[l]))

    gates, locs, rec, counts = _router(rows, h2, w_router[l].astype(BF16), row(router_bias[l]))
    counts = counts[0].astype(jnp.int32)
    padded = (counts + CHUNK + MOE_BLOCK - 1) // MOE_BLOCK * MOE_BLOCK
    pad_end = jnp.cumsum(padded).astype(jnp.int32)
    pad_start = pad_end - padded
    max_rows = n_rows * TOP_K + rows.tiles * N_EXPERTS * (PAIR - 1) + N_EXPERTS * (CHUNK + MOE_BLOCK - 1)
    n_blocks = -(-max_rows // MOE_BLOCK)
    rec_flat = rec.reshape(-1)
    x_sorted = _dispatch(rows, pad_end, pad_start, counts, rec_flat, locs, h2, n_blocks * MOE_BLOCK)
    y_sorted = _experts(pad_end, pad_start, x_sorted, w1[l], w3[l], w2[l])
    y_p, y_s = _combine(rows, pad_start, rec_flat, y_sorted, locs, x1, h2, gates, mod3,
                        ws1[l].astype(BF16), ws3[l].astype(BF16), ws2[l].astype(BF16), row(normf_g))
    return (y_p.reshape(batch, seq, D_MODEL), y_s.reshape(dec_batch, dec_seq, D_MODEL),
            state_k.reshape(batch, 1, seq, N_HEADS, 2, HEAD_DIM),
            state_v.reshape(batch, 1, seq, N_HEADS, 2 * HEAD_DIM))
```
